```python
import math
import jax, jax.numpy as jnp
from jax import lax
import numpy as np

D_MODEL = 1024
BATCH = 8
SEQ = 4096
DEPTH = 2

GRID_W = 64
PLE_DIM = 256

HEAD_DIM = 64
ROPE_THETA = 10000.0
Q_BLOCK = 128
EPS = 1e-6

HA = 6
Q_LORA = 256
KV_LORA = 128
NOPE_A = 64
ROPE_A = 32
V_A = 64
A_COLS = Q_LORA + KV_LORA + ROPE_A

HB = 6
KVB = 2
B_COLS = (HB + 2 * KVB) * HEAD_DIM

HC = 4
KVC = 2
WINDOW = 128
C_COLS = (HC + 2 * KVC) * HEAD_DIM

IN_COLS = A_COLS + B_COLS + C_COLS
MIX_WIDTH = HA * V_A + HB * HEAD_DIM + HC * HEAD_DIM

N_EXPERTS = 32
TOP_K = 4
D_FF = D_MODEL
SWIGLU_LIMIT = 7.0
SWIGLU_ALPHA = 1.702

kernel_name = "hybrid_mla_axialgqa_swa_sink_moe_encoder"


def rms_norm(x, g):
    xf = x.astype(jnp.float32)
    y = xf * lax.rsqrt(jnp.mean(xf * xf, axis=-1, keepdims=True) + EPS)
    return (y * g.astype(jnp.float32)).astype(x.dtype)


def rope_cos_sin(pos, dim):
    inv = 1.0 / (ROPE_THETA ** (jnp.arange(0, dim, 2, dtype=jnp.float32) / dim))
    ang = pos.astype(jnp.float32)[:, None] * inv[None, :]
    return jnp.cos(ang), jnp.sin(ang)


def apply_rope(x, cos, sin):
    xf = x.astype(jnp.float32)
    half = x.shape[-1] // 2
    x1, x2 = xf[..., :half], xf[..., half:]
    c, s = cos[None, :, None, :], sin[None, :, None, :]
    return jnp.concatenate([x1 * c - x2 * s, x2 * c + x1 * s], axis=-1).astype(x.dtype)


def apply_axial_rope(x, cos_r, sin_r, cos_w, sin_w):
    half = x.shape[-1] // 2
    return jnp.concatenate([apply_rope(x[..., :half], cos_r, sin_r),
                            apply_rope(x[..., half:], cos_w, sin_w)], axis=-1)


def dense_block_attention(q, k, v, scale):
    B, S, H, dk = q.shape
    G, dv = k.shape[2], v.shape[-1]
    R = H // G
    nb = S // Q_BLOCK
    qb = q.reshape(B, nb, Q_BLOCK, G, R, dk).transpose(1, 0, 2, 3, 4, 5)

    def one_block(qblk):
        s = jnp.einsum('bqgrd,bkgd->bgrqk', qblk, k).astype(jnp.float32) * scale
        pr = jax.nn.softmax(s, axis=-1).astype(v.dtype)
        return jnp.einsum('bgrqk,bkgd->bqgrd', pr, v)

    o = lax.map(one_block, qb)
    return o.transpose(1, 0, 2, 3, 4, 5).reshape(B, S, H, dv)


def banded_sink_attention(q, k, v, sink, scale):
    B, S, H, d = q.shape
    G = k.shape[2]
    R = H // G
    nb = S // WINDOW
    pad = ((0, 0), (WINDOW, WINDOW), (0, 0), (0, 0))
    kp = jnp.pad(k, pad).reshape(B, nb + 2, WINDOW, G, d)
    vp = jnp.pad(v, pad).reshape(B, nb + 2, WINDOW, G, d)
    kw = jnp.concatenate([kp[:, :-2], kp[:, 1:-1], kp[:, 2:]], axis=2)
    vw = jnp.concatenate([vp[:, :-2], vp[:, 1:-1], vp[:, 2:]], axis=2)
    qb = q.reshape(B, nb, WINDOW, G, R, d)
    s = jnp.einsum('bnqgrd,bnkgd->bngrqk', qb, kw).astype(jnp.float32) * scale
    rel = jnp.arange(3 * WINDOW)[None, :] - jnp.arange(WINDOW)[:, None]
    band = (rel >= 0) & (rel <= 2 * WINDOW)
    kpos = (jnp.arange(nb)[:, None] - 1) * WINDOW + jnp.arange(3 * WINDOW)[None, :]
    inb = (kpos >= 0) & (kpos < S)
    mask = band[None, :, :] & inb[:, None, :]
    s = jnp.where(mask[None, :, None, None, :, :], s, -1e30)
    sk = sink.astype(jnp.float32).reshape(1, 1, G, R, 1, 1)
    m = jnp.maximum(jnp.max(s, axis=-1, keepdims=True), sk)
    e = jnp.exp(s - m)
    pr = e / (jnp.sum(e, axis=-1, keepdims=True) + jnp.exp(sk - m))
    o = jnp.einsum('bngrqk,bnkgd->bnqgrd', pr.astype(v.dtype), vw)
    return o.reshape(B, S, H, d)


def mla_mixer(z, q_norm, wq_up, kv_norm, wkv_up, cos, sin):
    B, S, _ = z.shape
    c_q = rms_norm(z[..., :Q_LORA], q_norm)
    c_kv = rms_norm(z[..., Q_LORA:Q_LORA + KV_LORA], kv_norm)
    k_rope = z[..., Q_LORA + KV_LORA:].reshape(B, S, 1, ROPE_A)
    q = (c_q @ wq_up).reshape(B, S, HA, NOPE_A + ROPE_A)
    q = jnp.concatenate([q[..., :NOPE_A], apply_rope(q[..., NOPE_A:], cos, sin)], axis=-1)
    kv = (c_kv @ wkv_up).reshape(B, S, HA, NOPE_A + V_A)
    k_rope = jnp.broadcast_to(apply_rope(k_rope, cos, sin), (B, S, HA, ROPE_A))
    k = jnp.concatenate([kv[..., :NOPE_A], k_rope], axis=-1)
    v = kv[..., NOPE_A:]
    o = dense_block_attention(q, k, v, (NOPE_A + ROPE_A) ** -0.5)
    return o.reshape(B, S, HA * V_A)


def axial_gqa_mixer(z, q_norm, k_norm, cos_r, sin_r, cos_w, sin_w):
    B, S, _ = z.shape
    nq, nk = HB * HEAD_DIM, KVB * HEAD_DIM
    q = rms_norm(z[..., :nq].reshape(B, S, HB, HEAD_DIM), q_norm)
    k = rms_norm(z[..., nq:nq + nk].reshape(B, S, KVB, HEAD_DIM), k_norm)
    v = z[..., nq + nk:].reshape(B, S, KVB, HEAD_DIM)
    q = apply_axial_rope(q, cos_r, sin_r, cos_w, sin_w)
    k = apply_axial_rope(k, cos_r, sin_r, cos_w, sin_w)
    o = dense_block_attention(q, k, v, HEAD_DIM ** -0.5)
    return o.reshape(B, S, HB * HEAD_DIM)


def window_sink_mixer(z, sink, cos, sin):
    B, S, _ = z.shape
    nq, nk = HC * HEAD_DIM, KVC * HEAD_DIM
    q = apply_rope(z[..., :nq].reshape(B, S, HC, HEAD_DIM), cos, sin)
    k = apply_rope(z[..., nq:nq + nk].reshape(B, S, KVC, HEAD_DIM), cos, sin)
    v = z[..., nq + nk:].reshape(B, S, KVC, HEAD_DIM)
    o = banded_sink_attention(q, k, v, sink, HEAD_DIM ** -0.5)
    return o.reshape(B, S, HC * HEAD_DIM)


def moe(h, router_w, router_b, w_up, b_up, w_down, b_down):
    B, S, D = h.shape
    T = B * S
    hf = h.reshape(T, D)
    logits = (hf @ router_w + router_b).astype(jnp.float32)
    top_vals, top_idx = lax.top_k(logits, TOP_K)
    gates = jax.nn.softmax(top_vals, axis=-1).astype(h.dtype)
    flat_e = top_idx.reshape(-1)
    order = jnp.argsort(flat_e)
    sorted_e = flat_e[order]
    tok = order // TOP_K
    xs = hf[tok]
    group_sizes = jnp.bincount(flat_e, length=N_EXPERTS).astype(jnp.int32)
    gu = lax.ragged_dot(xs, w_up, group_sizes) + b_up[sorted_e]
    x_glu = jnp.minimum(gu[..., :D_FF], SWIGLU_LIMIT)
    x_lin = jnp.clip(gu[..., D_FF:], -SWIGLU_LIMIT, SWIGLU_LIMIT)
    act = x_glu * jax.nn.sigmoid(SWIGLU_ALPHA * x_glu) * (x_lin + 1.0)
    out = lax.ragged_dot(act, w_down, group_sizes) + b_down[sorted_e]
    out = out * gates.reshape(-1)[order][:, None]
    y = jax.ops.segment_sum(out, tok, num_segments=T)
    return y.reshape(B, S, D)


def setup_inputs(seed: int = 0) -> dict:
    key = jax.random.key(seed)
    ks = jax.random.split(key, 24)
    f32 = jnp.float32

    def nrm(k, shape, scale):
        return jax.random.normal(k, shape, f32) * scale

    def gain(k, shape):
        return 1.0 + 0.02 * jax.random.normal(k, shape, f32)

    L, D = DEPTH, D_MODEL
    return {
        "x": nrm(ks[0], (BATCH, SEQ, D), 1.0),
        "p": nrm(ks[1], (DEPTH, BATCH, SEQ, PLE_DIM), 1.0),
        "attn_norm": gain(ks[2], (L, D)),
        "w_in": nrm(ks[3], (L, D, IN_COLS), D ** -0.5),
        "mla_q_norm": gain(ks[4], (L, Q_LORA)),
        "mla_wq_up": nrm(ks[5], (L, Q_LORA, HA * (NOPE_A + ROPE_A)), Q_LORA ** -0.5),
        "mla_kv_norm": gain(ks[6], (L, KV_LORA)),
        "mla_wkv_up": nrm(ks[7], (L, KV_LORA, HA * (NOPE_A + V_A)), KV_LORA ** -0.5),
        "gqa_q_norm": gain(ks[8], (L, HEAD_DIM)),
        "gqa_k_norm": gain(ks[9], (L, HEAD_DIM)),
        "swa_sink": nrm(ks[10], (L, HC), 0.5),
        "w_out": nrm(ks[11], (L, MIX_WIDTH, D), MIX_WIDTH ** -0.5),
        "moe_norm": gain(ks[12], (L, D)),
        "router_w": nrm(ks[13], (L, D, N_EXPERTS), D ** -0.5),
        "router_b": nrm(ks[14], (L, N_EXPERTS), 0.01),
        "w_up": nrm(ks[15], (L, N_EXPERTS, D, 2 * D_FF), D ** -0.5),
        "b_up": nrm(ks[16], (L, N_EXPERTS, 2 * D_FF), 0.01),
        "w_down": nrm(ks[17], (L, N_EXPERTS, D_FF, D), D_FF ** -0.5),
        "b_down": nrm(ks[18], (L, N_EXPERTS, D), 0.01),
        "ple_norm": gain(ks[19], (L, D)),
        "w_ple": nrm(ks[20], (L, PLE_DIM, D), PLE_DIM ** -0.5),
        "w_ple_gate": nrm(ks[21], (L, D, D), D ** -0.5),
        "final_norm": gain(ks[22], (D,)),
    }


def reference(x, p, attn_norm, w_in, mla_q_norm, mla_wq_up, mla_kv_norm, mla_wkv_up,
              gqa_q_norm, gqa_k_norm, swa_sink, w_out, moe_norm, router_w, router_b,
              w_up, b_up, w_down, b_down, ple_norm, w_ple, w_ple_gate, final_norm):
    B, S, _ = x.shape
    ROWS = S // GRID_W
    pos = jnp.arange(S, dtype=jnp.int32)
    cos_a, sin_a = rope_cos_sin(pos, ROPE_A)
    cos_c, sin_c = rope_cos_sin(pos, HEAD_DIM)
    rows = jnp.repeat(jnp.arange(ROWS, dtype=jnp.int32), GRID_W)
    cols = jnp.tile(jnp.arange(GRID_W, dtype=jnp.int32), ROWS)
    cos_r, sin_r = rope_cos_sin(rows, HEAD_DIM // 2)
    cos_w, sin_w = rope_cos_sin(cols, HEAD_DIM // 2)

    for i in range(DEPTH):
        h = rms_norm(x, attn_norm[i])
        z = h @ w_in[i]
        z_a, z_b, z_c = jnp.split(z, [A_COLS, A_COLS + B_COLS], axis=-1)
        o_a = mla_mixer(z_a, mla_q_norm[i], mla_wq_up[i], mla_kv_norm[i], mla_wkv_up[i], cos_a, sin_a)
        o_b = axial_gqa_mixer(z_b, gqa_q_norm[i], gqa_k_norm[i], cos_r, sin_r, cos_w, sin_w)
        o_c = window_sink_mixer(z_c, swa_sink[i], cos_c, sin_c)
        mix = jnp.concatenate([o_a, o_b, o_c], axis=-1)
        x = x + mix @ w_out[i]
        x = x + moe(rms_norm(x, moe_norm[i]), router_w[i], router_b[i],
                    w_up[i], b_up[i], w_down[i], b_down[i])
        gate = jax.nn.sigmoid(rms_norm(x, ple_norm[i]) @ w_ple_gate[i])
        x = x + (p[i] @ w_ple[i]) * gate
    return rms_norm(x, final_norm)
```

```python
import functools
import math

import numpy as np
import jax
import jax.numpy as jnp
from jax import lax
from jax.experimental import pallas as pl
from jax.experimental.pallas import tpu as pltpu

F32 = jnp.float32
BF16 = jnp.bfloat16

D_MODEL = 1024
BATCH = 8
SEQ = 4096
DEPTH = 2
TOKENS = BATCH * SEQ
GRID_W = 64
PLE_DIM = 256
HEAD_DIM = 64
ROPE_THETA = 10000.0
EPS = 1e-6
HA, Q_LORA, KV_LORA, NOPE_A, ROPE_A, V_A = 6, 256, 128, 64, 32, 64
HB, KVB = 6, 2
HC, KVC, WINDOW = 4, 2, 128
A_COLS = Q_LORA + KV_LORA + ROPE_A
B_COLS = (HB + 2 * KVB) * HEAD_DIM
C_COLS = (HC + 2 * KVC) * HEAD_DIM
N_EXPERTS = 32
TOP_K = 4
D_FF = D_MODEL
SWIGLU_LIMIT = 7.0
SWIGLU_ALPHA = 1.702

LANES = 128
HALF = LANES // 2

TM_IN = 512
TQ = 256
TK = 512
TM_OUT = 512
TM_DISPATCH = 512
TM_EXPERT = 512
TM_COMBINE = 256
N_EXPERT_TILES = TOKENS * TOP_K // TM_EXPERT + N_EXPERTS
N_SORTED_ROWS = N_EXPERT_TILES * TM_EXPERT
VMEM_LIMIT = 52 * 1024 * 1024

Z_CQ = 0
Z_CKV = Q_LORA
Z_KROPE = Z_CKV + KV_LORA
Z_BQ = Z_KROPE + LANES
Z_BK = Z_BQ + 3 * LANES
Z_BV = Z_BK + LANES
Z_CQS = Z_BV + LANES
Z_CK = Z_CQS + 2 * LANES
Z_CV = Z_CK + LANES
Z_COLS = Z_CV + LANES

NEG_BIG = -1e30
LOG2E = math.log2(math.e)


def _rms(x, g):
    return x * lax.rsqrt(jnp.mean(x * x, axis=-1, keepdims=True) + EPS) * g


def _lane_is_low(shape):
    return lax.broadcasted_iota(jnp.int32, shape, len(shape) - 1) < HALF


def _rms_per_half(xs, g):
    low = _lane_is_low(xs.shape)
    x2 = xs * xs
    s_lo = jnp.sum(jnp.where(low, x2, 0.0), axis=-1, keepdims=True)
    s_hi = jnp.sum(jnp.where(low, 0.0, x2), axis=-1, keepdims=True)
    ms = jnp.where(low, s_lo, s_hi) * (1.0 / HEAD_DIM)
    return xs * lax.rsqrt(ms + EPS) * g


def _in_proj_kernel(x_ref, g_ref, win_ref, gq_ref, wq_ref, gkv_ref, wkv_ref, gbq_ref, gbk_ref, tab_ref,
                    qa_ref, ka_ref, va_ref, qb_ref, kb_ref, vb_ref, qc_ref, kc_ref, vc_ref):
    h = _rms(x_ref[0], g_ref[...])
    z = jnp.dot(h.astype(BF16), win_ref[...], preferred_element_type=F32)
    tm = z.shape[0]
    ones = jnp.ones((tm, LANES), BF16)
    low = _lane_is_low((tm, LANES))

    def rope(xs, table, shift):
        base = table * 3 * LANES
        c = tab_ref[:, base:base + LANES]
        s_up = tab_ref[:, base + LANES:base + 2 * LANES]
        s_dn = tab_ref[:, base + 2 * LANES:base + 3 * LANES]
        return xs * c + pltpu.roll(xs, LANES - shift, 1) * s_up + pltpu.roll(xs, shift, 1) * s_dn

    c_q = _rms(z[:, Z_CQ:Z_CQ + Q_LORA], gq_ref[...])
    q = jnp.dot(c_q.astype(BF16), wq_ref[...], preferred_element_type=F32)
    c_kv = _rms(z[:, Z_CKV:Z_CKV + KV_LORA], gkv_ref[...])
    kv = jnp.dot(c_kv.astype(BF16), wkv_ref[...], preferred_element_type=F32)
    k_rope = rope(z[:, Z_KROPE:Z_KROPE + LANES], 0, ROPE_A // 2)
    for hd in range(HA):
        qa_ref[0, hd] = rope(q[:, hd * LANES:(hd + 1) * LANES], 0, ROPE_A // 2).astype(BF16)
        ka_ref[0, hd] = (kv[:, hd * LANES:(hd + 1) * LANES] + k_rope).astype(BF16)
    for j in range(HA // 2):
        va_ref[0, j, :, 0:LANES] = kv[:, (HA + j) * LANES:(HA + j + 1) * LANES].astype(BF16)
        va_ref[0, j, :, LANES:2 * LANES] = ones

    for j in range(HB // 2):
        s = rope(_rms_per_half(z[:, Z_BQ + j * LANES:Z_BQ + (j + 1) * LANES], gbq_ref[...]), 1, HEAD_DIM // 4)
        qb_ref[0, j] = jnp.where(low, s, 0.0).astype(BF16)
        qb_ref[0, HB // 2 + j] = jnp.where(low, 0.0, s).astype(BF16)
    kb_ref[0, 0] = rope(_rms_per_half(z[:, Z_BK:Z_BK + LANES], gbk_ref[...]), 1, HEAD_DIM // 4).astype(BF16)
    vb_ref[0, 0, :, 0:LANES] = z[:, Z_BV:Z_BV + LANES].astype(BF16)
    vb_ref[0, 0, :, LANES:2 * LANES] = ones

    for j in range(HC // 2):
        s = rope(z[:, Z_CQS + j * LANES:Z_CQS + (j + 1) * LANES], 2, HEAD_DIM // 2)
        qc_ref[0, j] = jnp.where(low, s, 0.0).astype(BF16)
        qc_ref[0, HC // 2 + j] = jnp.where(low, 0.0, s).astype(BF16)
    kc_ref[0, 0] = rope(z[:, Z_CK:Z_CK + LANES], 2, HEAD_DIM // 2).astype(BF16)
    vc_ref[0, 0, :, 0:LANES] = z[:, Z_CV:Z_CV + LANES].astype(BF16)
    vc_ref[0, 0, :, LANES:2 * LANES] = ones


def _in_proj(x3, g, win, gq, wq, gkv, wkv, gbq, gbk, tabs):
    nst = SEQ // TM_IN
    const2 = lambda b, s: (0, 0)
    head_out = lambda n, w: (jax.ShapeDtypeStruct((BATCH, n, SEQ, w), BF16),
                             pl.BlockSpec((1, n, TM_IN, w), lambda b, s: (b, 0, s, 0)))
    outs = [head_out(HA, LANES), head_out(HA, LANES), head_out(HA // 2, 2 * LANES),
            head_out(HB, LANES), head_out(1, LANES), head_out(1, 2 * LANES),
            head_out(HC, LANES), head_out(1, LANES), head_out(1, 2 * LANES)]
    return pl.pallas_call(
        _in_proj_kernel,
        grid=(BATCH, nst),
        in_specs=[
            pl.BlockSpec((1, TM_IN, D_MODEL), lambda b, s: (b, s, 0)),
            pl.BlockSpec((1, D_MODEL), const2),
            pl.BlockSpec((D_MODEL, Z_COLS), const2),
            pl.BlockSpec((1, Q_LORA), const2),
            pl.BlockSpec((Q_LORA, HA * LANES), const2),
            pl.BlockSpec((1, KV_LORA), const2),
            pl.BlockSpec((KV_LORA, HA * LANES + (HA // 2) * LANES), const2),
            pl.BlockSpec((1, LANES), const2),
            pl.BlockSpec((1, LANES), const2),
            pl.BlockSpec((TM_IN, 9 * LANES), lambda b, s: (s, 0)),
        ],
        out_specs=[o[1] for o in outs],
        out_shape=[o[0] for o in outs],
        compiler_params=pltpu.CompilerParams(
            dimension_semantics=("arbitrary", "arbitrary"), vmem_limit_bytes=VMEM_LIMIT),
        name="in_proj",
    )(x3, g, win, gq, wq, gkv, wkv, gbq, gbk, tabs)


def _attn_kernel(q1_ref, q2_ref, k1_ref, k2_ref, v_ref, o_ref, *, c):
    tq = q1_ref.shape[2]
    q1 = q1_ref[0, 0]
    q2 = q2_ref[0, 0]

    def one_head(q, k, v, m, acc):
        s = lax.dot_general(q, k, (((1,), (1,)), ((), ())), preferred_element_type=F32)
        m_new = jnp.maximum(m, jnp.max(s, axis=-1, keepdims=True))
        p = jnp.exp2((s - m_new) * c)
        alpha = jnp.exp2((m - m_new) * c)
        acc = acc * alpha + jnp.dot(p.astype(BF16), v, preferred_element_type=F32)
        return m_new, acc

    def body(j, carry):
        m1, a1, m2, a2 = carry
        ks = pl.multiple_of(j * TK, TK)
        v = v_ref[0, 0, pl.ds(ks, TK), :]
        m1, a1 = one_head(q1, k1_ref[0, 0, pl.ds(ks, TK), :], v, m1, a1)
        m2, a2 = one_head(q2, k2_ref[0, 0, pl.ds(ks, TK), :], v, m2, a2)
        return m1, a1, m2, a2

    m0 = jnp.full((tq, 1), NEG_BIG, F32)
    a0 = jnp.zeros((tq, 2 * LANES), F32)
    _, a1, _, a2 = lax.fori_loop(0, SEQ // TK, body, (m0, a0, m0, a0))
    low = _lane_is_low((tq, LANES))
    num = jnp.where(low, a1[:, :LANES], a2[:, :LANES])
    den = jnp.where(low, a1[:, LANES:], a2[:, LANES:])
    o_ref[0] = (num / den).astype(o_ref.dtype)


def _dense_attention(q, k, vext, n_pairs, head_maps, scale, name):
    h1, h2, g1, g2, pv = head_maps
    kernel = functools.partial(_attn_kernel, c=scale * LOG2E)
    return pl.pallas_call(
        kernel,
        grid=(BATCH, n_pairs, SEQ // TQ),
        in_specs=[
            pl.BlockSpec((1, 1, TQ, LANES), lambda b, p, i: (b, h1(p), i, 0)),
            pl.BlockSpec((1, 1, TQ, LANES), lambda b, p, i: (b, h2(p), i, 0)),
            pl.BlockSpec((1, 1, SEQ, LANES), lambda b, p, i: (b, g1(p), 0, 0)),
            pl.BlockSpec((1, 1, SEQ, LANES), lambda b, p, i: (b, g2(p), 0, 0)),
            pl.BlockSpec((1, 1, SEQ, 2 * LANES), lambda b, p, i: (b, pv(p), 0, 0)),
        ],
        out_specs=pl.BlockSpec((1, TQ, LANES), lambda b, p, i: (b, i, p)),
        out_shape=jax.ShapeDtypeStruct((BATCH, SEQ, n_pairs * LANES), BF16),
        compiler_params=pltpu.CompilerParams(
            dimension_semantics=("arbitrary", "arbitrary", "arbitrary"), vmem_limit_bytes=VMEM_LIMIT),
        name=name,
    )(q, q, k, k, vext)


def _win_kernel(sink_ref, q1_ref, q2_ref, k_ref, v_ref, o_ref, *, scale):
    pair = pl.program_id(1)
    n = pl.program_id(2)
    nb = SEQ // WINDOW
    start = pl.multiple_of(jnp.clip(n - 1, 0, nb - 3) * WINDOW, WINDOW)
    k = k_ref[0, 0, pl.ds(start, 3 * WINDOW), :]
    v = v_ref[0, 0, pl.ds(start, 3 * WINDOW), :]
    qpos = n * WINDOW + lax.broadcasted_iota(jnp.int32, (WINDOW, 3 * WINDOW), 0)
    kpos = start + lax.broadcasted_iota(jnp.int32, (WINDOW, 3 * WINDOW), 1)
    band = jnp.abs(qpos - kpos) <= WINDOW

    def one_head(q, sink):
        s = lax.dot_general(q, k, (((1,), (1,)), ((), ())), preferred_element_type=F32) * scale
        s = jnp.where(band, s, NEG_BIG)
        m = jnp.maximum(jnp.max(s, axis=-1, keepdims=True), sink)
        p = jnp.exp(s - m)
        acc = jnp.dot(p.astype(BF16), v, preferred_element_type=F32)
        return acc[:, :LANES] / (acc[:, LANES:] + jnp.exp(sink - m))

    o1 = one_head(q1_ref[0, 0], sink_ref[pair])
    o2 = one_head(q2_ref[0, 0], sink_ref[HC // 2 + pair])
    o_ref[0] = jnp.where(_lane_is_low((WINDOW, LANES)), o1, o2).astype(o_ref.dtype)


def _window_attention(sink, q, k, vext):
    kernel = functools.partial(_win_kernel, scale=HEAD_DIM ** -0.5)
    n_pairs = HC // 2
    return pl.pallas_call(
        kernel,
        grid=(BATCH, n_pairs, SEQ // WINDOW),
        in_specs=[
            pl.BlockSpec(memory_space=pltpu.SMEM),
            pl.BlockSpec((1, 1, WINDOW, LANES), lambda b, p, i: (b, p, i, 0)),
            pl.BlockSpec((1, 1, WINDOW, LANES), lambda b, p, i: (b, n_pairs + p, i, 0)),
            pl.BlockSpec((1, 1, SEQ, LANES), lambda b, p, i: (b, 0, 0, 0)),
            pl.BlockSpec((1, 1, SEQ, 2 * LANES), lambda b, p, i: (b, 0, 0, 0)),
        ],
        out_specs=pl.BlockSpec((1, WINDOW, LANES), lambda b, p, i: (b, i, p)),
        out_shape=jax.ShapeDtypeStruct((BATCH, SEQ, n_pairs * LANES), BF16),
        compiler_params=pltpu.CompilerParams(
            dimension_semantics=("arbitrary", "arbitrary", "arbitrary"), vmem_limit_bytes=VMEM_LIMIT),
        name="win_attn",
    )(sink, q, q, k, vext)


def _out_proj_kernel(x_ref, oa_ref, ob_ref, oc_ref, wa_ref, wb_ref, wc_ref, g_ref, rwt_ref, rb_ref,
                     x1_ref, hm_ref, idx_ref, gate_ref, rank_ref, cnt_ref, run_ref):
    @pl.when(pl.program_id(0) == 0)
    def _():
        run_ref[...] = jnp.zeros_like(run_ref)

    x1 = (x_ref[...]
          + jnp.dot(oa_ref[...], wa_ref[...], preferred_element_type=F32)
          + jnp.dot(ob_ref[...], wb_ref[...], preferred_element_type=F32)
          + jnp.dot(oc_ref[...], wc_ref[...], preferred_element_type=F32))
    x1_ref[...] = x1
    hm = _rms(x1, g_ref[...])
    hm_ref[...] = hm
    tm = hm.shape[0]

    logits = lax.dot_general(rwt_ref[...], hm, (((1,), (1,)), ((), ())),
                             precision=lax.Precision.HIGHEST, preferred_element_type=F32) + rb_ref[...]
    eidx = lax.broadcasted_iota(jnp.int32, (N_EXPERTS, tm), 0)
    vals, sels, hots = [], [], []
    cur = logits
    for _ in range(TOP_K):
        mx = jnp.max(cur, axis=0, keepdims=True)
        sel = jnp.min(jnp.where(cur == mx, eidx, N_EXPERTS), axis=0, keepdims=True)
        hot = eidx == sel
        vals.append(mx)
        sels.append(sel)
        hots.append(hot)
        cur = jnp.where(hot, -jnp.inf, cur)
    exps = [jnp.exp(v - vals[0]) for v in vals]
    denom = exps[0] + exps[1] + exps[2] + exps[3]
    gates = jnp.concatenate([e / denom for e in exps] + [jnp.zeros((LANES - TOP_K, tm), F32)], axis=0)
    gate_ref[...] = gates.T
    idx_ref[...] = jnp.concatenate(sels, axis=0)

    hot_all = jnp.zeros((N_EXPERTS, tm), F32)
    for hot in hots:
        hot_all = hot_all + jnp.where(hot, 1.0, 0.0)
    row = lax.broadcasted_iota(jnp.int32, (tm, tm), 0)
    col = lax.broadcasted_iota(jnp.int32, (tm, tm), 1)
    before = jnp.where(row < col, 1.0, 0.0).astype(BF16)
    rank_full = jnp.dot(hot_all.astype(BF16), before, preferred_element_type=F32) + run_ref[...]
    ranks = [jnp.sum(jnp.where(hot, rank_full, 0.0), axis=0, keepdims=True) for hot in hots]
    rank_ref[...] = jnp.concatenate(ranks, axis=0).astype(jnp.int32)
    run_ref[...] = run_ref[...] + jnp.sum(hot_all, axis=1, keepdims=True)
    cnt_ref[...] = jnp.broadcast_to(run_ref[...], (N_EXPERTS, LANES)).astype(jnp.int32)


def _out_proj(x, oa, ob, oc, wa, wb, wc, g, rwt, rb):
    tm = TM_OUT
    const = lambda i: (0, 0)
    row_blk = lambda w: pl.BlockSpec((tm, w), lambda i: (i, 0))
    col_blk = pl.BlockSpec((TOP_K, tm), lambda i: (0, i))
    return pl.pallas_call(
        _out_proj_kernel,
        grid=(TOKENS // tm,),
        in_specs=[
            row_blk(D_MODEL), row_blk(oa.shape[1]), row_blk(ob.shape[1]), row_blk(oc.shape[1]),
            pl.BlockSpec(wa.shape, const), pl.BlockSpec(wb.shape, const), pl.BlockSpec(wc.shape, const),
            pl.BlockSpec((1, D_MODEL), const),
            pl.BlockSpec((N_EXPERTS, D_MODEL), const),
            pl.BlockSpec((N_EXPERTS, 1), const),
        ],
        out_specs=[row_blk(D_MODEL), row_blk(D_MODEL), col_blk, row_blk(LANES), col_blk,
                   pl.BlockSpec((N_EXPERTS, LANES), const)],
        out_shape=[
            jax.ShapeDtypeStruct((TOKENS, D_MODEL), F32),
            jax.ShapeDtypeStruct((TOKENS, D_MODEL), F32),
            jax.ShapeDtypeStruct((TOP_K, TOKENS), jnp.int32),
            jax.ShapeDtypeStruct((TOKENS, LANES), F32),
            jax.ShapeDtypeStruct((TOP_K, TOKENS), jnp.int32),
            jax.ShapeDtypeStruct((N_EXPERTS, LANES), jnp.int32),
        ],
        scratch_shapes=[pltpu.VMEM((N_EXPERTS, 1), F32)],
        compiler_params=pltpu.CompilerParams(
            dimension_semantics=("arbitrary",), vmem_limit_bytes=VMEM_LIMIT),
        name="out_proj",
    )(x, oa, ob, oc, wa, wb, wc, g, rwt, rb)


def _pos_at(pos_smem, k, t, tm):
    return pos_smem[k * (tm // LANES) + lax.shift_right_logical(t, 7), lax.bitwise_and(t, LANES - 1)]


def _dispatch_kernel(pos_hbm, hm_ref, xs_init_ref, xs_ref, pos_smem, pos_sem, row_sem):
    del xs_init_ref
    i = pl.program_id(0)
    tm = hm_ref.shape[0]
    load = pltpu.make_async_copy(pos_hbm.at[i], pos_smem, pos_sem)
    load.start()
    load.wait()

    def row_copy(t, k):
        return pltpu.make_async_copy(hm_ref.at[pl.ds(t, 1)],
                                     xs_ref.at[pl.ds(_pos_at(pos_smem, k, t, tm), 1)], row_sem)

    def issue(t, carry):
        for k in range(TOP_K):
            row_copy(t, k).start()
        return carry

    def drain(t, carry):
        for k in range(TOP_K):
            row_copy(t, k).wait()
        return carry

    lax.fori_loop(0, tm, issue, 0)
    lax.fori_loop(0, tm, drain, 0)


def _dispatch(pos_tiles, hm, xs_init):
    tm = TM_DISPATCH
    return pl.pallas_call(
        _dispatch_kernel,
        grid=(TOKENS // tm,),
        in_specs=[
            pl.BlockSpec(memory_space=pl.ANY),
            pl.BlockSpec((tm, D_MODEL), lambda i: (i, 0)),
            pl.BlockSpec(memory_space=pl.ANY),
        ],
        out_specs=pl.BlockSpec(memory_space=pl.ANY),
        out_shape=jax.ShapeDtypeStruct((N_SORTED_ROWS, D_MODEL), F32),
        scratch_shapes=[
            pltpu.SMEM((TOP_K * tm // LANES, LANES), jnp.int32),
            pltpu.SemaphoreType.DMA,
            pltpu.SemaphoreType.DMA,
        ],
        input_output_aliases={2: 0},
        compiler_params=pltpu.CompilerParams(
            dimension_semantics=("arbitrary",), vmem_limit_bytes=VMEM_LIMIT, has_side_effects=True),
        name="dispatch",
    )(pos_tiles, hm, xs_init)


def _experts_kernel(te_ref, ts_ref, nv_ref, xs_ref, wup_ref, bup_ref, wdn_ref, bdn_ref, ys_ref):
    del te_ref, ts_ref

    @pl.when(pl.program_id(0) < nv_ref[0])
    def _():
        x = xs_ref[...].astype(BF16)
        gu = jnp.dot(x, wup_ref[0], preferred_element_type=F32) + bup_ref[0]
        x_glu = jnp.minimum(gu[:, :D_FF], SWIGLU_LIMIT)
        x_lin = jnp.clip(gu[:, D_FF:], -SWIGLU_LIMIT, SWIGLU_LIMIT)
        act = x_glu * jax.nn.sigmoid(SWIGLU_ALPHA * x_glu) * (x_lin + 1.0)
        ys_ref[...] = jnp.dot(act.astype(BF16), wdn_ref[0], preferred_element_type=F32) + bdn_ref[0]


def _experts(tile_expert, tile_src, n_valid, xs, wup, bup, wdn, bdn):
    tm = TM_EXPERT
    grid_spec = pltpu.PrefetchScalarGridSpec(
        num_scalar_prefetch=3,
        grid=(N_EXPERT_TILES,),
        in_specs=[
            pl.BlockSpec((tm, D_MODEL), lambda i, te, ts, nv: (ts[i], 0)),
            pl.BlockSpec((1, D_MODEL, 2 * D_FF), lambda i, te, ts, nv: (te[i], 0, 0)),
            pl.BlockSpec((1, 1, 2 * D_FF), lambda i, te, ts, nv: (te[i], 0, 0)),
            pl.BlockSpec((1, D_FF, D_MODEL), lambda i, te, ts, nv: (te[i], 0, 0)),
            pl.BlockSpec((1, 1, D_MODEL), lambda i, te, ts, nv: (te[i], 0, 0)),
        ],
        out_specs=pl.BlockSpec((tm, D_MODEL), lambda i, te, ts, nv: (ts[i], 0)),
    )
    return pl.pallas_call(
        _experts_kernel,
        grid_spec=grid_spec,
        out_shape=jax.ShapeDtypeStruct((N_SORTED_ROWS, D_MODEL), F32),
        compiler_params=pltpu.CompilerParams(
            dimension_semantics=("arbitrary",), vmem_limit_bytes=VMEM_LIMIT),
        name="experts",
    )(tile_expert, tile_src, n_valid, xs, wup, bup, wdn, bdn)


def _combine_kernel(pos_hbm, ys_hbm, x1_ref, gate_ref, p_ref, g_ref, wg_ref, wp_ref, gf_ref,
                    o_ref, pos_smem, rows_ref, pos_sem, row_sem, *, apply_final_norm):
    i = pl.program_id(0)
    tm = x1_ref.shape[0]
    load = pltpu.make_async_copy(pos_hbm.at[i], pos_smem, pos_sem)
    load.start()
    load.wait()

    def row_copy(t, k):
        return pltpu.make_async_copy(ys_hbm.at[pl.ds(_pos_at(pos_smem, k, t, tm), 1)],
                                     rows_ref.at[k, pl.ds(t, 1)], row_sem)

    def issue(t, carry):
        for k in range(TOP_K):
            row_copy(t, k).start()
        return carry

    def drain(t, carry):
        for k in range(TOP_K):
            row_copy(t, k).wait()
        return carry

    lax.fori_loop(0, tm, issue, 0)
    lax.fori_loop(0, tm, drain, 0)

    x2 = x1_ref[...]
    for k in range(TOP_K):
        x2 = x2 + rows_ref[k] * gate_ref[:, k:k + 1]
    hp = _rms(x2, g_ref[...]).astype(BF16)
    gate = jax.nn.sigmoid(jnp.dot(hp, wg_ref[...], preferred_element_type=F32))
    pe = jnp.dot(p_ref[...].astype(BF16), wp_ref[...], preferred_element_type=F32)
    x3 = x2 + pe * gate
    if apply_final_norm:
        x3 = _rms(x3, gf_ref[...])
    o_ref[...] = x3


def _combine(pos_tiles, ys, x1, gates_t, p, g, wg, wp, gf, apply_final_norm):
    tm = TM_COMBINE
    const = lambda i: (0, 0)
    row_blk = lambda w: pl.BlockSpec((tm, w), lambda i: (i, 0))
    kernel = functools.partial(_combine_kernel, apply_final_norm=apply_final_norm)
    return pl.pallas_call(
        kernel,
        grid=(TOKENS // tm,),
        in_specs=[
            pl.BlockSpec(memory_space=pl.ANY),
            pl.BlockSpec(memory_space=pl.ANY),
            row_blk(D_MODEL), row_blk(LANES), row_blk(PLE_DIM),
            pl.BlockSpec((1, D_MODEL), const),
            pl.BlockSpec((D_MODEL, D_MODEL), const),
            pl.BlockSpec((PLE_DIM, D_MODEL), const),
            pl.BlockSpec((1, D_MODEL), const),
        ],
        out_specs=row_blk(D_MODEL),
        out_shape=jax.ShapeDtypeStruct((TOKENS, D_MODEL), F32),
        scratch_shapes=[
            pltpu.SMEM((TOP_K * tm // LANES, LANES), jnp.int32),
            pltpu.VMEM((TOP_K, tm, D_MODEL), F32),
            pltpu.SemaphoreType.DMA,
            pltpu.SemaphoreType.DMA,
        ],
        compiler_params=pltpu.CompilerParams(
            dimension_semantics=("arbitrary",), vmem_limit_bytes=VMEM_LIMIT),
        name="combine",
    )(pos_tiles, ys, x1, gates_t, p, g, wg, wp, gf)


def _in_proj_columns():
    src = np.full((Z_COLS,), -1, np.int64)
    src[Z_CQ:Z_CQ + Q_LORA] = np.arange(Q_LORA)
    src[Z_CKV:Z_CKV + KV_LORA] = Q_LORA + np.arange(KV_LORA)
    src[Z_KROPE + NOPE_A:Z_KROPE + NOPE_A + ROPE_A] = Q_LORA + KV_LORA + np.arange(ROPE_A)
    b0 = A_COLS
    for j in range(HB // 2):
        src[Z_BQ + j * LANES:Z_BQ + j * LANES + HALF] = b0 + j * HEAD_DIM + np.arange(HEAD_DIM)
        src[Z_BQ + j * LANES + HALF:Z_BQ + (j + 1) * LANES] = b0 + (HB // 2 + j) * HEAD_DIM + np.arange(HEAD_DIM)
    src[Z_BK:Z_BK + LANES] = b0 + HB * HEAD_DIM + np.arange(LANES)
    src[Z_BV:Z_BV + LANES] = b0 + (HB + KVB) * HEAD_DIM + np.arange(LANES)
    c0 = A_COLS + B_COLS
    for j in range(HC // 2):
        src[Z_CQS + j * LANES:Z_CQS + j * LANES + HALF] = c0 + j * HEAD_DIM + np.arange(HEAD_DIM)
        src[Z_CQS + j * LANES + HALF:Z_CQS + (j + 1) * LANES] = c0 + (HC // 2 + j) * HEAD_DIM + np.arange(HEAD_DIM)
    src[Z_CK:Z_CK + LANES] = c0 + HC * HEAD_DIM + np.arange(LANES)
    src[Z_CV:Z_CV + LANES] = c0 + (HC + KVC) * HEAD_DIM + np.arange(LANES)
    return src


def _wq_columns():
    src = np.full((HA * LANES,), -1, np.int64)
    dq = NOPE_A + ROPE_A
    for h in range(HA):
        src[h * LANES:h * LANES + dq] = h * dq + np.arange(dq)
    return src


def _wkv_columns():
    src = np.full((HA * LANES + (HA // 2) * LANES,), -1, np.int64)
    dkv = NOPE_A + V_A
    for h in range(HA):
        src[h * LANES:h * LANES + NOPE_A] = h * dkv + np.arange(NOPE_A)
        v0 = HA * LANES + (h // 2) * LANES + (h % 2) * HALF
        src[v0:v0 + V_A] = h * dkv + NOPE_A + np.arange(V_A)
    return src


def _paired_rows(base, n_heads):
    rows = []
    for j in range(n_heads // 2):
        rows.append(base + j * HEAD_DIM + np.arange(HEAD_DIM))
        rows.append(base + (n_heads // 2 + j) * HEAD_DIM + np.arange(HEAD_DIM))
    return np.concatenate(rows)


def _take_cols(w, src):
    cols = jnp.take(w, jnp.asarray(np.maximum(src, 0)), axis=1)
    return jnp.where(jnp.asarray(src >= 0)[None, :], cols, 0.0)


def _rope_tables():
    def cos_sin(pos, dim):
        inv = 1.0 / (ROPE_THETA ** (jnp.arange(0, dim, 2, dtype=F32) / dim))
        ang = pos.astype(F32)[:, None] * inv[None, :]
        return jnp.cos(ang), jnp.sin(ang)

    pos = jnp.arange(SEQ, dtype=jnp.int32)
    rows = pos // GRID_W
    cols = pos % GRID_W
    zeros = lambda w: jnp.zeros((SEQ, w), F32)
    ones = lambda w: jnp.ones((SEQ, w), F32)

    cos_a, sin_a = cos_sin(pos, ROPE_A)
    c_a = jnp.concatenate([ones(NOPE_A), cos_a, cos_a, ones(32)], axis=1)
    up_a = jnp.concatenate([zeros(NOPE_A), -sin_a, zeros(16), zeros(32)], axis=1)
    dn_a = jnp.concatenate([zeros(NOPE_A), zeros(16), sin_a, zeros(32)], axis=1)

    cos_r, sin_r = cos_sin(rows, HEAD_DIM // 2)
    cos_w, sin_w = cos_sin(cols, HEAD_DIM // 2)
    z16 = zeros(16)
    c_b = jnp.concatenate([cos_r, cos_r, cos_w, cos_w] * 2, axis=1)
    up_b = jnp.concatenate([-sin_r, z16, -sin_w, z16] * 2, axis=1)
    dn_b = jnp.concatenate([z16, sin_r, z16, sin_w] * 2, axis=1)

    cos_c, sin_c = cos_sin(pos, HEAD_DIM)
    z32 = zeros(32)
    c_c = jnp.concatenate([cos_c, cos_c] * 2, axis=1)
    up_c = jnp.concatenate([-sin_c, z32] * 2, axis=1)
    dn_c = jnp.concatenate([z32, sin_c] * 2, axis=1)
    return jnp.concatenate([c_a, up_a, dn_a, c_b, up_b, dn_b, c_c, up_c, dn_c], axis=1)


def _routing_tables(idx, rank, counts):
    tiles_per_expert = (counts + TM_EXPERT - 1) // TM_EXPERT
    tile_end = jnp.cumsum(tiles_per_expert)
    row_start = (tile_end - tiles_per_expert) * TM_EXPERT
    pos = jnp.take(row_start, idx) + rank
    n_valid = tile_end[-1]
    tile_ids = jnp.arange(N_EXPERT_TILES, dtype=jnp.int32)
    tile_src = jnp.minimum(tile_ids, n_valid - 1)
    tile_expert = jnp.minimum(jnp.searchsorted(tile_end, tile_src, side="right"), N_EXPERTS - 1)
    return pos, tile_expert.astype(jnp.int32), tile_src.astype(jnp.int32), n_valid.reshape(1).astype(jnp.int32)


def _pos_tiles(pos, tm):
    return pos.reshape(TOP_K, TOKENS // tm, tm).transpose(1, 0, 2).reshape(TOKENS // tm, TOP_K * tm // LANES, LANES)


def kernel(x, p, attn_norm, w_in, mla_q_norm, mla_wq_up, mla_kv_norm, mla_wkv_up, gqa_q_norm, gqa_k_norm,
           swa_sink, w_out, moe_norm, router_w, router_b, w_up, b_up, w_down, b_down, ple_norm, w_ple,
           w_ple_gate, final_norm):
    tabs = _rope_tables()
    in_cols, wq_cols, wkv_cols = _in_proj_columns(), _wq_columns(), _wkv_columns()
    rows_b = _paired_rows(HA * V_A, HB)
    rows_c = _paired_rows(HA * V_A + HB * HEAD_DIM, HC)
    ident = lambda n: (lambda pr: pr * 0 + n)
    maps_a = (lambda pr: 2 * pr, lambda pr: 2 * pr + 1, lambda pr: 2 * pr, lambda pr: 2 * pr + 1, lambda pr: pr)
    maps_b = (lambda pr: pr, lambda pr: HB // 2 + pr, ident(0), ident(0), ident(0))

    xf = x.reshape(TOKENS, D_MODEL)
    for i in range(DEPTH):
        win = _take_cols(w_in[i], in_cols).astype(BF16)
        wq = _take_cols(mla_wq_up[i], wq_cols).astype(BF16)
        wkv = _take_cols(mla_wkv_up[i], wkv_cols).astype(BF16)
        gbq = jnp.tile(gqa_q_norm[i], 2)[None, :]
        gbk = jnp.tile(gqa_k_norm[i], 2)[None, :]
        qa, ka, va, qb, kb, vb, qc, kc, vc = _in_proj(
            xf.reshape(BATCH, SEQ, D_MODEL), attn_norm[i][None, :], win, mla_q_norm[i][None, :], wq,
            mla_kv_norm[i][None, :], wkv, gbq, gbk, tabs)

        oa = _dense_attention(qa, ka, va, HA // 2, maps_a, (NOPE_A + ROPE_A) ** -0.5, "attn_a")
        ob = _dense_attention(qb, kb, vb, HB // 2, maps_b, HEAD_DIM ** -0.5, "attn_b")
        oc = _window_attention(swa_sink[i], qc, kc, vc)

        wo = w_out[i]
        x1, hm, idx, gates_t, rank, cnt = _out_proj(
            xf, oa.reshape(TOKENS, -1), ob.reshape(TOKENS, -1), oc.reshape(TOKENS, -1),
            wo[:HA * V_A].astype(BF16), jnp.take(wo, jnp.asarray(rows_b), axis=0).astype(BF16),
            jnp.take(wo, jnp.asarray(rows_c), axis=0).astype(BF16),
            moe_norm[i][None, :], router_w[i].T, router_b[i][:, None])

        pos, tile_expert, tile_src, n_valid = _routing_tables(idx, rank, cnt[:, 0])
        xs = _dispatch(_pos_tiles(pos, TM_DISPATCH), hm, jnp.zeros((N_SORTED_ROWS, D_MODEL), F32))
        ys = _experts(tile_expert, tile_src, n_valid, xs, w_up[i].astype(BF16), b_up[i][:, None, :],
                      w_down[i].astype(BF16), b_down[i][:, None, :])
        xf = _combine(_pos_tiles(pos, TM_COMBINE), ys, x1, gates_t, p[i].reshape(TOKENS, PLE_DIM),
                      ple_norm[i][None, :], w_ple_gate[i].astype(BF16), w_ple[i].astype(BF16),
                      final_norm[None, :], apply_final_norm=(i == DEPTH - 1))
    return xf.reshape(BATCH, SEQ, D_MODEL)
```

```python
import functools
import math

import numpy as np
import jax
import jax.numpy as jnp
from jax import lax
from jax.experimental import pallas as pl
from jax.experimental.pallas import tpu as pltpu

F32 = jnp.float32
BF16 = jnp.bfloat16

D_MODEL = 1024
BATCH = 8
SEQ = 4096
DEPTH = 2
TOKENS = BATCH * SEQ
GRID_W = 64
PLE_DIM = 256
HEAD_DIM = 64
ROPE_THETA = 10000.0
EPS = 1e-6
HA, Q_LORA, KV_LORA, NOPE_A, ROPE_A, V_A = 6, 256, 128, 64, 32, 64
HB, KVB = 6, 2
HC, KVC, WINDOW = 4, 2, 128
A_COLS = Q_LORA + KV_LORA + ROPE_A
B_COLS = (HB + 2 * KVB) * HEAD_DIM
C_COLS = (HC + 2 * KVC) * HEAD_DIM
N_EXPERTS = 32
TOP_K = 4
D_FF = D_MODEL
SWIGLU_LIMIT = 7.0
SWIGLU_ALPHA = 1.702

LANES = 128
HALF = LANES // 2
BF16_SUBLANES = 16

VT_ROWS = HEAD_DIM + BF16_SUBLANES

TM_IN = 512
TQ = 512
TQ_SUB = 256
TM_OUT = 512
TM_DISPATCH = 512
TM_EXPERT = 512
TM_COMBINE = 256
N_EXPERT_TILES = TOKENS * TOP_K // TM_EXPERT + N_EXPERTS
N_SORTED_ROWS = N_EXPERT_TILES * TM_EXPERT
VMEM_LIMIT = 52 * 1024 * 1024

Z_CQ = 0
Z_CKV = Q_LORA
Z_KROPE = Z_CKV + KV_LORA
Z_BQ = Z_KROPE + LANES
Z_BK = Z_BQ + 3 * LANES
Z_BV = Z_BK + LANES
Z_CQS = Z_BV + LANES
Z_CK = Z_CQS + 2 * LANES
Z_CV = Z_CK + LANES
Z_COLS = Z_CV + LANES

NEG_BIG = -1e30
LOG2E = math.log2(math.e)


def _rms(x, g):
    return x * lax.rsqrt(jnp.mean(x * x, axis=-1, keepdims=True) + EPS) * g


def _lane_is_low(shape):
    return lax.broadcasted_iota(jnp.int32, shape, len(shape) - 1) < HALF


def _rms_per_half(xs, g):
    low = _lane_is_low(xs.shape)
    x2 = xs * xs
    s_lo = jnp.sum(jnp.where(low, x2, 0.0), axis=-1, keepdims=True)
    s_hi = jnp.sum(jnp.where(low, 0.0, x2), axis=-1, keepdims=True)
    ms = jnp.where(low, s_lo, s_hi) * (1.0 / HEAD_DIM)
    return xs * lax.rsqrt(ms + EPS) * g


def _in_proj_kernel(x_ref, g_ref, win_ref, gq_ref, wq_ref, gkv_ref, wkv_ref, gbq_ref, gbk_ref, tab_ref,
                    qa_ref, ka_ref, va_ref, qb_ref, kb_ref, vb_ref, qc_ref, kc_ref, vc_ref):
    h = _rms(x_ref[0], g_ref[...])
    z = jnp.dot(h.astype(BF16), win_ref[...], preferred_element_type=F32)
    tm = z.shape[0]
    ones = jnp.ones((tm, LANES), BF16)
    ones_t = jnp.ones((VT_ROWS - HEAD_DIM, tm), BF16)
    low = _lane_is_low((tm, LANES))
    top = lax.broadcasted_iota(jnp.int32, (LANES, tm), 0) < HALF

    def rope(xs, table, shift):
        base = table * 3 * LANES
        c = tab_ref[:, base:base + LANES]
        s_up = tab_ref[:, base + LANES:base + 2 * LANES]
        s_dn = tab_ref[:, base + 2 * LANES:base + 3 * LANES]
        return xs * c + pltpu.roll(xs, LANES - shift, 1) * s_up + pltpu.roll(xs, shift, 1) * s_dn

    c_q = _rms(z[:, Z_CQ:Z_CQ + Q_LORA], gq_ref[...])
    q = jnp.dot(c_q.astype(BF16), wq_ref[...], preferred_element_type=F32)
    c_kv = _rms(z[:, Z_CKV:Z_CKV + KV_LORA], gkv_ref[...])
    kv = jnp.dot(c_kv.astype(BF16), wkv_ref[...], preferred_element_type=F32)
    k_rope = rope(z[:, Z_KROPE:Z_KROPE + LANES], 0, ROPE_A // 2)
    for hd in range(HA):
        qa_ref[0, hd] = rope(q[:, hd * LANES:(hd + 1) * LANES], 0, ROPE_A // 2).T.astype(BF16)
        ka_ref[0, hd] = (kv[:, hd * LANES:(hd + 1) * LANES] + k_rope).astype(BF16)
    for j in range(HA // 2):
        vt = kv[:, (HA + j) * LANES:(HA + j + 1) * LANES].T.astype(BF16)
        for half in range(2):
            va_ref[0, 2 * j + half, 0:HEAD_DIM, :] = vt[half * HALF:(half + 1) * HALF]
            va_ref[0, 2 * j + half, HEAD_DIM:VT_ROWS, :] = ones_t

    for j in range(HB // 2):
        s = rope(_rms_per_half(z[:, Z_BQ + j * LANES:Z_BQ + (j + 1) * LANES], gbq_ref[...]), 1, HEAD_DIM // 4)
        st = s.T
        qb_ref[0, j] = jnp.where(top, st, 0.0).astype(BF16)
        qb_ref[0, HB // 2 + j] = jnp.where(top, 0.0, st).astype(BF16)
    kb_ref[0, 0] = rope(_rms_per_half(z[:, Z_BK:Z_BK + LANES], gbk_ref[...]), 1, HEAD_DIM // 4).astype(BF16)
    vt = z[:, Z_BV:Z_BV + LANES].T.astype(BF16)
    for g in range(KVB):
        vb_ref[0, g, 0:HEAD_DIM, :] = vt[g * HALF:(g + 1) * HALF]
        vb_ref[0, g, HEAD_DIM:VT_ROWS, :] = ones_t

    for j in range(HC // 2):
        s = rope(z[:, Z_CQS + j * LANES:Z_CQS + (j + 1) * LANES], 2, HEAD_DIM // 2)
        qc_ref[0, j] = jnp.where(low, s, 0.0).astype(BF16)
        qc_ref[0, HC // 2 + j] = jnp.where(low, 0.0, s).astype(BF16)
    kc_ref[0, 0] = rope(z[:, Z_CK:Z_CK + LANES], 2, HEAD_DIM // 2).astype(BF16)
    vc_ref[0, 0, :, 0:LANES] = z[:, Z_CV:Z_CV + LANES].astype(BF16)
    vc_ref[0, 0, :, LANES:2 * LANES] = ones


def _in_proj(x3, g, win, gq, wq, gkv, wkv, gbq, gbk, tabs):
    nst = SEQ // TM_IN
    const2 = lambda b, s: (0, 0)
    head_out = lambda n, w: (jax.ShapeDtypeStruct((BATCH, n, SEQ, w), BF16),
                             pl.BlockSpec((1, n, TM_IN, w), lambda b, s: (b, 0, s, 0)))
    head_out_t = lambda n, r: (jax.ShapeDtypeStruct((BATCH, n, r, SEQ), BF16),
                               pl.BlockSpec((1, n, r, TM_IN), lambda b, s: (b, 0, 0, s)))
    outs = [head_out_t(HA, LANES), head_out(HA, LANES), head_out_t(HA, VT_ROWS),
            head_out_t(HB, LANES), head_out(1, LANES), head_out_t(KVB, VT_ROWS),
            head_out(HC, LANES), head_out(1, LANES), head_out(1, 2 * LANES)]
    return pl.pallas_call(
        _in_proj_kernel,
        grid=(BATCH, nst),
        in_specs=[
            pl.BlockSpec((1, TM_IN, D_MODEL), lambda b, s: (b, s, 0)),
            pl.BlockSpec((1, D_MODEL), const2),
            pl.BlockSpec((D_MODEL, Z_COLS), const2),
            pl.BlockSpec((1, Q_LORA), const2),
            pl.BlockSpec((Q_LORA, HA * LANES), const2),
            pl.BlockSpec((1, KV_LORA), const2),
            pl.BlockSpec((KV_LORA, HA * LANES + (HA // 2) * LANES), const2),
            pl.BlockSpec((1, LANES), const2),
            pl.BlockSpec((1, LANES), const2),
            pl.BlockSpec((TM_IN, 9 * LANES), lambda b, s: (s, 0)),
        ],
        out_specs=[o[1] for o in outs],
        out_shape=[o[0] for o in outs],
        compiler_params=pltpu.CompilerParams(
            dimension_semantics=("arbitrary", "arbitrary"), vmem_limit_bytes=VMEM_LIMIT),
        name="in_proj",
    )(x3, g, win, gq, wq, gkv, wkv, gbq, gbk, tabs)


def _attn_kernel(q1_ref, q2_ref, k1_ref, k2_ref, v1_ref, v2_ref, o_ref, *, c):
    n_sub = q1_ref.shape[3] // TQ_SUB

    def scores(q_ref, k_ref, sub):
        qt = q_ref[0, 0, :, sub * TQ_SUB:(sub + 1) * TQ_SUB]
        return jnp.dot(k_ref[0, 0], qt, preferred_element_type=F32)

    def probs(st):
        m = jnp.max(st, axis=0, keepdims=True)
        return jnp.exp2((st - m) * c).astype(BF16)

    def values(v_ref, pt):
        acc = jnp.dot(v_ref[0, 0], pt, preferred_element_type=F32)
        return acc[:HEAD_DIM] / acc[HEAD_DIM:HEAD_DIM + 1]

    st = [scores(q1_ref, k1_ref, 0), scores(q2_ref, k2_ref, 0)]
    for sub in range(n_sub):
        st_next = ([scores(q1_ref, k1_ref, sub + 1), scores(q2_ref, k2_ref, sub + 1)]
                   if sub + 1 < n_sub else None)
        pt = [probs(st[0]), probs(st[1])]
        ot = jnp.concatenate([values(v1_ref, pt[0]), values(v2_ref, pt[1])], axis=0)
        o_ref[0, sub * TQ_SUB:(sub + 1) * TQ_SUB, :] = ot.T.astype(o_ref.dtype)
        st = st_next


def _dense_attention(qt, k, vt, n_heads, k_head, v_head, scale, name):
    kernel = functools.partial(_attn_kernel, c=scale * LOG2E)
    q_spec = lambda off: pl.BlockSpec((1, 1, LANES, TQ), lambda b, p, i: (b, 2 * p + off, 0, i))
    k_spec = lambda off: pl.BlockSpec((1, 1, SEQ, LANES), lambda b, p, i: (b, k_head(2 * p + off), 0, 0))
    v_spec = lambda off: pl.BlockSpec((1, 1, VT_ROWS, SEQ), lambda b, p, i: (b, v_head(2 * p + off), 0, 0))
    return pl.pallas_call(
        kernel,
        grid=(BATCH, n_heads // 2, SEQ // TQ),
        in_specs=[q_spec(0), q_spec(1), k_spec(0), k_spec(1), v_spec(0), v_spec(1)],
        out_specs=pl.BlockSpec((1, TQ, LANES), lambda b, p, i: (b, i, p)),
        out_shape=jax.ShapeDtypeStruct((BATCH, SEQ, n_heads * HEAD_DIM), BF16),
        compiler_params=pltpu.CompilerParams(
            dimension_semantics=("arbitrary", "arbitrary", "arbitrary"), vmem_limit_bytes=VMEM_LIMIT),
        name=name,
    )(qt, qt, k, k, vt, vt)


def _win_kernel(sink_ref, q1_ref, q2_ref, k_ref, v_ref, o_ref, *, scale):
    pair = pl.program_id(1)
    n = pl.program_id(2)
    nb = SEQ // WINDOW
    start = pl.multiple_of(jnp.clip(n - 1, 0, nb - 3) * WINDOW, WINDOW)
    k = k_ref[0, 0, pl.ds(start, 3 * WINDOW), :]
    v = v_ref[0, 0, pl.ds(start, 3 * WINDOW), :]
    qpos = n * WINDOW + lax.broadcasted_iota(jnp.int32, (WINDOW, 3 * WINDOW), 0)
    kpos = start + lax.broadcasted_iota(jnp.int32, (WINDOW, 3 * WINDOW), 1)
    band = jnp.abs(qpos - kpos) <= WINDOW

    def one_head(q, sink):
        s = lax.dot_general(q, k, (((1,), (1,)), ((), ())), preferred_element_type=F32) * scale
        s = jnp.where(band, s, NEG_BIG)
        m = jnp.maximum(jnp.max(s, axis=-1, keepdims=True), sink)
        p = jnp.exp(s - m)
        acc = jnp.dot(p.astype(BF16), v, preferred_element_type=F32)
        return acc[:, :LANES] / (acc[:, LANES:] + jnp.exp(sink - m))

    o1 = one_head(q1_ref[0, 0], sink_ref[pair])
    o2 = one_head(q2_ref[0, 0], sink_ref[HC // 2 + pair])
    o_ref[0] = jnp.where(_lane_is_low((WINDOW, LANES)), o1, o2).astype(o_ref.dtype)


def _window_attention(sink, q, k, vext):
    kernel = functools.partial(_win_kernel, scale=HEAD_DIM ** -0.5)
    n_pairs = HC // 2
    return pl.pallas_call(
        kernel,
        grid=(BATCH, n_pairs, SEQ // WINDOW),
        in_specs=[
            pl.BlockSpec(memory_space=pltpu.SMEM),
            pl.BlockSpec((1, 1, WINDOW, LANES), lambda b, p, i: (b, p, i, 0)),
            pl.BlockSpec((1, 1, WINDOW, LANES), lambda b, p, i: (b, n_pairs + p, i, 0)),
            pl.BlockSpec((1, 1, SEQ, LANES), lambda b, p, i: (b, 0, 0, 0)),
            pl.BlockSpec((1, 1, SEQ, 2 * LANES), lambda b, p, i: (b, 0, 0, 0)),
        ],
        out_specs=pl.BlockSpec((1, WINDOW, LANES), lambda b, p, i: (b, i, p)),
        out_shape=jax.ShapeDtypeStruct((BATCH, SEQ, n_pairs * LANES), BF16),
        compiler_params=pltpu.CompilerParams(
            dimension_semantics=("arbitrary", "arbitrary", "arbitrary"), vmem_limit_bytes=VMEM_LIMIT),
        name="win_attn",
    )(sink, q, q, k, vext)


def _out_proj_kernel(x_ref, oa_ref, ob_ref, oc_ref, wa_ref, wb_ref, wc_ref, g_ref, rwt_ref, rb_ref,
                     x1_ref, hm_ref, idx_ref, gate_ref, rank_ref, cnt_ref, run_ref):
    @pl.when(pl.program_id(0) == 0)
    def _():
        run_ref[...] = jnp.zeros_like(run_ref)

    x1 = (x_ref[...]
          + jnp.dot(oa_ref[...], wa_ref[...], preferred_element_type=F32)
          + jnp.dot(ob_ref[...], wb_ref[...], preferred_element_type=F32)
          + jnp.dot(oc_ref[...], wc_ref[...], preferred_element_type=F32))
    x1_ref[...] = x1
    hm = _rms(x1, g_ref[...])
    hm_ref[...] = hm
    tm = hm.shape[0]

    logits = lax.dot_general(rwt_ref[...], hm, (((1,), (1,)), ((), ())),
                             precision=lax.Precision.HIGHEST, preferred_element_type=F32) + rb_ref[...]
    eidx = lax.broadcasted_iota(jnp.int32, (N_EXPERTS, tm), 0)
    vals, sels, hots = [], [], []
    cur = logits
    for _ in range(TOP_K):
        mx = jnp.max(cur, axis=0, keepdims=True)
        sel = jnp.min(jnp.where(cur == mx, eidx, N_EXPERTS), axis=0, keepdims=True)
        hot = eidx == sel
        vals.append(mx)
        sels.append(sel)
        hots.append(hot)
        cur = jnp.where(hot, -jnp.inf, cur)
    exps = [jnp.exp(v - vals[0]) for v in vals]
    denom = exps[0] + exps[1] + exps[2] + exps[3]
    gates = jnp.concatenate([e / denom for e in exps] + [jnp.zeros((LANES - TOP_K, tm), F32)], axis=0)
    gate_ref[...] = gates.T
    idx_ref[...] = jnp.concatenate(sels, axis=0)

    hot_all = jnp.zeros((N_EXPERTS, tm), F32)
    for hot in hots:
        hot_all = hot_all + jnp.where(hot, 1.0, 0.0)
    row = lax.broadcasted_iota(jnp.int32, (tm, tm), 0)
    col = lax.broadcasted_iota(jnp.int32, (tm, tm), 1)
    before = jnp.where(row < col, 1.0, 0.0).astype(BF16)
    rank_full = jnp.dot(hot_all.astype(BF16), before, preferred_element_type=F32) + run_ref[...]
    ranks = [jnp.sum(jnp.where(hot, rank_full, 0.0), axis=0, keepdims=True) for hot in hots]
    rank_ref[...] = jnp.concatenate(ranks, axis=0).astype(jnp.int32)
    run_ref[...] = run_ref[...] + jnp.sum(hot_all, axis=1, keepdims=True)
    cnt_ref[...] = jnp.broadcast_to(run_ref[...], (N_EXPERTS, LANES)).astype(jnp.int32)


def _out_proj(x, oa, ob, oc, wa, wb, wc, g, rwt, rb):
    tm = TM_OUT
    const = lambda i: (0, 0)
    row_blk = lambda w: pl.BlockSpec((tm, w), lambda i: (i, 0))
    col_blk = pl.BlockSpec((TOP_K, tm), lambda i: (0, i))
    return pl.pallas_call(
        _out_proj_kernel,
        grid=(TOKENS // tm,),
        in_specs=[
            row_blk(D_MODEL), row_blk(oa.shape[1]), row_blk(ob.shape[1]), row_blk(oc.shape[1]),
            pl.BlockSpec(wa.shape, const), pl.BlockSpec(wb.shape, const), pl.BlockSpec(wc.shape, const),
            pl.BlockSpec((1, D_MODEL), const),
            pl.BlockSpec((N_EXPERTS, D_MODEL), const),
            pl.BlockSpec((N_EXPERTS, 1), const),
        ],
        out_specs=[row_blk(D_MODEL), row_blk(D_MODEL), col_blk, row_blk(LANES), col_blk,
                   pl.BlockSpec((N_EXPERTS, LANES), const)],
        out_shape=[
            jax.ShapeDtypeStruct((TOKENS, D_MODEL), F32),
            jax.ShapeDtypeStruct((TOKENS, D_MODEL), F32),
            jax.ShapeDtypeStruct((TOP_K, TOKENS), jnp.int32),
            jax.ShapeDtypeStruct((TOKENS, LANES), F32),
            jax.ShapeDtypeStruct((TOP_K, TOKENS), jnp.int32),
            jax.ShapeDtypeStruct((N_EXPERTS, LANES), jnp.int32),
        ],
        scratch_shapes=[pltpu.VMEM((N_EXPERTS, 1), F32)],
        compiler_params=pltpu.CompilerParams(
            dimension_semantics=("arbitrary",), vmem_limit_bytes=VMEM_LIMIT),
        name="out_proj",
    )(x, oa, ob, oc, wa, wb, wc, g, rwt, rb)


def _pos_at(pos_smem, k, t, tm):
    return pos_smem[k * (tm // LANES) + lax.shift_right_logical(t, 7), lax.bitwise_and(t, LANES - 1)]


def _dispatch_kernel(pos_hbm, hm_ref, xs_ref, pos_smem, pos_sem, row_sems):
    i = pl.program_id(0)
    tm = hm_ref.shape[0]
    load = pltpu.make_async_copy(pos_hbm.at[i], pos_smem, pos_sem)
    load.start()
    load.wait()

    def issue(t, carry):
        for k in range(TOP_K):
            pltpu.make_async_copy(hm_ref.at[pl.ds(t, 1)], xs_ref.at[pl.ds(_pos_at(pos_smem, k, t, tm), 1)],
                                  row_sems.at[k]).start(priority=k % 2)
        return carry

    lax.fori_loop(0, tm, issue, 0, unroll=4)
    for k in range(TOP_K):
        pltpu.make_async_copy(hm_ref, xs_ref.at[pl.ds(0, tm)], row_sems.at[k]).wait()


def _dispatch(pos_tiles, hm):
    tm = TM_DISPATCH
    return pl.pallas_call(
        _dispatch_kernel,
        grid=(TOKENS // tm,),
        in_specs=[
            pl.BlockSpec(memory_space=pl.ANY),
            pl.BlockSpec((tm, D_MODEL), lambda i: (i, 0)),
        ],
        out_specs=pl.BlockSpec(memory_space=pl.ANY),
        out_shape=jax.ShapeDtypeStruct((N_SORTED_ROWS, D_MODEL), F32),
        scratch_shapes=[
            pltpu.SMEM((TOP_K * tm // LANES, LANES), jnp.int32),
            pltpu.SemaphoreType.DMA,
            pltpu.SemaphoreType.DMA((TOP_K,)),
        ],
        compiler_params=pltpu.CompilerParams(
            dimension_semantics=("arbitrary",), vmem_limit_bytes=VMEM_LIMIT, has_side_effects=True),
        name="dispatch",
    )(pos_tiles, hm)


def _experts_kernel(te_ref, ts_ref, nr_ref, xs_ref, wup_ref, bup_ref, wdn_ref, bdn_ref, ys_ref):
    del te_ref, ts_ref
    n_rows = nr_ref[pl.program_id(0)]

    @pl.when(n_rows > 0)
    def _():
        row = lax.broadcasted_iota(jnp.int32, xs_ref.shape, 0)
        x = jnp.where(row < n_rows, xs_ref[...], 0.0).astype(BF16)
        gu = jnp.dot(x, wup_ref[0], preferred_element_type=F32) + bup_ref[0]
        x_glu = jnp.minimum(gu[:, :D_FF], SWIGLU_LIMIT)
        x_lin = jnp.clip(gu[:, D_FF:], -SWIGLU_LIMIT, SWIGLU_LIMIT)
        act = x_glu * jax.nn.sigmoid(SWIGLU_ALPHA * x_glu) * (x_lin + 1.0)
        ys_ref[...] = jnp.dot(act.astype(BF16), wdn_ref[0], preferred_element_type=F32) + bdn_ref[0]


def _experts(tile_expert, tile_src, tile_rows, xs, wup, bup, wdn, bdn):
    tm = TM_EXPERT
    grid_spec = pltpu.PrefetchScalarGridSpec(
        num_scalar_prefetch=3,
        grid=(N_EXPERT_TILES,),
        in_specs=[
            pl.BlockSpec((tm, D_MODEL), lambda i, te, ts, nv: (ts[i], 0)),
            pl.BlockSpec((1, D_MODEL, 2 * D_FF), lambda i, te, ts, nv: (te[i], 0, 0)),
            pl.BlockSpec((1, 1, 2 * D_FF), lambda i, te, ts, nv: (te[i], 0, 0)),
            pl.BlockSpec((1, D_FF, D_MODEL), lambda i, te, ts, nv: (te[i], 0, 0)),
            pl.BlockSpec((1, 1, D_MODEL), lambda i, te, ts, nv: (te[i], 0, 0)),
        ],
        out_specs=pl.BlockSpec((tm, D_MODEL), lambda i, te, ts, nv: (ts[i], 0)),
    )
    return pl.pallas_call(
        _experts_kernel,
        grid_spec=grid_spec,
        out_shape=jax.ShapeDtypeStruct((N_SORTED_ROWS, D_MODEL), F32),
        compiler_params=pltpu.CompilerParams(
            dimension_semantics=("arbitrary",), vmem_limit_bytes=VMEM_LIMIT),
        name="experts",
    )(tile_expert, tile_src, tile_rows, xs, wup, bup, wdn, bdn)


def _combine_kernel(pos_hbm, ys_hbm, x1_ref, gate_ref, p_ref, g_ref, wg_ref, wp_ref, gf_ref,
                    o_ref, pos_smem, rows_ref, pos_sem, row_sems, *, apply_final_norm):
    i = pl.program_id(0)
    tm = x1_ref.shape[0]
    load = pltpu.make_async_copy(pos_hbm.at[i], pos_smem, pos_sem)
    load.start()
    load.wait()

    def issue(t, carry):
        for k in range(TOP_K):
            pltpu.make_async_copy(ys_hbm.at[pl.ds(_pos_at(pos_smem, k, t, tm), 1)],
                                  rows_ref.at[k, pl.ds(t, 1)], row_sems.at[k]).start(priority=k % 2)
        return carry

    lax.fori_loop(0, tm, issue, 0, unroll=4)
    for k in range(TOP_K):
        pltpu.make_async_copy(ys_hbm.at[pl.ds(0, tm)], rows_ref.at[k], row_sems.at[k]).wait()

    x2 = x1_ref[...]
    for k in range(TOP_K):
        x2 = x2 + rows_ref[k] * gate_ref[:, k:k + 1]
    hp = _rms(x2, g_ref[...]).astype(BF16)
    gate = jax.nn.sigmoid(jnp.dot(hp, wg_ref[...], preferred_element_type=F32))
    pe = jnp.dot(p_ref[...].astype(BF16), wp_ref[...], preferred_element_type=F32)
    x3 = x2 + pe * gate
    if apply_final_norm:
        x3 = _rms(x3, gf_ref[...])
    o_ref[...] = x3


def _combine(pos_tiles, ys, x1, gates_t, p, g, wg, wp, gf, apply_final_norm):
    tm = TM_COMBINE
    const = lambda i: (0, 0)
    row_blk = lambda w: pl.BlockSpec((tm, w), lambda i: (i, 0))
    kernel = functools.partial(_combine_kernel, apply_final_norm=apply_final_norm)
    return pl.pallas_call(
        kernel,
        grid=(TOKENS // tm,),
        in_specs=[
            pl.BlockSpec(memory_space=pl.ANY),
            pl.BlockSpec(memory_space=pl.ANY),
            row_blk(D_MODEL), row_blk(LANES), row_blk(PLE_DIM),
            pl.BlockSpec((1, D_MODEL), const),
            pl.BlockSpec((D_MODEL, D_MODEL), const),
            pl.BlockSpec((PLE_DIM, D_MODEL), const),
            pl.BlockSpec((1, D_MODEL), const),
        ],
        out_specs=row_blk(D_MODEL),
        out_shape=jax.ShapeDtypeStruct((TOKENS, D_MODEL), F32),
        scratch_shapes=[
            pltpu.SMEM((TOP_K * tm // LANES, LANES), jnp.int32),
            pltpu.VMEM((TOP_K, tm, D_MODEL), F32),
            pltpu.SemaphoreType.DMA,
            pltpu.SemaphoreType.DMA((TOP_K,)),
        ],
        compiler_params=pltpu.CompilerParams(
            dimension_semantics=("arbitrary",), vmem_limit_bytes=VMEM_LIMIT),
        name="combine",
    )(pos_tiles, ys, x1, gates_t, p, g, wg, wp, gf)


def _in_proj_columns():
    src = np.full((Z_COLS,), -1, np.int64)
    src[Z_CQ:Z_CQ + Q_LORA] = np.arange(Q_LORA)
    src[Z_CKV:Z_CKV + KV_LORA] = Q_LORA + np.arange(KV_LORA)
    src[Z_KROPE + NOPE_A:Z_KROPE + NOPE_A + ROPE_A] = Q_LORA + KV_LORA + np.arange(ROPE_A)
    b0 = A_COLS
    for j in range(HB // 2):
        src[Z_BQ + j * LANES:Z_BQ + j * LANES + HALF] = b0 + j * HEAD_DIM + np.arange(HEAD_DIM)
        src[Z_BQ + j * LANES + HALF:Z_BQ + (j + 1) * LANES] = b0 + (HB // 2 + j) * HEAD_DIM + np.arange(HEAD_DIM)
    src[Z_BK:Z_BK + LANES] = b0 + HB * HEAD_DIM + np.arange(LANES)
    src[Z_BV:Z_BV + LANES] = b0 + (HB + KVB) * HEAD_DIM + np.arange(LANES)
    c0 = A_COLS + B_COLS
    for j in range(HC // 2):
        src[Z_CQS + j * LANES:Z_CQS + j * LANES + HALF] = c0 + j * HEAD_DIM + np.arange(HEAD_DIM)
        src[Z_CQS + j * LANES + HALF:Z_CQS + (j + 1) * LANES] = c0 + (HC // 2 + j) * HEAD_DIM + np.arange(HEAD_DIM)
    src[Z_CK:Z_CK + LANES] = c0 + HC * HEAD_DIM + np.arange(LANES)
    src[Z_CV:Z_CV + LANES] = c0 + (HC + KVC) * HEAD_DIM + np.arange(LANES)
    return src


def _wq_columns():
    src = np.full((HA * LANES,), -1, np.int64)
    dq = NOPE_A + ROPE_A
    for h in range(HA):
        src[h * LANES:h * LANES + dq] = h * dq + np.arange(dq)
    return src


def _wkv_columns():
    src = np.full((HA * LANES + (HA // 2) * LANES,), -1, np.int64)
    dkv = NOPE_A + V_A
    for h in range(HA):
        src[h * LANES:h * LANES + NOPE_A] = h * dkv + np.arange(NOPE_A)
        v0 = HA * LANES + (h // 2) * LANES + (h % 2) * HALF
        src[v0:v0 + V_A] = h * dkv + NOPE_A + np.arange(V_A)
    return src


def _paired_rows(base, n_heads):
    rows = []
    for j in range(n_heads // 2):
        rows.append(base + j * HEAD_DIM + np.arange(HEAD_DIM))
        rows.append(base + (n_heads // 2 + j) * HEAD_DIM + np.arange(HEAD_DIM))
    return np.concatenate(rows)


def _take_cols(w, src):
    cols = jnp.take(w, jnp.asarray(np.maximum(src, 0)), axis=1)
    return jnp.where(jnp.asarray(src >= 0)[None, :], cols, 0.0)


def _rope_tables():
    def cos_sin(pos, dim):
        inv = 1.0 / (ROPE_THETA ** (jnp.arange(0, dim, 2, dtype=F32) / dim))
        ang = pos.astype(F32)[:, None] * inv[None, :]
        return jnp.cos(ang), jnp.sin(ang)

    pos = jnp.arange(SEQ, dtype=jnp.int32)
    rows = pos // GRID_W
    cols = pos % GRID_W
    zeros = lambda w: jnp.zeros((SEQ, w), F32)
    ones = lambda w: jnp.ones((SEQ, w), F32)

    cos_a, sin_a = cos_sin(pos, ROPE_A)
    c_a = jnp.concatenate([ones(NOPE_A), cos_a, cos_a, ones(32)], axis=1)
    up_a = jnp.concatenate([zeros(NOPE_A), -sin_a, zeros(16), zeros(32)], axis=1)
    dn_a = jnp.concatenate([zeros(NOPE_A), zeros(16), sin_a, zeros(32)], axis=1)

    cos_r, sin_r = cos_sin(rows, HEAD_DIM // 2)
    cos_w, sin_w = cos_sin(cols, HEAD_DIM // 2)
    z16 = zeros(16)
    c_b = jnp.concatenate([cos_r, cos_r, cos_w, cos_w] * 2, axis=1)
    up_b = jnp.concatenate([-sin_r, z16, -sin_w, z16] * 2, axis=1)
    dn_b = jnp.concatenate([z16, sin_r, z16, sin_w] * 2, axis=1)

    cos_c, sin_c = cos_sin(pos, HEAD_DIM)
    z32 = zeros(32)
    c_c = jnp.concatenate([cos_c, cos_c] * 2, axis=1)
    up_c = jnp.concatenate([-sin_c, z32] * 2, axis=1)
    dn_c = jnp.concatenate([z32, sin_c] * 2, axis=1)
    return jnp.concatenate([c_a, up_a, dn_a, c_b, up_b, dn_b, c_c, up_c, dn_c], axis=1)


def _routing_tables(idx, rank, counts):
    tiles_per_expert = (counts + TM_EXPERT - 1) // TM_EXPERT
    tile_end = jnp.cumsum(tiles_per_expert)
    tile_start = tile_end - tiles_per_expert
    row_start = tile_start * TM_EXPERT
    experts = jnp.arange(N_EXPERTS, dtype=jnp.int32)
    pos = jnp.sum(jnp.where(idx[..., None] == experts, row_start, 0), axis=-1) + rank
    n_valid = tile_end[-1]
    tile_ids = jnp.arange(N_EXPERT_TILES, dtype=jnp.int32)
    tile_src = jnp.minimum(tile_ids, n_valid - 1)
    tile_expert = jnp.minimum(jnp.sum(tile_src[:, None] >= tile_end[None, :], axis=-1), N_EXPERTS - 1)
    hot = tile_expert[:, None] == experts
    rows_left = jnp.sum(jnp.where(hot, counts - (tile_src[:, None] - tile_start) * TM_EXPERT, 0), axis=-1)
    tile_rows = jnp.where(tile_ids < n_valid, jnp.clip(rows_left, 0, TM_EXPERT), 0)
    return pos, tile_expert.astype(jnp.int32), tile_src.astype(jnp.int32), tile_rows.astype(jnp.int32)


def _pos_tiles(pos, tm):
    return pos.reshape(TOP_K, TOKENS // tm, tm).transpose(1, 0, 2).reshape(TOKENS // tm, TOP_K * tm // LANES, LANES)


def kernel(x, p, attn_norm, w_in, mla_q_norm, mla_wq_up, mla_kv_norm, mla_wkv_up, gqa_q_norm, gqa_k_norm,
           swa_sink, w_out, moe_norm, router_w, router_b, w_up, b_up, w_down, b_down, ple_norm, w_ple,
           w_ple_gate, final_norm):
    tabs = _rope_tables()
    in_cols, wq_cols, wkv_cols = _in_proj_columns(), _wq_columns(), _wkv_columns()
    rows_c = _paired_rows(HA * V_A + HB * HEAD_DIM, HC)

    xf = x.reshape(TOKENS, D_MODEL)
    for i in range(DEPTH):
        win = _take_cols(w_in[i], in_cols).astype(BF16)
        wq = _take_cols(mla_wq_up[i], wq_cols).astype(BF16)
        wkv = _take_cols(mla_wkv_up[i], wkv_cols).astype(BF16)
        gbq = jnp.tile(gqa_q_norm[i], 2)[None, :]
        gbk = jnp.tile(gqa_k_norm[i], 2)[None, :]
        qa, ka, va, qb, kb, vb, qc, kc, vc = _in_proj(
            xf.reshape(BATCH, SEQ, D_MODEL), attn_norm[i][None, :], win, mla_q_norm[i][None, :], wq,
            mla_kv_norm[i][None, :], wkv, gbq, gbk, tabs)

        oa = _dense_attention(qa, ka, va, HA, lambda h: h, lambda h: h, (NOPE_A + ROPE_A) ** -0.5, "attn_a")
        ob = _dense_attention(qb, kb, vb, HB, lambda h: h * 0, lambda h: h // (HB // KVB),
                              HEAD_DIM ** -0.5, "attn_b")
        oc = _window_attention(swa_sink[i], qc, kc, vc)

        wo = w_out[i]
        x1, hm, idx, gates_t, rank, cnt = _out_proj(
            xf, oa.reshape(TOKENS, -1), ob.reshape(TOKENS, -1), oc.reshape(TOKENS, -1),
            wo[:HA * V_A].astype(BF16), wo[HA * V_A:HA * V_A + HB * HEAD_DIM].astype(BF16),
            jnp.take(wo, jnp.asarray(rows_c), axis=0).astype(BF16),
            moe_norm[i][None, :], router_w[i].T, router_b[i][:, None])

        pos, tile_expert, tile_src, tile_rows = _routing_tables(idx, rank, cnt[:, 0])
        xs = _dispatch(_pos_tiles(pos, TM_DISPATCH), hm)
        ys = _experts(tile_expert, tile_src, tile_rows, xs, w_up[i].astype(BF16), b_up[i][:, None, :],
                      w_down[i].astype(BF16), b_down[i][:, None, :])
        xf = _combine(_pos_tiles(pos, TM_COMBINE), ys, x1, gates_t, p[i].reshape(TOKENS, PLE_DIM),
                      ple_norm[i][None, :], w_ple_gate[i].astype(BF16), w_ple[i].astype(BF16),
                      final_norm[None, :], apply_final_norm=(i == DEPTH - 1))
    return xf.reshape(BATCH, SEQ, D_MODEL)
```

```python
import functools
import math

import numpy as np
import jax
import jax.numpy as jnp
from jax import lax
from jax.experimental import pallas as pl
from jax.experimental.pallas import tpu as pltpu

F32 = jnp.float32
BF16 = jnp.bfloat16

D_MODEL = 1024
BATCH = 8
SEQ = 4096
DEPTH = 2
TOKENS = BATCH * SEQ
GRID_W = 64
PLE_DIM = 256
HEAD_DIM = 64
ROPE_THETA = 10000.0
EPS = 1e-6
HA, Q_LORA, KV_LORA, NOPE_A, ROPE_A, V_A = 6, 256, 128, 64, 32, 64
HB, KVB = 6, 2
HC, KVC, WINDOW = 4, 2, 128
A_COLS = Q_LORA + KV_LORA + ROPE_A
B_COLS = (HB + 2 * KVB) * HEAD_DIM
C_COLS = (HC + 2 * KVC) * HEAD_DIM
N_EXPERTS = 32
TOP_K = 4
D_FF = D_MODEL
SWIGLU_LIMIT = 7.0
SWIGLU_ALPHA = 1.702

LANES = 128
HALF = LANES // 2
BF16_SUBLANES = 16
ROW_TILES = D_MODEL // LANES

VT_ROWS = HEAD_DIM + BF16_SUBLANES

TM_IN = 512
TQ = 512
TQ_SUB = 256
TM_OUT = 512
TM_DISPATCH = 512
TM_EXPERT = 512
TM_COMBINE = 256
ZERO_ROWS = 64
N_EXPERT_TILES = TOKENS * TOP_K // TM_EXPERT + N_EXPERTS
N_SORTED_ROWS = N_EXPERT_TILES * TM_EXPERT
VMEM_LIMIT = 52 * 1024 * 1024

Z_CQ = 0
Z_CKV = Q_LORA
Z_KROPE = Z_CKV + KV_LORA
Z_BQ = Z_KROPE + LANES
Z_BK = Z_BQ + 3 * LANES
Z_BV = Z_BK + LANES
Z_CQS = Z_BV + LANES
Z_CK = Z_CQS + 2 * LANES
Z_CV = Z_CK + LANES
Z_COLS = Z_CV + LANES

NEG_BIG = -1e30
LOG2E = math.log2(math.e)


def _rms(x, g):
    return x * lax.rsqrt(jnp.mean(x * x, axis=-1, keepdims=True) + EPS) * g


def _lane_is_low(shape):
    return lax.broadcasted_iota(jnp.int32, shape, len(shape) - 1) < HALF


def _rms_per_half(xs, g):
    low = _lane_is_low(xs.shape)
    x2 = xs * xs
    s_lo = jnp.sum(jnp.where(low, x2, 0.0), axis=-1, keepdims=True)
    s_hi = jnp.sum(jnp.where(low, 0.0, x2), axis=-1, keepdims=True)
    ms = jnp.where(low, s_lo, s_hi) * (1.0 / HEAD_DIM)
    return xs * lax.rsqrt(ms + EPS) * g


def _token_tile_spec(tm, index):
    return pl.BlockSpec((tm * ROW_TILES, LANES), lambda i, *prefetch: (index(i, *prefetch), 0))


def _token_tile(ref, row):
    return ref.at[pl.ds(pl.multiple_of(row * ROW_TILES, ROW_TILES), ROW_TILES)]


def _token_tiles(ref, n_rows):
    return ref.at[pl.ds(0, n_rows * ROW_TILES)]


def _store_token_tiles(ref, x):
    tm = x.shape[0]
    for c in range(ROW_TILES):
        ref[pl.ds(c, tm, stride=ROW_TILES), :] = x[:, c * LANES:(c + 1) * LANES]


def _load_token_tiles(ref):
    tm = ref.shape[0] // ROW_TILES
    return jnp.concatenate([ref[pl.ds(c, tm, stride=ROW_TILES), :] for c in range(ROW_TILES)], axis=1)


def _in_proj_kernel(x_ref, g_ref, win_ref, gq_ref, wq_ref, gkv_ref, wkv_ref, gbq_ref, gbk_ref, tab_ref,
                    qa_ref, ka_ref, va_ref, qb_ref, kb_ref, vb_ref, qc_ref, kc_ref, vc_ref):
    h = _rms(x_ref[0], g_ref[...])
    z = jnp.dot(h.astype(BF16), win_ref[...], preferred_element_type=F32)
    tm = z.shape[0]
    ones = jnp.ones((tm, LANES), BF16)
    ones_t = jnp.ones((VT_ROWS - HEAD_DIM, tm), BF16)
    low = _lane_is_low((tm, LANES))
    top = lax.broadcasted_iota(jnp.int32, (LANES, tm), 0) < HALF

    def rope(xs, table, shift):
        base = table * 3 * LANES
        c = tab_ref[:, base:base + LANES]
        s_up = tab_ref[:, base + LANES:base + 2 * LANES]
        s_dn = tab_ref[:, base + 2 * LANES:base + 3 * LANES]
        return xs * c + pltpu.roll(xs, LANES - shift, 1) * s_up + pltpu.roll(xs, shift, 1) * s_dn

    c_q = _rms(z[:, Z_CQ:Z_CQ + Q_LORA], gq_ref[...])
    q = jnp.dot(c_q.astype(BF16), wq_ref[...], preferred_element_type=F32)
    c_kv = _rms(z[:, Z_CKV:Z_CKV + KV_LORA], gkv_ref[...])
    kv = jnp.dot(c_kv.astype(BF16), wkv_ref[...], preferred_element_type=F32)
    k_rope = rope(z[:, Z_KROPE:Z_KROPE + LANES], 0, ROPE_A // 2)
    for hd in range(HA):
        qa_ref[0, hd] = rope(q[:, hd * LANES:(hd + 1) * LANES], 0, ROPE_A // 2).T.astype(BF16)
        ka_ref[0, hd] = (kv[:, hd * LANES:(hd + 1) * LANES] + k_rope).astype(BF16)
    for j in range(HA // 2):
        vt = kv[:, (HA + j) * LANES:(HA + j + 1) * LANES].T.astype(BF16)
        for half in range(2):
            va_ref[0, 2 * j + half, 0:HEAD_DIM, :] = vt[half * HALF:(half + 1) * HALF]
            va_ref[0, 2 * j + half, HEAD_DIM:VT_ROWS, :] = ones_t

    for j in range(HB // 2):
        s = rope(_rms_per_half(z[:, Z_BQ + j * LANES:Z_BQ + (j + 1) * LANES], gbq_ref[...]), 1, HEAD_DIM // 4)
        st = s.T
        qb_ref[0, j] = jnp.where(top, st, 0.0).astype(BF16)
        qb_ref[0, HB // 2 + j] = jnp.where(top, 0.0, st).astype(BF16)
    kb_ref[0, 0] = rope(_rms_per_half(z[:, Z_BK:Z_BK + LANES], gbk_ref[...]), 1, HEAD_DIM // 4).astype(BF16)
    vt = z[:, Z_BV:Z_BV + LANES].T.astype(BF16)
    for g in range(KVB):
        vb_ref[0, g, 0:HEAD_DIM, :] = vt[g * HALF:(g + 1) * HALF]
        vb_ref[0, g, HEAD_DIM:VT_ROWS, :] = ones_t

    for j in range(HC // 2):
        s = rope(z[:, Z_CQS + j * LANES:Z_CQS + (j + 1) * LANES], 2, HEAD_DIM // 2)
        qc_ref[0, j] = jnp.where(low, s, 0.0).astype(BF16)
        qc_ref[0, HC // 2 + j] = jnp.where(low, 0.0, s).astype(BF16)
    kc_ref[0, 0] = rope(z[:, Z_CK:Z_CK + LANES], 2, HEAD_DIM // 2).astype(BF16)
    vc_ref[0, 0, :, 0:LANES] = z[:, Z_CV:Z_CV + LANES].astype(BF16)
    vc_ref[0, 0, :, LANES:2 * LANES] = ones


def _in_proj(x3, g, win, gq, wq, gkv, wkv, gbq, gbk, tabs):
    nst = SEQ // TM_IN
    const2 = lambda b, s: (0, 0)
    head_out = lambda n, w: (jax.ShapeDtypeStruct((BATCH, n, SEQ, w), BF16),
                             pl.BlockSpec((1, n, TM_IN, w), lambda b, s: (b, 0, s, 0)))
    head_out_t = lambda n, r: (jax.ShapeDtypeStruct((BATCH, n, r, SEQ), BF16),
                               pl.BlockSpec((1, n, r, TM_IN), lambda b, s: (b, 0, 0, s)))
    outs = [head_out_t(HA, LANES), head_out(HA, LANES), head_out_t(HA, VT_ROWS),
            head_out_t(HB, LANES), head_out(1, LANES), head_out_t(KVB, VT_ROWS),
            head_out(HC, LANES), head_out(1, LANES), head_out(1, 2 * LANES)]
    return pl.pallas_call(
        _in_proj_kernel,
        grid=(BATCH, nst),
        in_specs=[
            pl.BlockSpec((1, TM_IN, D_MODEL), lambda b, s: (b, s, 0)),
            pl.BlockSpec((1, D_MODEL), const2),
            pl.BlockSpec((D_MODEL, Z_COLS), const2),
            pl.BlockSpec((1, Q_LORA), const2),
            pl.BlockSpec((Q_LORA, HA * LANES), const2),
            pl.BlockSpec((1, KV_LORA), const2),
            pl.BlockSpec((KV_LORA, HA * LANES + (HA // 2) * LANES), const2),
            pl.BlockSpec((1, LANES), const2),
            pl.BlockSpec((1, LANES), const2),
            pl.BlockSpec((TM_IN, 9 * LANES), lambda b, s: (s, 0)),
        ],
        out_specs=[o[1] for o in outs],
        out_shape=[o[0] for o in outs],
        compiler_params=pltpu.CompilerParams(
            dimension_semantics=("arbitrary", "arbitrary"), vmem_limit_bytes=VMEM_LIMIT),
        name="in_proj",
    )(x3, g, win, gq, wq, gkv, wkv, gbq, gbk, tabs)


def _attn_kernel(q1_ref, q2_ref, k1_ref, k2_ref, v1_ref, v2_ref, o_ref, *, c):
    n_sub = q1_ref.shape[3] // TQ_SUB

    def scores(q_ref, k_ref, sub):
        qt = q_ref[0, 0, :, sub * TQ_SUB:(sub + 1) * TQ_SUB]
        return jnp.dot(k_ref[0, 0], qt, preferred_element_type=F32)

    def probs(st):
        m = jnp.max(st, axis=0, keepdims=True)
        return jnp.exp2((st - m) * c).astype(BF16)

    def values(v_ref, pt):
        acc = jnp.dot(v_ref[0, 0], pt, preferred_element_type=F32)
        return acc[:HEAD_DIM] / acc[HEAD_DIM:HEAD_DIM + 1]

    st = [scores(q1_ref, k1_ref, 0), scores(q2_ref, k2_ref, 0)]
    for sub in range(n_sub):
        st_next = ([scores(q1_ref, k1_ref, sub + 1), scores(q2_ref, k2_ref, sub + 1)]
                   if sub + 1 < n_sub else None)
        pt = [probs(st[0]), probs(st[1])]
        ot = jnp.concatenate([values(v1_ref, pt[0]), values(v2_ref, pt[1])], axis=0)
        o_ref[0, sub * TQ_SUB:(sub + 1) * TQ_SUB, :] = ot.T.astype(o_ref.dtype)
        st = st_next


def _dense_attention(qt, k, vt, n_heads, k_head, v_head, scale, name):
    kernel = functools.partial(_attn_kernel, c=scale * LOG2E)
    q_spec = lambda off: pl.BlockSpec((1, 1, LANES, TQ), lambda b, p, i: (b, 2 * p + off, 0, i))
    k_spec = lambda off: pl.BlockSpec((1, 1, SEQ, LANES), lambda b, p, i: (b, k_head(2 * p + off), 0, 0))
    v_spec = lambda off: pl.BlockSpec((1, 1, VT_ROWS, SEQ), lambda b, p, i: (b, v_head(2 * p + off), 0, 0))
    return pl.pallas_call(
        kernel,
        grid=(BATCH, n_heads // 2, SEQ // TQ),
        in_specs=[q_spec(0), q_spec(1), k_spec(0), k_spec(1), v_spec(0), v_spec(1)],
        out_specs=pl.BlockSpec((1, TQ, LANES), lambda b, p, i: (b, i, p)),
        out_shape=jax.ShapeDtypeStruct((BATCH, SEQ, n_heads * HEAD_DIM), BF16),
        compiler_params=pltpu.CompilerParams(
            dimension_semantics=("arbitrary", "arbitrary", "arbitrary"), vmem_limit_bytes=VMEM_LIMIT),
        name=name,
    )(qt, qt, k, k, vt, vt)


def _win_kernel(sink_ref, q1_ref, q2_ref, k_ref, v_ref, o_ref, *, scale):
    pair = pl.program_id(1)
    n = pl.program_id(2)
    nb = SEQ // WINDOW
    start = pl.multiple_of(jnp.clip(n - 1, 0, nb - 3) * WINDOW, WINDOW)
    k = k_ref[0, 0, pl.ds(start, 3 * WINDOW), :]
    v = v_ref[0, 0, pl.ds(start, 3 * WINDOW), :]
    qpos = n * WINDOW + lax.broadcasted_iota(jnp.int32, (WINDOW, 3 * WINDOW), 0)
    kpos = start + lax.broadcasted_iota(jnp.int32, (WINDOW, 3 * WINDOW), 1)
    band = jnp.abs(qpos - kpos) <= WINDOW

    def one_head(q, sink):
        s = lax.dot_general(q, k, (((1,), (1,)), ((), ())), preferred_element_type=F32) * scale
        s = jnp.where(band, s, NEG_BIG)
        m = jnp.maximum(jnp.max(s, axis=-1, keepdims=True), sink)
        p = jnp.exp(s - m)
        acc = jnp.dot(p.astype(BF16), v, preferred_element_type=F32)
        return acc[:, :LANES] / (acc[:, LANES:] + jnp.exp(sink - m))

    o1 = one_head(q1_ref[0, 0], sink_ref[pair])
    o2 = one_head(q2_ref[0, 0], sink_ref[HC // 2 + pair])
    o_ref[0] = jnp.where(_lane_is_low((WINDOW, LANES)), o1, o2).astype(o_ref.dtype)


def _window_attention(sink, q, k, vext):
    kernel = functools.partial(_win_kernel, scale=HEAD_DIM ** -0.5)
    n_pairs = HC // 2
    return pl.pallas_call(
        kernel,
        grid=(BATCH, n_pairs, SEQ // WINDOW),
        in_specs=[
            pl.BlockSpec(memory_space=pltpu.SMEM),
            pl.BlockSpec((1, 1, WINDOW, LANES), lambda b, p, i: (b, p, i, 0)),
            pl.BlockSpec((1, 1, WINDOW, LANES), lambda b, p, i: (b, n_pairs + p, i, 0)),
            pl.BlockSpec((1, 1, SEQ, LANES), lambda b, p, i: (b, 0, 0, 0)),
            pl.BlockSpec((1, 1, SEQ, 2 * LANES), lambda b, p, i: (b, 0, 0, 0)),
        ],
        out_specs=pl.BlockSpec((1, WINDOW, LANES), lambda b, p, i: (b, i, p)),
        out_shape=jax.ShapeDtypeStruct((BATCH, SEQ, n_pairs * LANES), BF16),
        compiler_params=pltpu.CompilerParams(
            dimension_semantics=("arbitrary", "arbitrary", "arbitrary"), vmem_limit_bytes=VMEM_LIMIT),
        name="win_attn",
    )(sink, q, q, k, vext)


def _out_proj_kernel(x_ref, oa_ref, ob_ref, oc_ref, wa_ref, wb_ref, wc_ref, g_ref, rwt_ref, rb_ref,
                     x1_ref, hm_ref, idx_ref, gate_ref, rank_ref, cnt_ref, run_ref):
    @pl.when(pl.program_id(0) == 0)
    def _():
        run_ref[...] = jnp.zeros_like(run_ref)

    x1 = (x_ref[...]
          + jnp.dot(oa_ref[...], wa_ref[...], preferred_element_type=F32)
          + jnp.dot(ob_ref[...], wb_ref[...], preferred_element_type=F32)
          + jnp.dot(oc_ref[...], wc_ref[...], preferred_element_type=F32))
    x1_ref[...] = x1
    hm = _rms(x1, g_ref[...])
    _store_token_tiles(hm_ref, hm)
    tm = hm.shape[0]

    logits = lax.dot_general(rwt_ref[...], hm, (((1,), (1,)), ((), ())),
                             precision=lax.Precision.HIGHEST, preferred_element_type=F32) + rb_ref[...]
    eidx = lax.broadcasted_iota(jnp.int32, (N_EXPERTS, tm), 0)
    vals, sels, hots = [], [], []
    cur = logits
    for _ in range(TOP_K):
        mx = jnp.max(cur, axis=0, keepdims=True)
        sel = jnp.min(jnp.where(cur == mx, eidx, N_EXPERTS), axis=0, keepdims=True)
        hot = eidx == sel
        vals.append(mx)
        sels.append(sel)
        hots.append(hot)
        cur = jnp.where(hot, -jnp.inf, cur)
    exps = [jnp.exp(v - vals[0]) for v in vals]
    denom = exps[0] + exps[1] + exps[2] + exps[3]
    gates = jnp.concatenate([e / denom for e in exps] + [jnp.zeros((LANES - TOP_K, tm), F32)], axis=0)
    gate_ref[...] = gates.T
    idx_ref[...] = jnp.concatenate(sels, axis=0)

    hot_all = jnp.zeros((N_EXPERTS, tm), F32)
    for hot in hots:
        hot_all = hot_all + jnp.where(hot, 1.0, 0.0)
    row = lax.broadcasted_iota(jnp.int32, (tm, tm), 0)
    col = lax.broadcasted_iota(jnp.int32, (tm, tm), 1)
    before = jnp.where(row < col, 1.0, 0.0).astype(BF16)
    rank_full = jnp.dot(hot_all.astype(BF16), before, preferred_element_type=F32) + run_ref[...]
    ranks = [jnp.sum(jnp.where(hot, rank_full, 0.0), axis=0, keepdims=True) for hot in hots]
    rank_ref[...] = jnp.concatenate(ranks, axis=0).astype(jnp.int32)
    run_ref[...] = run_ref[...] + jnp.sum(hot_all, axis=1, keepdims=True)
    cnt_ref[...] = jnp.broadcast_to(run_ref[...], (N_EXPERTS, LANES)).astype(jnp.int32)


def _out_proj(x, oa, ob, oc, wa, wb, wc, g, rwt, rb):
    tm = TM_OUT
    const = lambda i: (0, 0)
    row_blk = lambda w: pl.BlockSpec((tm, w), lambda i: (i, 0))
    col_blk = pl.BlockSpec((TOP_K, tm), lambda i: (0, i))
    return pl.pallas_call(
        _out_proj_kernel,
        grid=(TOKENS // tm,),
        in_specs=[
            row_blk(D_MODEL), row_blk(oa.shape[1]), row_blk(ob.shape[1]), row_blk(oc.shape[1]),
            pl.BlockSpec(wa.shape, const), pl.BlockSpec(wb.shape, const), pl.BlockSpec(wc.shape, const),
            pl.BlockSpec((1, D_MODEL), const),
            pl.BlockSpec((N_EXPERTS, D_MODEL), const),
            pl.BlockSpec((N_EXPERTS, 1), const),
        ],
        out_specs=[row_blk(D_MODEL), _token_tile_spec(tm, lambda i: i), col_blk, row_blk(LANES), col_blk,
                   pl.BlockSpec((N_EXPERTS, LANES), const)],
        out_shape=[
            jax.ShapeDtypeStruct((TOKENS, D_MODEL), F32),
            jax.ShapeDtypeStruct((TOKENS * ROW_TILES, LANES), F32),
            jax.ShapeDtypeStruct((TOP_K, TOKENS), jnp.int32),
            jax.ShapeDtypeStruct((TOKENS, LANES), F32),
            jax.ShapeDtypeStruct((TOP_K, TOKENS), jnp.int32),
            jax.ShapeDtypeStruct((N_EXPERTS, LANES), jnp.int32),
        ],
        scratch_shapes=[pltpu.VMEM((N_EXPERTS, 1), F32)],
        compiler_params=pltpu.CompilerParams(
            dimension_semantics=("arbitrary",), vmem_limit_bytes=VMEM_LIMIT),
        name="out_proj",
    )(x, oa, ob, oc, wa, wb, wc, g, rwt, rb)


def _issue_row_copies(pos_smem, tm, copy):
    for chunk in range(tm // LANES):
        def issue(j, carry, chunk=chunk):
            for k in range(TOP_K):
                copy(chunk * LANES + j, k, pos_smem[k * (tm // LANES) + chunk, j])
            return carry
        lax.fori_loop(0, LANES, issue, 0, unroll=4)


def _dispatch_kernel(pad_start_ref, pad_count_ref, pos_hbm, hm_ref, xs_ref,
                     pos_smem, zero_ref, pos_sem, row_sems, pad_sem):
    i = pl.program_id(0)
    tm = hm_ref.shape[0] // ROW_TILES
    load = pltpu.make_async_copy(pos_hbm.at[i], pos_smem, pos_sem)
    load.start()

    @pl.when(i == 0)
    def _():
        zero_ref[...] = jnp.zeros_like(zero_ref)
        zero_row = _token_tile(zero_ref, 0)
        for e in range(N_EXPERTS):
            start = pad_start_ref[e]

            def fill(r, carry, start=start):
                pltpu.make_async_copy(zero_row, _token_tile(xs_ref, start + r), pad_sem).start()
                return carry

            def drain(r, carry):
                pltpu.make_async_copy(zero_row, _token_tile(xs_ref, 0), pad_sem).wait()
                return carry

            lax.fori_loop(0, pad_count_ref[e], fill, 0)
            lax.fori_loop(0, pad_count_ref[e], drain, 0)

        tail_start = pad_start_ref[N_EXPERTS]
        chunk = ZERO_ROWS * ROW_TILES

        def fill_tail(r, carry):
            dst = xs_ref.at[pl.ds(pl.multiple_of((tail_start + r * ZERO_ROWS) * ROW_TILES, chunk), chunk)]
            pltpu.make_async_copy(zero_ref, dst, pad_sem).start()
            return carry

        def drain_tail(r, carry):
            pltpu.make_async_copy(zero_ref, xs_ref.at[pl.ds(0, chunk)], pad_sem).wait()
            return carry

        lax.fori_loop(0, pad_count_ref[N_EXPERTS], fill_tail, 0)
        lax.fori_loop(0, pad_count_ref[N_EXPERTS], drain_tail, 0)

    load.wait()

    def copy(t, k, pos):
        pltpu.make_async_copy(_token_tile(hm_ref, t), _token_tile(xs_ref, pos),
                              row_sems.at[k]).start(priority=k % 2)

    _issue_row_copies(pos_smem, tm, copy)
    for k in range(TOP_K):
        pltpu.make_async_copy(hm_ref, _token_tiles(xs_ref, tm), row_sems.at[k]).wait()


def _dispatch(pad_start, pad_count, pos_tiles, hm):
    tm = TM_DISPATCH
    grid_spec = pltpu.PrefetchScalarGridSpec(
        num_scalar_prefetch=2,
        grid=(TOKENS // tm,),
        in_specs=[
            pl.BlockSpec(memory_space=pl.ANY),
            _token_tile_spec(tm, lambda i, ps, pc: i),
        ],
        out_specs=pl.BlockSpec(memory_space=pl.ANY),
        scratch_shapes=[
            pltpu.SMEM((TOP_K * tm // LANES, LANES), jnp.int32),
            pltpu.VMEM((ZERO_ROWS * ROW_TILES, LANES), F32),
            pltpu.SemaphoreType.DMA,
            pltpu.SemaphoreType.DMA((TOP_K,)),
            pltpu.SemaphoreType.DMA,
        ],
    )
    return pl.pallas_call(
        _dispatch_kernel,
        grid_spec=grid_spec,
        out_shape=jax.ShapeDtypeStruct((N_SORTED_ROWS * ROW_TILES, LANES), F32),
        compiler_params=pltpu.CompilerParams(
            dimension_semantics=("arbitrary",), vmem_limit_bytes=VMEM_LIMIT, has_side_effects=True),
        name="dispatch",
    )(pad_start, pad_count, pos_tiles, hm)


def _experts_kernel(te_ref, ts_ref, nr_ref, xs_ref, wup_ref, bup_ref, wdn_ref, bdn_ref, ys_ref):
    del te_ref, ts_ref
    n_rows = nr_ref[pl.program_id(0)]

    @pl.when(n_rows > 0)
    def _():
        x = _load_token_tiles(xs_ref).astype(BF16)
        gu = jnp.dot(x, wup_ref[0], preferred_element_type=F32) + bup_ref[0]
        x_glu = jnp.minimum(gu[:, :D_FF], SWIGLU_LIMIT)
        x_lin = jnp.clip(gu[:, D_FF:], -SWIGLU_LIMIT, SWIGLU_LIMIT)
        act = x_glu * jax.nn.sigmoid(SWIGLU_ALPHA * x_glu) * (x_lin + 1.0)
        _store_token_tiles(
            ys_ref, jnp.dot(act.astype(BF16), wdn_ref[0], preferred_element_type=F32) + bdn_ref[0])

    @pl.when(n_rows == 0)
    def _():
        ys_ref[...] = jnp.zeros_like(ys_ref)


def _experts(tile_expert, tile_src, tile_rows, xs, wup, bup, wdn, bdn):
    tm = TM_EXPERT
    grid_spec = pltpu.PrefetchScalarGridSpec(
        num_scalar_prefetch=3,
        grid=(N_EXPERT_TILES,),
        in_specs=[
            _token_tile_spec(tm, lambda i, te, ts, nr: ts[i]),
            pl.BlockSpec((1, D_MODEL, 2 * D_FF), lambda i, te, ts, nr: (te[i], 0, 0)),
            pl.BlockSpec((1, 1, 2 * D_FF), lambda i, te, ts, nr: (te[i], 0, 0)),
            pl.BlockSpec((1, D_FF, D_MODEL), lambda i, te, ts, nr: (te[i], 0, 0)),
            pl.BlockSpec((1, 1, D_MODEL), lambda i, te, ts, nr: (te[i], 0, 0)),
        ],
        out_specs=_token_tile_spec(tm, lambda i, te, ts, nr: i),
    )
    return pl.pallas_call(
        _experts_kernel,
        grid_spec=grid_spec,
        out_shape=jax.ShapeDtypeStruct((N_SORTED_ROWS * ROW_TILES, LANES), F32),
        compiler_params=pltpu.CompilerParams(
            dimension_semantics=("arbitrary",), vmem_limit_bytes=VMEM_LIMIT),
        name="experts",
    )(tile_expert, tile_src, tile_rows, xs, wup, bup, wdn, bdn)


def _combine_kernel(pos_hbm, ys_hbm, x1_ref, gate_ref, p_ref, g_ref, wg_ref, wp_ref, gf_ref,
                    o_ref, pos_smem, rows_ref, pos_sem, row_sems, *, apply_final_norm):
    i = pl.program_id(0)
    tm = x1_ref.shape[0]
    load = pltpu.make_async_copy(pos_hbm.at[i], pos_smem, pos_sem)
    load.start()
    load.wait()

    def copy(t, k, pos):
        pltpu.make_async_copy(_token_tile(ys_hbm, pos), _token_tile(rows_ref.at[k], t),
                              row_sems.at[k]).start(priority=k % 2)

    _issue_row_copies(pos_smem, tm, copy)
    for k in range(TOP_K):
        pltpu.make_async_copy(_token_tiles(ys_hbm, tm), rows_ref.at[k], row_sems.at[k]).wait()

    x2 = x1_ref[...]
    for k in range(TOP_K):
        x2 = x2 + _load_token_tiles(rows_ref.at[k]) * gate_ref[:, k:k + 1]
    hp = _rms(x2, g_ref[...]).astype(BF16)
    gate = jax.nn.sigmoid(jnp.dot(hp, wg_ref[...], preferred_element_type=F32))
    pe = jnp.dot(p_ref[...].astype(BF16), wp_ref[...], preferred_element_type=F32)
    x3 = x2 + pe * gate
    if apply_final_norm:
        x3 = _rms(x3, gf_ref[...])
    o_ref[...] = x3


def _combine(pos_tiles, ys, x1, gates_t, p, g, wg, wp, gf, apply_final_norm):
    tm = TM_COMBINE
    const = lambda i: (0, 0)
    row_blk = lambda w: pl.BlockSpec((tm, w), lambda i: (i, 0))
    kernel = functools.partial(_combine_kernel, apply_final_norm=apply_final_norm)
    return pl.pallas_call(
        kernel,
        grid=(TOKENS // tm,),
        in_specs=[
            pl.BlockSpec(memory_space=pl.ANY),
            pl.BlockSpec(memory_space=pl.ANY),
            row_blk(D_MODEL), row_blk(LANES), row_blk(PLE_DIM),
            pl.BlockSpec((1, D_MODEL), const),
            pl.BlockSpec((D_MODEL, D_MODEL), const),
            pl.BlockSpec((PLE_DIM, D_MODEL), const),
            pl.BlockSpec((1, D_MODEL), const),
        ],
        out_specs=row_blk(D_MODEL),
        out_shape=jax.ShapeDtypeStruct((TOKENS, D_MODEL), F32),
        scratch_shapes=[
            pltpu.SMEM((TOP_K * tm // LANES, LANES), jnp.int32),
            pltpu.VMEM((TOP_K, tm * ROW_TILES, LANES), F32),
            pltpu.SemaphoreType.DMA,
            pltpu.SemaphoreType.DMA((TOP_K,)),
        ],
        compiler_params=pltpu.CompilerParams(
            dimension_semantics=("arbitrary",), vmem_limit_bytes=VMEM_LIMIT),
        name="combine",
    )(pos_tiles, ys, x1, gates_t, p, g, wg, wp, gf)


def _in_proj_columns():
    src = np.full((Z_COLS,), -1, np.int64)
    src[Z_CQ:Z_CQ + Q_LORA] = np.arange(Q_LORA)
    src[Z_CKV:Z_CKV + KV_LORA] = Q_LORA + np.arange(KV_LORA)
    src[Z_KROPE + NOPE_A:Z_KROPE + NOPE_A + ROPE_A] = Q_LORA + KV_LORA + np.arange(ROPE_A)
    b0 = A_COLS
    for j in range(HB // 2):
        src[Z_BQ + j * LANES:Z_BQ + j * LANES + HALF] = b0 + j * HEAD_DIM + np.arange(HEAD_DIM)
        src[Z_BQ + j * LANES + HALF:Z_BQ + (j + 1) * LANES] = b0 + (HB // 2 + j) * HEAD_DIM + np.arange(HEAD_DIM)
    src[Z_BK:Z_BK + LANES] = b0 + HB * HEAD_DIM + np.arange(LANES)
    src[Z_BV:Z_BV + LANES] = b0 + (HB + KVB) * HEAD_DIM + np.arange(LANES)
    c0 = A_COLS + B_COLS
    for j in range(HC // 2):
        src[Z_CQS + j * LANES:Z_CQS + j * LANES + HALF] = c0 + j * HEAD_DIM + np.arange(HEAD_DIM)
        src[Z_CQS + j * LANES + HALF:Z_CQS + (j + 1) * LANES] = c0 + (HC // 2 + j) * HEAD_DIM + np.arange(HEAD_DIM)
    src[Z_CK:Z_CK + LANES] = c0 + HC * HEAD_DIM + np.arange(LANES)
    src[Z_CV:Z_CV + LANES] = c0 + (HC + KVC) * HEAD_DIM + np.arange(LANES)
    return src


def _wq_columns():
    src = np.full((HA * LANES,), -1, np.int64)
    dq = NOPE_A + ROPE_A
    for h in range(HA):
        src[h * LANES:h * LANES + dq] = h * dq + np.arange(dq)
    return src


def _wkv_columns():
    src = np.full((HA * LANES + (HA // 2) * LANES,), -1, np.int64)
    dkv = NOPE_A + V_A
    for h in range(HA):
        src[h * LANES:h * LANES + NOPE_A] = h * dkv + np.arange(NOPE_A)
        v0 = HA * LANES + (h // 2) * LANES + (h % 2) * HALF
        src[v0:v0 + V_A] = h * dkv + NOPE_A + np.arange(V_A)
    return src


def _paired_rows(base, n_heads):
    rows = []
    for j in range(n_heads // 2):
        rows.append(base + j * HEAD_DIM + np.arange(HEAD_DIM))
        rows.append(base + (n_heads // 2 + j) * HEAD_DIM + np.arange(HEAD_DIM))
    return np.concatenate(rows)


def _take_cols(w, src):
    cols = jnp.take(w, jnp.asarray(np.maximum(src, 0)), axis=1)
    return jnp.where(jnp.asarray(src >= 0)[None, :], cols, 0.0)


def _rope_tables():
    def cos_sin(pos, dim):
        inv = 1.0 / (ROPE_THETA ** (jnp.arange(0, dim, 2, dtype=F32) / dim))
        ang = pos.astype(F32)[:, None] * inv[None, :]
        return jnp.cos(ang), jnp.sin(ang)

    pos = jnp.arange(SEQ, dtype=jnp.int32)
    rows = pos // GRID_W
    cols = pos % GRID_W
    zeros = lambda w: jnp.zeros((SEQ, w), F32)
    ones = lambda w: jnp.ones((SEQ, w), F32)

    cos_a, sin_a = cos_sin(pos, ROPE_A)
    c_a = jnp.concatenate([ones(NOPE_A), cos_a, cos_a, ones(32)], axis=1)
    up_a = jnp.concatenate([zeros(NOPE_A), -sin_a, zeros(16), zeros(32)], axis=1)
    dn_a = jnp.concatenate([zeros(NOPE_A), zeros(16), sin_a, zeros(32)], axis=1)

    cos_r, sin_r = cos_sin(rows, HEAD_DIM // 2)
    cos_w, sin_w = cos_sin(cols, HEAD_DIM // 2)
    z16 = zeros(16)
    c_b = jnp.concatenate([cos_r, cos_r, cos_w, cos_w] * 2, axis=1)
    up_b = jnp.concatenate([-sin_r, z16, -sin_w, z16] * 2, axis=1)
    dn_b = jnp.concatenate([z16, sin_r, z16, sin_w] * 2, axis=1)

    cos_c, sin_c = cos_sin(pos, HEAD_DIM)
    z32 = zeros(32)
    c_c = jnp.concatenate([cos_c, cos_c] * 2, axis=1)
    up_c = jnp.concatenate([-sin_c, z32] * 2, axis=1)
    dn_c = jnp.concatenate([z32, sin_c] * 2, axis=1)
    return jnp.concatenate([c_a, up_a, dn_a, c_b, up_b, dn_b, c_c, up_c, dn_c], axis=1)


def _routing_tables(idx, rank, counts):
    tiles_per_expert = (counts + TM_EXPERT - 1) // TM_EXPERT
    tile_end = jnp.cumsum(tiles_per_expert)
    tile_start = tile_end - tiles_per_expert
    row_start = tile_start * TM_EXPERT
    experts = jnp.arange(N_EXPERTS, dtype=jnp.int32)
    pos = jnp.sum(jnp.where(idx[..., None] == experts, row_start, 0), axis=-1) + rank
    n_valid = tile_end[-1]
    tile_ids = jnp.arange(N_EXPERT_TILES, dtype=jnp.int32)
    tile_src = jnp.minimum(tile_ids, n_valid - 1)
    tile_expert = jnp.minimum(jnp.sum(tile_src[:, None] >= tile_end[None, :], axis=-1), N_EXPERTS - 1)
    hot = tile_expert[:, None] == experts
    rows_left = jnp.sum(jnp.where(hot, counts - (tile_src[:, None] - tile_start) * TM_EXPERT, 0), axis=-1)
    tile_rows = jnp.where(tile_ids < n_valid, jnp.clip(rows_left, 0, TM_EXPERT), 0)
    tail_start = n_valid * TM_EXPERT
    pad_start = jnp.concatenate([row_start + counts, tail_start[None]])
    pad_count = jnp.concatenate([tiles_per_expert * TM_EXPERT - counts,
                                 ((N_SORTED_ROWS - tail_start) // ZERO_ROWS)[None]])
    tables = (tile_expert, tile_src, tile_rows, pad_start, pad_count)
    return (pos,) + tuple(t.astype(jnp.int32) for t in tables)


def _pos_tiles(pos, tm):
    return pos.reshape(TOP_K, TOKENS // tm, tm).transpose(1, 0, 2).reshape(TOKENS // tm, TOP_K * tm // LANES, LANES)


def kernel(x, p, attn_norm, w_in, mla_q_norm, mla_wq_up, mla_kv_norm, mla_wkv_up, gqa_q_norm, gqa_k_norm,
           swa_sink, w_out, moe_norm, router_w, router_b, w_up, b_up, w_down, b_down, ple_norm, w_ple,
           w_ple_gate, final_norm):
    tabs = _rope_tables()
    in_cols, wq_cols, wkv_cols = _in_proj_columns(), _wq_columns(), _wkv_columns()
    rows_c = _paired_rows(HA * V_A + HB * HEAD_DIM, HC)

    xf = x.reshape(TOKENS, D_MODEL)
    for i in range(DEPTH):
        win = _take_cols(w_in[i], in_cols).astype(BF16)
        wq = _take_cols(mla_wq_up[i], wq_cols).astype(BF16)
        wkv = _take_cols(mla_wkv_up[i], wkv_cols).astype(BF16)
        gbq = jnp.tile(gqa_q_norm[i], 2)[None, :]
        gbk = jnp.tile(gqa_k_norm[i], 2)[None, :]
        qa, ka, va, qb, kb, vb, qc, kc, vc = _in_proj(
            xf.reshape(BATCH, SEQ, D_MODEL), attn_norm[i][None, :], win, mla_q_norm[i][None, :], wq,
            mla_kv_norm[i][None, :], wkv, gbq, gbk, tabs)

        oa = _dense_attention(qa, ka, va, HA, lambda h: h, lambda h: h, (NOPE_A + ROPE_A) ** -0.5, "attn_a")
        ob = _dense_attention(qb, kb, vb, HB, lambda h: h * 0, lambda h: h // (HB // KVB),
                              HEAD_DIM ** -0.5, "attn_b")
        oc = _window_attention(swa_sink[i], qc, kc, vc)

        wo = w_out[i]
        x1, hm, idx, gates_t, rank, cnt = _out_proj(
            xf, oa.reshape(TOKENS, -1), ob.reshape(TOKENS, -1), oc.reshape(TOKENS, -1),
            wo[:HA * V_A].astype(BF16), wo[HA * V_A:HA * V_A + HB * HEAD_DIM].astype(BF16),
            jnp.take(wo, jnp.asarray(rows_c), axis=0).astype(BF16),
            moe_norm[i][None, :], router_w[i].T, router_b[i][:, None])

        pos, tile_expert, tile_src, tile_rows, pad_start, pad_count = _routing_tables(idx, rank, cnt[:, 0])
        xs = _dispatch(pad_start, pad_count, _pos_tiles(pos, TM_DISPATCH), hm)
        ys = _experts(tile_expert, tile_src, tile_rows, xs, w_up[i].astype(BF16), b_up[i][:, None, :],
                      w_down[i].astype(BF16), b_down[i][:, None, :])
        xf = _combine(_pos_tiles(pos, TM_COMBINE), ys, x1, gates_t, p[i].reshape(TOKENS, PLE_DIM),
                      ple_norm[i][None, :], w_ple_gate[i].astype(BF16), w_ple[i].astype(BF16),
                      final_norm[None, :], apply_final_norm=(i == DEPTH - 1))
    return xf.reshape(BATCH, SEQ, D_MODEL)
```

```python
import functools
import math

import numpy as np
import jax
import jax.numpy as jnp
from jax import lax
from jax.experimental import pallas as pl
from jax.experimental.pallas import tpu as pltpu

F32 = jnp.float32
BF16 = jnp.bfloat16

D_MODEL = 1024
BATCH = 8
SEQ = 4096
DEPTH = 2
TOKENS = BATCH * SEQ
GRID_W = 64
PLE_DIM = 256
HEAD_DIM = 64
ROPE_THETA = 10000.0
EPS = 1e-6
HA, Q_LORA, KV_LORA, NOPE_A, ROPE_A, V_A = 6, 256, 128, 64, 32, 64
HB, KVB = 6, 2
HC, KVC, WINDOW = 4, 2, 128
A_COLS = Q_LORA + KV_LORA + ROPE_A
B_COLS = (HB + 2 * KVB) * HEAD_DIM
C_COLS = (HC + 2 * KVC) * HEAD_DIM
N_EXPERTS = 32
TOP_K = 4
D_FF = D_MODEL
SWIGLU_LIMIT = 7.0
SWIGLU_ALPHA = 1.702

LANES = 128
HALF = LANES // 2
BF16_SUBLANES = 16
ROW_TILES = D_MODEL // LANES

VT_ROWS = HEAD_DIM + BF16_SUBLANES

TM_IN = 512
TQ = 512
TQ_SUB = 256
WIN_BLOCKS = 4
TM_OUT = 512
TM_DISPATCH = 1024
TM_EXPERT = 512
TM_COMBINE = 256
ZERO_ROWS = 64
N_EXPERT_TILES = TOKENS * TOP_K // TM_EXPERT + N_EXPERTS
N_SORTED_ROWS = N_EXPERT_TILES * TM_EXPERT
VMEM_LIMIT = 52 * 1024 * 1024

Z_CQ = 0
Z_CKV = Q_LORA
Z_KROPE = Z_CKV + KV_LORA
Z_BQ = Z_KROPE + LANES
Z_BK = Z_BQ + 3 * LANES
Z_BV = Z_BK + LANES
Z_CQS = Z_BV + LANES
Z_CK = Z_CQS + 2 * LANES
Z_CV = Z_CK + LANES
Z_COLS = Z_CV + LANES

NEG_BIG = -1e30
LOG2E = math.log2(math.e)


def _rms(x, g):
    return x * lax.rsqrt(jnp.mean(x * x, axis=-1, keepdims=True) + EPS) * g


def _lane_is_low(shape):
    return lax.broadcasted_iota(jnp.int32, shape, len(shape) - 1) < HALF


def _rms_per_half(xs, g):
    low = _lane_is_low(xs.shape)
    x2 = xs * xs
    s_lo = jnp.sum(jnp.where(low, x2, 0.0), axis=-1, keepdims=True)
    s_hi = jnp.sum(jnp.where(low, 0.0, x2), axis=-1, keepdims=True)
    ms = jnp.where(low, s_lo, s_hi) * (1.0 / HEAD_DIM)
    return xs * lax.rsqrt(ms + EPS) * g


def _token_tile_spec(tm, index):
    return pl.BlockSpec((tm * ROW_TILES, LANES), lambda i, *prefetch: (index(i, *prefetch), 0))


def _token_tile(ref, row):
    return ref.at[pl.ds(pl.multiple_of(row * ROW_TILES, ROW_TILES), ROW_TILES)]


def _token_tiles(ref, n_rows):
    return ref.at[pl.ds(0, n_rows * ROW_TILES)]


def _store_token_tiles(ref, x):
    tm = x.shape[0]
    for c in range(ROW_TILES):
        ref[pl.ds(c, tm, stride=ROW_TILES), :] = x[:, c * LANES:(c + 1) * LANES]


def _load_token_tiles(ref):
    tm = ref.shape[0] // ROW_TILES
    return jnp.concatenate([ref[pl.ds(c, tm, stride=ROW_TILES), :] for c in range(ROW_TILES)], axis=1)


def _in_proj_kernel(x_ref, g_ref, win_ref, gq_ref, wq_ref, gkv_ref, wkv_ref, gbq_ref, gbk_ref, tab_ref,
                    qa_ref, ka_ref, va_ref, qb_ref, kb_ref, vb_ref, qc_ref, kc_ref, vc_ref):
    h = _rms(x_ref[0], g_ref[...])
    z = jnp.dot(h.astype(BF16), win_ref[...], preferred_element_type=F32)
    tm = z.shape[0]
    ones = jnp.ones((tm, LANES), BF16)
    ones_t = jnp.ones((VT_ROWS - HEAD_DIM, tm), BF16)
    low = _lane_is_low((tm, LANES))
    top = lax.broadcasted_iota(jnp.int32, (LANES, tm), 0) < HALF

    def rope(xs, table, shift):
        base = table * 3 * LANES
        c = tab_ref[:, base:base + LANES]
        s_up = tab_ref[:, base + LANES:base + 2 * LANES]
        s_dn = tab_ref[:, base + 2 * LANES:base + 3 * LANES]
        return xs * c + pltpu.roll(xs, LANES - shift, 1) * s_up + pltpu.roll(xs, shift, 1) * s_dn

    c_q = _rms(z[:, Z_CQ:Z_CQ + Q_LORA], gq_ref[...])
    q = jnp.dot(c_q.astype(BF16), wq_ref[...], preferred_element_type=F32)
    c_kv = _rms(z[:, Z_CKV:Z_CKV + KV_LORA], gkv_ref[...])
    kv = jnp.dot(c_kv.astype(BF16), wkv_ref[...], preferred_element_type=F32)
    k_rope = rope(z[:, Z_KROPE:Z_KROPE + LANES], 0, ROPE_A // 2)
    for hd in range(HA):
        qa_ref[0, hd] = rope(q[:, hd * LANES:(hd + 1) * LANES], 0, ROPE_A // 2).T.astype(BF16)
        ka_ref[0, hd] = (kv[:, hd * LANES:(hd + 1) * LANES] + k_rope).astype(BF16)
    for j in range(HA // 2):
        vt = kv[:, (HA + j) * LANES:(HA + j + 1) * LANES].T.astype(BF16)
        for half in range(2):
            va_ref[0, 2 * j + half, 0:HEAD_DIM, :] = vt[half * HALF:(half + 1) * HALF]
            va_ref[0, 2 * j + half, HEAD_DIM:VT_ROWS, :] = ones_t

    for j in range(HB // 2):
        s = rope(_rms_per_half(z[:, Z_BQ + j * LANES:Z_BQ + (j + 1) * LANES], gbq_ref[...]), 1, HEAD_DIM // 4)
        st = s.T
        qb_ref[0, j] = jnp.where(top, st, 0.0).astype(BF16)
        qb_ref[0, HB // 2 + j] = jnp.where(top, 0.0, st).astype(BF16)
    kb_ref[0, 0] = rope(_rms_per_half(z[:, Z_BK:Z_BK + LANES], gbk_ref[...]), 1, HEAD_DIM // 4).astype(BF16)
    vt = z[:, Z_BV:Z_BV + LANES].T.astype(BF16)
    for g in range(KVB):
        vb_ref[0, g, 0:HEAD_DIM, :] = vt[g * HALF:(g + 1) * HALF]
        vb_ref[0, g, HEAD_DIM:VT_ROWS, :] = ones_t

    for j in range(HC // 2):
        s = rope(z[:, Z_CQS + j * LANES:Z_CQS + (j + 1) * LANES], 2, HEAD_DIM // 2)
        qc_ref[0, j] = jnp.where(low, s, 0.0).astype(BF16)
        qc_ref[0, HC // 2 + j] = jnp.where(low, 0.0, s).astype(BF16)
    kc_ref[0, 0] = rope(z[:, Z_CK:Z_CK + LANES], 2, HEAD_DIM // 2).astype(BF16)
    vc_ref[0, 0, :, 0:LANES] = z[:, Z_CV:Z_CV + LANES].astype(BF16)
    vc_ref[0, 0, :, LANES:2 * LANES] = ones


def _in_proj(x3, g, win, gq, wq, gkv, wkv, gbq, gbk, tabs):
    nst = SEQ // TM_IN
    const2 = lambda b, s: (0, 0)
    head_out = lambda n, w: (jax.ShapeDtypeStruct((BATCH, n, SEQ, w), BF16),
                             pl.BlockSpec((1, n, TM_IN, w), lambda b, s: (b, 0, s, 0)))
    head_out_t = lambda n, r: (jax.ShapeDtypeStruct((BATCH, n, r, SEQ), BF16),
                               pl.BlockSpec((1, n, r, TM_IN), lambda b, s: (b, 0, 0, s)))
    outs = [head_out_t(HA, LANES), head_out(HA, LANES), head_out_t(HA, VT_ROWS),
            head_out_t(HB, LANES), head_out(1, LANES), head_out_t(KVB, VT_ROWS),
            head_out(HC, LANES), head_out(1, LANES), head_out(1, 2 * LANES)]
    return pl.pallas_call(
        _in_proj_kernel,
        grid=(BATCH, nst),
        in_specs=[
            pl.BlockSpec((1, TM_IN, D_MODEL), lambda b, s: (b, s, 0)),
            pl.BlockSpec((1, D_MODEL), const2),
            pl.BlockSpec((D_MODEL, Z_COLS), const2),
            pl.BlockSpec((1, Q_LORA), const2),
            pl.BlockSpec((Q_LORA, HA * LANES), const2),
            pl.BlockSpec((1, KV_LORA), const2),
            pl.BlockSpec((KV_LORA, HA * LANES + (HA // 2) * LANES), const2),
            pl.BlockSpec((1, LANES), const2),
            pl.BlockSpec((1, LANES), const2),
            pl.BlockSpec((TM_IN, 9 * LANES), lambda b, s: (s, 0)),
        ],
        out_specs=[o[1] for o in outs],
        out_shape=[o[0] for o in outs],
        compiler_params=pltpu.CompilerParams(
            dimension_semantics=("arbitrary", "arbitrary"), vmem_limit_bytes=VMEM_LIMIT),
        name="in_proj",
    )(x3, g, win, gq, wq, gkv, wkv, gbq, gbk, tabs)


def _attn_kernel(q1_ref, q2_ref, k1_ref, k2_ref, v1_ref, v2_ref, o_ref, *, c):
    n_sub = q1_ref.shape[3] // TQ_SUB

    def scores(q_ref, k_ref, sub):
        qt = q_ref[0, 0, :, sub * TQ_SUB:(sub + 1) * TQ_SUB]
        return jnp.dot(k_ref[0, 0], qt, preferred_element_type=F32)

    def probs(st):
        m = jnp.max(st, axis=0, keepdims=True)
        return jnp.exp2((st - m) * c).astype(BF16)

    def values(v_ref, pt):
        acc = jnp.dot(v_ref[0, 0], pt, preferred_element_type=F32)
        return acc[:HEAD_DIM] / acc[HEAD_DIM:HEAD_DIM + 1]

    st = [scores(q1_ref, k1_ref, 0), scores(q2_ref, k2_ref, 0)]
    for sub in range(n_sub):
        st_next = ([scores(q1_ref, k1_ref, sub + 1), scores(q2_ref, k2_ref, sub + 1)]
                   if sub + 1 < n_sub else None)
        pt = [probs(st[0]), probs(st[1])]
        ot = jnp.concatenate([values(v1_ref, pt[0]), values(v2_ref, pt[1])], axis=0)
        o_ref[0, sub * TQ_SUB:(sub + 1) * TQ_SUB, :] = ot.T.astype(o_ref.dtype)
        st = st_next


def _dense_attention(qt, k, vt, n_heads, k_head, v_head, scale, name):
    kernel = functools.partial(_attn_kernel, c=scale * LOG2E)
    q_spec = lambda off: pl.BlockSpec((1, 1, LANES, TQ), lambda b, p, i: (b, 2 * p + off, 0, i))
    k_spec = lambda off: pl.BlockSpec((1, 1, SEQ, LANES), lambda b, p, i: (b, k_head(2 * p + off), 0, 0))
    v_spec = lambda off: pl.BlockSpec((1, 1, VT_ROWS, SEQ), lambda b, p, i: (b, v_head(2 * p + off), 0, 0))
    return pl.pallas_call(
        kernel,
        grid=(BATCH, n_heads // 2, SEQ // TQ),
        in_specs=[q_spec(0), q_spec(1), k_spec(0), k_spec(1), v_spec(0), v_spec(1)],
        out_specs=pl.BlockSpec((1, TQ, LANES), lambda b, p, i: (b, i, p)),
        out_shape=jax.ShapeDtypeStruct((BATCH, SEQ, n_heads * HEAD_DIM), BF16),
        compiler_params=pltpu.CompilerParams(
            dimension_semantics=("arbitrary", "arbitrary", "arbitrary"), vmem_limit_bytes=VMEM_LIMIT),
        name=name,
    )(qt, qt, k, k, vt, vt)


def _win_kernel(sink_ref, q1_ref, q2_ref, k_ref, v_ref, o_ref, *, scale):
    pair = pl.program_id(1)
    nb = SEQ // WINDOW
    sink1 = sink_ref[pair]
    sink2 = sink_ref[HC // 2 + pair]
    low = _lane_is_low((WINDOW, LANES))
    rel = (lax.broadcasted_iota(jnp.int32, (WINDOW, 3 * WINDOW), 0)
           - lax.broadcasted_iota(jnp.int32, (WINDOW, 3 * WINDOW), 1))

    for blk in range(WIN_BLOCKS):
        n = pl.program_id(2) * WIN_BLOCKS + blk
        start = pl.multiple_of(jnp.clip(n - 1, 0, nb - 3) * WINDOW, WINDOW)
        k = k_ref[0, 0, pl.ds(start, 3 * WINDOW), :]
        v = v_ref[0, 0, pl.ds(start, 3 * WINDOW), :]
        band = jnp.abs(rel + (n * WINDOW - start)) <= WINDOW

        def one_head(q, sink):
            s = lax.dot_general(q, k, (((1,), (1,)), ((), ())), preferred_element_type=F32) * scale
            s = jnp.where(band, s, NEG_BIG)
            m = jnp.maximum(jnp.max(s, axis=-1, keepdims=True), sink)
            p = jnp.exp(s - m)
            acc = jnp.dot(p.astype(BF16), v, preferred_element_type=F32)
            return acc[:, :LANES] / (acc[:, LANES:] + jnp.exp(sink - m))

        rows = slice(blk * WINDOW, (blk + 1) * WINDOW)
        o1 = one_head(q1_ref[0, 0, rows, :], sink1)
        o2 = one_head(q2_ref[0, 0, rows, :], sink2)
        o_ref[0, rows, :] = jnp.where(low, o1, o2).astype(o_ref.dtype)


def _window_attention(sink, q, k, vext):
    kernel = functools.partial(_win_kernel, scale=HEAD_DIM ** -0.5)
    n_pairs = HC // 2
    return pl.pallas_call(
        kernel,
        grid=(BATCH, n_pairs, SEQ // (WINDOW * WIN_BLOCKS)),
        in_specs=[
            pl.BlockSpec(memory_space=pltpu.SMEM),
            pl.BlockSpec((1, 1, WINDOW * WIN_BLOCKS, LANES), lambda b, p, i: (b, p, i, 0)),
            pl.BlockSpec((1, 1, WINDOW * WIN_BLOCKS, LANES), lambda b, p, i: (b, n_pairs + p, i, 0)),
            pl.BlockSpec((1, 1, SEQ, LANES), lambda b, p, i: (b, 0, 0, 0)),
            pl.BlockSpec((1, 1, SEQ, 2 * LANES), lambda b, p, i: (b, 0, 0, 0)),
        ],
        out_specs=pl.BlockSpec((1, WINDOW * WIN_BLOCKS, LANES), lambda b, p, i: (b, i, p)),
        out_shape=jax.ShapeDtypeStruct((BATCH, SEQ, n_pairs * LANES), BF16),
        compiler_params=pltpu.CompilerParams(
            dimension_semantics=("arbitrary", "arbitrary", "arbitrary"), vmem_limit_bytes=VMEM_LIMIT),
        name="win_attn",
    )(sink, q, q, k, vext)


def _out_proj_kernel(x_ref, oa_ref, ob_ref, oc_ref, wa_ref, wb_ref, wc_ref, g_ref, rwt_ref, rb_ref,
                     x1_ref, hm_ref, idx_ref, gate_ref, rank_ref, cnt_ref, run_ref):
    @pl.when(pl.program_id(0) == 0)
    def _():
        run_ref[...] = jnp.zeros_like(run_ref)

    x1 = (x_ref[...]
          + jnp.dot(oa_ref[...], wa_ref[...], preferred_element_type=F32)
          + jnp.dot(ob_ref[...], wb_ref[...], preferred_element_type=F32)
          + jnp.dot(oc_ref[...], wc_ref[...], preferred_element_type=F32))
    x1_ref[...] = x1
    hm = _rms(x1, g_ref[...])
    _store_token_tiles(hm_ref, hm)
    tm = hm.shape[0]

    logits = lax.dot_general(rwt_ref[...], hm, (((1,), (1,)), ((), ())),
                             precision=lax.Precision.HIGHEST, preferred_element_type=F32) + rb_ref[...]
    eidx = lax.broadcasted_iota(jnp.int32, (N_EXPERTS, tm), 0)
    vals, sels, hots = [], [], []
    cur = logits
    for _ in range(TOP_K):
        mx = jnp.max(cur, axis=0, keepdims=True)
        sel = jnp.min(jnp.where(cur == mx, eidx, N_EXPERTS), axis=0, keepdims=True)
        hot = eidx == sel
        vals.append(mx)
        sels.append(sel)
        hots.append(hot)
        cur = jnp.where(hot, -jnp.inf, cur)
    exps = [jnp.exp(v - vals[0]) for v in vals]
    denom = exps[0] + exps[1] + exps[2] + exps[3]
    gates = jnp.concatenate([e / denom for e in exps] + [jnp.zeros((LANES - TOP_K, tm), F32)], axis=0)
    gate_ref[...] = gates.T
    idx_ref[...] = jnp.concatenate(sels, axis=0)

    hot_all = jnp.zeros((N_EXPERTS, tm), F32)
    for hot in hots:
        hot_all = hot_all + jnp.where(hot, 1.0, 0.0)
    row = lax.broadcasted_iota(jnp.int32, (tm, tm), 0)
    col = lax.broadcasted_iota(jnp.int32, (tm, tm), 1)
    before = jnp.where(row < col, 1.0, 0.0).astype(BF16)
    rank_full = jnp.dot(hot_all.astype(BF16), before, preferred_element_type=F32) + run_ref[...]
    ranks = [jnp.sum(jnp.where(hot, rank_full, 0.0), axis=0, keepdims=True) for hot in hots]
    rank_ref[...] = jnp.concatenate(ranks, axis=0).astype(jnp.int32)
    run_ref[...] = run_ref[...] + jnp.sum(hot_all, axis=1, keepdims=True)
    cnt_ref[...] = jnp.broadcast_to(run_ref[...], (N_EXPERTS, LANES)).astype(jnp.int32)


def _out_proj(x, oa, ob, oc, wa, wb, wc, g, rwt, rb):
    tm = TM_OUT
    const = lambda i: (0, 0)
    row_blk = lambda w: pl.BlockSpec((tm, w), lambda i: (i, 0))
    col_blk = pl.BlockSpec((TOP_K, tm), lambda i: (0, i))
    return pl.pallas_call(
        _out_proj_kernel,
        grid=(TOKENS // tm,),
        in_specs=[
            row_blk(D_MODEL), row_blk(oa.shape[1]), row_blk(ob.shape[1]), row_blk(oc.shape[1]),
            pl.BlockSpec(wa.shape, const), pl.BlockSpec(wb.shape, const), pl.BlockSpec(wc.shape, const),
            pl.BlockSpec((1, D_MODEL), const),
            pl.BlockSpec((N_EXPERTS, D_MODEL), const),
            pl.BlockSpec((N_EXPERTS, 1), const),
        ],
        out_specs=[row_blk(D_MODEL), _token_tile_spec(tm, lambda i: i), col_blk, row_blk(LANES), col_blk,
                   pl.BlockSpec((N_EXPERTS, LANES), const)],
        out_shape=[
            jax.ShapeDtypeStruct((TOKENS, D_MODEL), F32),
            jax.ShapeDtypeStruct((TOKENS * ROW_TILES, LANES), F32),
            jax.ShapeDtypeStruct((TOP_K, TOKENS), jnp.int32),
            jax.ShapeDtypeStruct((TOKENS, LANES), F32),
            jax.ShapeDtypeStruct((TOP_K, TOKENS), jnp.int32),
            jax.ShapeDtypeStruct((N_EXPERTS, LANES), jnp.int32),
        ],
        scratch_shapes=[pltpu.VMEM((N_EXPERTS, 1), F32)],
        compiler_params=pltpu.CompilerParams(
            dimension_semantics=("arbitrary",), vmem_limit_bytes=VMEM_LIMIT),
        name="out_proj",
    )(x, oa, ob, oc, wa, wb, wc, g, rwt, rb)


def _issue_row_copies(pos_smem, tm, copy):
    for chunk in range(tm // LANES):
        def issue(j, carry, chunk=chunk):
            for k in range(TOP_K):
                copy(chunk * LANES + j, k, pos_smem[k * (tm // LANES) + chunk, j])
            return carry
        lax.fori_loop(0, LANES, issue, 0, unroll=4)


def _dispatch_kernel(pad_start_ref, pad_count_ref, pos_hbm, hm_ref, xs_ref,
                     pos_smem, zero_ref, pos_sem, row_sems, pad_sem):
    i = pl.program_id(0)
    tm = hm_ref.shape[0] // ROW_TILES
    load = pltpu.make_async_copy(pos_hbm.at[i], pos_smem, pos_sem)
    load.start()

    @pl.when(i == 0)
    def _():
        zero_ref[...] = jnp.zeros_like(zero_ref)
        zero_row = _token_tile(zero_ref, 0)
        for e in range(N_EXPERTS):
            start = pad_start_ref[e]

            def fill(r, carry, start=start):
                pltpu.make_async_copy(zero_row, _token_tile(xs_ref, start + r), pad_sem).start()
                return carry

            def drain(r, carry):
                pltpu.make_async_copy(zero_row, _token_tile(xs_ref, 0), pad_sem).wait()
                return carry

            lax.fori_loop(0, pad_count_ref[e], fill, 0)
            lax.fori_loop(0, pad_count_ref[e], drain, 0)

        tail_start = pad_start_ref[N_EXPERTS]
        chunk = ZERO_ROWS * ROW_TILES

        def fill_tail(r, carry):
            dst = xs_ref.at[pl.ds(pl.multiple_of((tail_start + r * ZERO_ROWS) * ROW_TILES, chunk), chunk)]
            pltpu.make_async_copy(zero_ref, dst, pad_sem).start()
            return carry

        def drain_tail(r, carry):
            pltpu.make_async_copy(zero_ref, xs_ref.at[pl.ds(0, chunk)], pad_sem).wait()
            return carry

        lax.fori_loop(0, pad_count_ref[N_EXPERTS], fill_tail, 0)
        lax.fori_loop(0, pad_count_ref[N_EXPERTS], drain_tail, 0)

    load.wait()

    def copy(t, k, pos):
        pltpu.make_async_copy(_token_tile(hm_ref, t), _token_tile(xs_ref, pos),
                              row_sems.at[k]).start(priority=k % 2)

    _issue_row_copies(pos_smem, tm, copy)
    for k in range(TOP_K):
        pltpu.make_async_copy(hm_ref, _token_tiles(xs_ref, tm), row_sems.at[k]).wait()


def _dispatch(pad_start, pad_count, pos_tiles, hm):
    tm = TM_DISPATCH
    grid_spec = pltpu.PrefetchScalarGridSpec(
        num_scalar_prefetch=2,
        grid=(TOKENS // tm,),
        in_specs=[
            pl.BlockSpec(memory_space=pl.ANY),
            _token_tile_spec(tm, lambda i, ps, pc: i),
        ],
        out_specs=pl.BlockSpec(memory_space=pl.ANY),
        scratch_shapes=[
            pltpu.SMEM((TOP_K * tm // LANES, LANES), jnp.int32),
            pltpu.VMEM((ZERO_ROWS * ROW_TILES, LANES), F32),
            pltpu.SemaphoreType.DMA,
            pltpu.SemaphoreType.DMA((TOP_K,)),
            pltpu.SemaphoreType.DMA,
        ],
    )
    return pl.pallas_call(
        _dispatch_kernel,
        grid_spec=grid_spec,
        out_shape=jax.ShapeDtypeStruct((N_SORTED_ROWS * ROW_TILES, LANES), F32),
        compiler_params=pltpu.CompilerParams(
            dimension_semantics=("arbitrary",), vmem_limit_bytes=VMEM_LIMIT, has_side_effects=True),
        name="dispatch",
    )(pad_start, pad_count, pos_tiles, hm)


def _experts_kernel(te_ref, ts_ref, nr_ref, xs_ref, wup_ref, bup_ref, wdn_ref, bdn_ref, ys_ref,
                    wup_bf_ref, wdn_bf_ref):
    del ts_ref
    i = pl.program_id(0)
    n_rows = nr_ref[i]

    @pl.when(jnp.logical_or(i == 0, te_ref[i] != te_ref[jnp.maximum(i - 1, 0)]))
    def _():
        wup_bf_ref[...] = wup_ref[0, 0].astype(BF16)
        wdn_bf_ref[...] = wdn_ref[0, 0].astype(BF16)

    @pl.when(n_rows > 0)
    def _():
        x = _load_token_tiles(xs_ref).astype(BF16)
        gu = jnp.dot(x, wup_bf_ref[...], preferred_element_type=F32) + bup_ref[0, 0]
        x_glu = jnp.minimum(gu[:, :D_FF], SWIGLU_LIMIT)
        x_lin = jnp.clip(gu[:, D_FF:], -SWIGLU_LIMIT, SWIGLU_LIMIT)
        act = x_glu * jax.nn.sigmoid(SWIGLU_ALPHA * x_glu) * (x_lin + 1.0)
        _store_token_tiles(
            ys_ref, jnp.dot(act.astype(BF16), wdn_bf_ref[...], preferred_element_type=F32) + bdn_ref[0, 0])

    @pl.when(n_rows == 0)
    def _():
        ys_ref[...] = jnp.zeros_like(ys_ref)


def _experts(layer, tile_expert, tile_src, tile_rows, xs, wup, bup, wdn, bdn):
    tm = TM_EXPERT
    per_expert = lambda r, c: pl.BlockSpec((1, 1, r, c), lambda i, te, ts, nr: (layer, te[i], 0, 0))
    grid_spec = pltpu.PrefetchScalarGridSpec(
        num_scalar_prefetch=3,
        grid=(N_EXPERT_TILES,),
        in_specs=[
            _token_tile_spec(tm, lambda i, te, ts, nr: ts[i]),
            per_expert(D_MODEL, 2 * D_FF), per_expert(1, 2 * D_FF),
            per_expert(D_FF, D_MODEL), per_expert(1, D_MODEL),
        ],
        out_specs=_token_tile_spec(tm, lambda i, te, ts, nr: i),
        scratch_shapes=[pltpu.VMEM((D_MODEL, 2 * D_FF), BF16), pltpu.VMEM((D_FF, D_MODEL), BF16)],
    )
    return pl.pallas_call(
        _experts_kernel,
        grid_spec=grid_spec,
        out_shape=jax.ShapeDtypeStruct((N_SORTED_ROWS * ROW_TILES, LANES), F32),
        compiler_params=pltpu.CompilerParams(
            dimension_semantics=("arbitrary",), vmem_limit_bytes=VMEM_LIMIT),
        name="experts",
    )(tile_expert, tile_src, tile_rows, xs, wup, bup, wdn, bdn)


def _combine_kernel(pos_hbm, ys_hbm, x1_ref, gate_ref, p_ref, g_ref, wg_ref, wp_ref, gf_ref,
                    o_ref, pos_smem, rows_ref, pos_sem, row_sems, *, apply_final_norm):
    i = pl.program_id(0)
    tm = x1_ref.shape[0]
    slot = lax.rem(i, 2)

    def start_gather(tile, dst_slot):
        load = pltpu.make_async_copy(pos_hbm.at[tile], pos_smem, pos_sem)
        load.start()
        load.wait()

        def copy(t, k, pos):
            pltpu.make_async_copy(_token_tile(ys_hbm, pos), _token_tile(rows_ref.at[dst_slot, k], t),
                                  row_sems.at[dst_slot, k]).start(priority=k % 2)

        _issue_row_copies(pos_smem, tm, copy)

    @pl.when(i == 0)
    def _():
        start_gather(i, slot)

    @pl.when(i + 1 < pl.num_programs(0))
    def _():
        start_gather(i + 1, 1 - slot)

    for k in range(TOP_K):
        pltpu.make_async_copy(_token_tiles(ys_hbm, tm), rows_ref.at[slot, k], row_sems.at[slot, k]).wait()

    x2 = x1_ref[...]
    for k in range(TOP_K):
        x2 = x2 + _load_token_tiles(rows_ref.at[slot, k]) * gate_ref[:, k:k + 1]
    hp = _rms(x2, g_ref[...]).astype(BF16)
    gate = jax.nn.sigmoid(jnp.dot(hp, wg_ref[...], preferred_element_type=F32))
    pe = jnp.dot(p_ref[...].astype(BF16), wp_ref[...], preferred_element_type=F32)
    x3 = x2 + pe * gate
    if apply_final_norm:
        x3 = _rms(x3, gf_ref[...])
    o_ref[...] = x3


def _combine(pos_tiles, ys, x1, gates_t, p, g, wg, wp, gf, apply_final_norm):
    tm = TM_COMBINE
    const = lambda i: (0, 0)
    row_blk = lambda w: pl.BlockSpec((tm, w), lambda i: (i, 0))
    kernel = functools.partial(_combine_kernel, apply_final_norm=apply_final_norm)
    return pl.pallas_call(
        kernel,
        grid=(TOKENS // tm,),
        in_specs=[
            pl.BlockSpec(memory_space=pl.ANY),
            pl.BlockSpec(memory_space=pl.ANY),
            row_blk(D_MODEL), row_blk(LANES), row_blk(PLE_DIM),
            pl.BlockSpec((1, D_MODEL), const),
            pl.BlockSpec((D_MODEL, D_MODEL), const),
            pl.BlockSpec((PLE_DIM, D_MODEL), const),
            pl.BlockSpec((1, D_MODEL), const),
        ],
        out_specs=row_blk(D_MODEL),
        out_shape=jax.ShapeDtypeStruct((TOKENS, D_MODEL), F32),
        scratch_shapes=[
            pltpu.SMEM((TOP_K * tm // LANES, LANES), jnp.int32),
            pltpu.VMEM((2, TOP_K, tm * ROW_TILES, LANES), F32),
            pltpu.SemaphoreType.DMA,
            pltpu.SemaphoreType.DMA((2, TOP_K)),
        ],
        compiler_params=pltpu.CompilerParams(
            dimension_semantics=("arbitrary",), vmem_limit_bytes=VMEM_LIMIT),
        name="combine",
    )(pos_tiles, ys, x1, gates_t, p, g, wg, wp, gf)


def _in_proj_columns():
    src = np.full((Z_COLS,), -1, np.int64)
    src[Z_CQ:Z_CQ + Q_LORA] = np.arange(Q_LORA)
    src[Z_CKV:Z_CKV + KV_LORA] = Q_LORA + np.arange(KV_LORA)
    src[Z_KROPE + NOPE_A:Z_KROPE + NOPE_A + ROPE_A] = Q_LORA + KV_LORA + np.arange(ROPE_A)
    b0 = A_COLS
    for j in range(HB // 2):
        src[Z_BQ + j * LANES:Z_BQ + j * LANES + HALF] = b0 + j * HEAD_DIM + np.arange(HEAD_DIM)
        src[Z_BQ + j * LANES + HALF:Z_BQ + (j + 1) * LANES] = b0 + (HB // 2 + j) * HEAD_DIM + np.arange(HEAD_DIM)
    src[Z_BK:Z_BK + LANES] = b0 + HB * HEAD_DIM + np.arange(LANES)
    src[Z_BV:Z_BV + LANES] = b0 + (HB + KVB) * HEAD_DIM + np.arange(LANES)
    c0 = A_COLS + B_COLS
    for j in range(HC // 2):
        src[Z_CQS + j * LANES:Z_CQS + j * LANES + HALF] = c0 + j * HEAD_DIM + np.arange(HEAD_DIM)
        src[Z_CQS + j * LANES + HALF:Z_CQS + (j + 1) * LANES] = c0 + (HC // 2 + j) * HEAD_DIM + np.arange(HEAD_DIM)
    src[Z_CK:Z_CK + LANES] = c0 + HC * HEAD_DIM + np.arange(LANES)
    src[Z_CV:Z_CV + LANES] = c0 + (HC + KVC) * HEAD_DIM + np.arange(LANES)
    return src


def _wq_columns():
    src = np.full((HA * LANES,), -1, np.int64)
    dq = NOPE_A + ROPE_A
    for h in range(HA):
        src[h * LANES:h * LANES + dq] = h * dq + np.arange(dq)
    return src


def _wkv_columns():
    src = np.full((HA * LANES + (HA // 2) * LANES,), -1, np.int64)
    dkv = NOPE_A + V_A
    for h in range(HA):
        src[h * LANES:h * LANES + NOPE_A] = h * dkv + np.arange(NOPE_A)
        v0 = HA * LANES + (h // 2) * LANES + (h % 2) * HALF
        src[v0:v0 + V_A] = h * dkv + NOPE_A + np.arange(V_A)
    return src


def _paired_rows(base, n_heads):
    rows = []
    for j in range(n_heads // 2):
        rows.append(base + j * HEAD_DIM + np.arange(HEAD_DIM))
        rows.append(base + (n_heads // 2 + j) * HEAD_DIM + np.arange(HEAD_DIM))
    return np.concatenate(rows)


def _take_cols(w, src):
    cols = jnp.take(w, jnp.asarray(np.maximum(src, 0)), axis=1)
    return jnp.where(jnp.asarray(src >= 0)[None, :], cols, 0.0)


def _rope_tables():
    def cos_sin(pos, dim):
        inv = 1.0 / (ROPE_THETA ** (jnp.arange(0, dim, 2, dtype=F32) / dim))
        ang = pos.astype(F32)[:, None] * inv[None, :]
        return jnp.cos(ang), jnp.sin(ang)

    pos = jnp.arange(SEQ, dtype=jnp.int32)
    rows = pos // GRID_W
    cols = pos % GRID_W
    zeros = lambda w: jnp.zeros((SEQ, w), F32)
    ones = lambda w: jnp.ones((SEQ, w), F32)

    cos_a, sin_a = cos_sin(pos, ROPE_A)
    c_a = jnp.concatenate([ones(NOPE_A), cos_a, cos_a, ones(32)], axis=1)
    up_a = jnp.concatenate([zeros(NOPE_A), -sin_a, zeros(16), zeros(32)], axis=1)
    dn_a = jnp.concatenate([zeros(NOPE_A), zeros(16), sin_a, zeros(32)], axis=1)

    cos_r, sin_r = cos_sin(rows, HEAD_DIM // 2)
    cos_w, sin_w = cos_sin(cols, HEAD_DIM // 2)
    z16 = zeros(16)
    c_b = jnp.concatenate([cos_r, cos_r, cos_w, cos_w] * 2, axis=1)
    up_b = jnp.concatenate([-sin_r, z16, -sin_w, z16] * 2, axis=1)
    dn_b = jnp.concatenate([z16, sin_r, z16, sin_w] * 2, axis=1)

    cos_c, sin_c = cos_sin(pos, HEAD_DIM)
    z32 = zeros(32)
    c_c = jnp.concatenate([cos_c, cos_c] * 2, axis=1)
    up_c = jnp.concatenate([-sin_c, z32] * 2, axis=1)
    dn_c = jnp.concatenate([z32, sin_c] * 2, axis=1)
    return jnp.concatenate([c_a, up_a, dn_a, c_b, up_b, dn_b, c_c, up_c, dn_c], axis=1)


def _routing_tables(idx, rank, counts):
    tiles_per_expert = (counts + TM_EXPERT - 1) // TM_EXPERT
    tile_end = jnp.cumsum(tiles_per_expert)
    tile_start = tile_end - tiles_per_expert
    row_start = tile_start * TM_EXPERT
    experts = jnp.arange(N_EXPERTS, dtype=jnp.int32)
    pos = jnp.sum(jnp.where(idx[..., None] == experts, row_start, 0), axis=-1) + rank
    n_valid = tile_end[-1]
    tile_ids = jnp.arange(N_EXPERT_TILES, dtype=jnp.int32)
    tile_src = jnp.minimum(tile_ids, n_valid - 1)
    tile_expert = jnp.minimum(jnp.sum(tile_src[:, None] >= tile_end[None, :], axis=-1), N_EXPERTS - 1)
    hot = tile_expert[:, None] == experts
    rows_left = jnp.sum(jnp.where(hot, counts - (tile_src[:, None] - tile_start) * TM_EXPERT, 0), axis=-1)
    tile_rows = jnp.where(tile_ids < n_valid, jnp.clip(rows_left, 0, TM_EXPERT), 0)
    tail_start = n_valid * TM_EXPERT
    pad_start = jnp.concatenate([row_start + counts, tail_start[None]])
    pad_count = jnp.concatenate([tiles_per_expert * TM_EXPERT - counts,
                                 ((N_SORTED_ROWS - tail_start) // ZERO_ROWS)[None]])
    tables = (tile_expert, tile_src, tile_rows, pad_start, pad_count)
    return (pos,) + tuple(t.astype(jnp.int32) for t in tables)


def _pos_tiles(pos, tm):
    return pos.reshape(TOP_K, TOKENS // tm, tm).transpose(1, 0, 2).reshape(TOKENS // tm, TOP_K * tm // LANES, LANES)


def kernel(x, p, attn_norm, w_in, mla_q_norm, mla_wq_up, mla_kv_norm, mla_wkv_up, gqa_q_norm, gqa_k_norm,
           swa_sink, w_out, moe_norm, router_w, router_b, w_up, b_up, w_down, b_down, ple_norm, w_ple,
           w_ple_gate, final_norm):
    tabs = _rope_tables()
    in_cols, wq_cols, wkv_cols = _in_proj_columns(), _wq_columns(), _wkv_columns()
    rows_c = _paired_rows(HA * V_A + HB * HEAD_DIM, HC)

    xf = x.reshape(TOKENS, D_MODEL)
    for i in range(DEPTH):
        win = _take_cols(w_in[i], in_cols).astype(BF16)
        wq = _take_cols(mla_wq_up[i], wq_cols).astype(BF16)
        wkv = _take_cols(mla_wkv_up[i], wkv_cols).astype(BF16)
        gbq = jnp.tile(gqa_q_norm[i], 2)[None, :]
        gbk = jnp.tile(gqa_k_norm[i], 2)[None, :]
        qa, ka, va, qb, kb, vb, qc, kc, vc = _in_proj(
            xf.reshape(BATCH, SEQ, D_MODEL), attn_norm[i][None, :], win, mla_q_norm[i][None, :], wq,
            mla_kv_norm[i][None, :], wkv, gbq, gbk, tabs)

        oa = _dense_attention(qa, ka, va, HA, lambda h: h, lambda h: h, (NOPE_A + ROPE_A) ** -0.5, "attn_a")
        ob = _dense_attention(qb, kb, vb, HB, lambda h: h * 0, lambda h: h // (HB // KVB),
                              HEAD_DIM ** -0.5, "attn_b")
        oc = _window_attention(swa_sink[i], qc, kc, vc)

        wo = w_out[i]
        x1, hm, idx, gates_t, rank, cnt = _out_proj(
            xf, oa.reshape(TOKENS, -1), ob.reshape(TOKENS, -1), oc.reshape(TOKENS, -1),
            wo[:HA * V_A].astype(BF16), wo[HA * V_A:HA * V_A + HB * HEAD_DIM].astype(BF16),
            jnp.take(wo, jnp.asarray(rows_c), axis=0).astype(BF16),
            moe_norm[i][None, :], router_w[i].T, router_b[i][:, None])

        pos, tile_expert, tile_src, tile_rows, pad_start, pad_count = _routing_tables(idx, rank, cnt[:, 0])
        xs = _dispatch(pad_start, pad_count, _pos_tiles(pos, TM_DISPATCH), hm)
        ys = _experts(i, tile_expert, tile_src, tile_rows, xs, w_up, b_up[:, :, None, :],
                      w_down, b_down[:, :, None, :])
        xf = _combine(_pos_tiles(pos, TM_COMBINE), ys, x1, gates_t, p[i].reshape(TOKENS, PLE_DIM),
                      ple_norm[i][None, :], w_ple_gate[i].astype(BF16), w_ple[i].astype(BF16),
                      final_norm[None, :], apply_final_norm=(i == DEPTH - 1))
    return xf.reshape(BATCH, SEQ, D_MODEL)
```

```python
import functools
import math

import numpy as np
import jax
import jax.numpy as jnp
from jax import lax
from jax.experimental import pallas as pl
from jax.experimental.pallas import tpu as pltpu

F32 = jnp.float32
BF16 = jnp.bfloat16

D_MODEL = 1024
BATCH = 8
SEQ = 4096
DEPTH = 2
TOKENS = BATCH * SEQ
GRID_W = 64
PLE_DIM = 256
HEAD_DIM = 64
ROPE_THETA = 10000.0
EPS = 1e-6
HA, Q_LORA, KV_LORA, NOPE_A, ROPE_A, V_A = 6, 256, 128, 64, 32, 64
HB, KVB = 6, 2
HC, KVC, WINDOW = 4, 2, 128
A_COLS = Q_LORA + KV_LORA + ROPE_A
B_COLS = (HB + 2 * KVB) * HEAD_DIM
C_COLS = (HC + 2 * KVC) * HEAD_DIM
N_EXPERTS = 32
TOP_K = 4
D_FF = D_MODEL
SWIGLU_LIMIT = 7.0
SWIGLU_ALPHA = 1.702

LANES = 128
HALF = LANES // 2
BF16_SUBLANES = 16
ROW_TILES = D_MODEL // LANES

VT_ROWS = HEAD_DIM + BF16_SUBLANES

TM_IN = 512
TQ = 512
TQ_SUB = 256
WIN_BLOCKS = 4
TM_OUT = 512
TM_DISPATCH = 1024
TM_EXPERT = 512
TM_COMBINE = 256
ZERO_ROWS = 64
N_EXPERT_TILES = TOKENS * TOP_K // TM_EXPERT + N_EXPERTS
N_SORTED_ROWS = N_EXPERT_TILES * TM_EXPERT
VMEM_LIMIT = 52 * 1024 * 1024

Z_CQ = 0
Z_CKV = Q_LORA
Z_KROPE = Z_CKV + KV_LORA
Z_BQ = Z_KROPE + LANES
Z_BK = Z_BQ + 3 * LANES
Z_BV = Z_BK + LANES
Z_CQS = Z_BV + LANES
Z_CK = Z_CQS + 2 * LANES
Z_CV = Z_CK + LANES
Z_COLS = Z_CV + LANES

NEG_BIG = -1e30
LOG2E = math.log2(math.e)
QSCALE_A = (NOPE_A + ROPE_A) ** -0.5 * LOG2E
QSCALE_B = HEAD_DIM ** -0.5 * LOG2E


def _rms(x, g):
    return x * lax.rsqrt(jnp.mean(x * x, axis=-1, keepdims=True) + EPS) * g


def _lane_is_low(shape):
    return lax.broadcasted_iota(jnp.int32, shape, len(shape) - 1) < HALF


def _rms_per_half(xs, g):
    low = _lane_is_low(xs.shape)
    x2 = xs * xs
    s_lo = jnp.sum(jnp.where(low, x2, 0.0), axis=-1, keepdims=True)
    s_hi = jnp.sum(jnp.where(low, 0.0, x2), axis=-1, keepdims=True)
    ms = jnp.where(low, s_lo, s_hi) * (1.0 / HEAD_DIM)
    return xs * lax.rsqrt(ms + EPS) * g


def _token_tile_spec(tm, index):
    return pl.BlockSpec((tm * ROW_TILES, LANES), lambda i, *prefetch: (index(i, *prefetch), 0))


def _token_tile(ref, row):
    return ref.at[pl.ds(pl.multiple_of(row * ROW_TILES, ROW_TILES), ROW_TILES)]


def _token_tiles(ref, n_rows):
    return ref.at[pl.ds(0, n_rows * ROW_TILES)]


def _store_token_tiles(ref, x):
    tm = x.shape[0]
    for c in range(ROW_TILES):
        ref[pl.ds(c, tm, stride=ROW_TILES), :] = x[:, c * LANES:(c + 1) * LANES]


def _load_token_tiles(ref):
    tm = ref.shape[0] // ROW_TILES
    return jnp.concatenate([ref[pl.ds(c, tm, stride=ROW_TILES), :] for c in range(ROW_TILES)], axis=1)


def _in_proj_kernel(x_ref, g_ref, win_ref, gq_ref, wq_ref, gkv_ref, wkv_ref, gbq_ref, gbk_ref, tab_ref,
                    qa_ref, ka_ref, va_ref, qb_ref, kb_ref, vb_ref, qc_ref, kc_ref, vc_ref):
    h = _rms(x_ref[0], g_ref[...])
    z = jnp.dot(h.astype(BF16), win_ref[...], preferred_element_type=F32)
    tm = z.shape[0]
    ones = jnp.ones((tm, LANES), BF16)
    ones_t = jnp.ones((VT_ROWS - HEAD_DIM, tm), BF16)
    low = _lane_is_low((tm, LANES))
    top = lax.broadcasted_iota(jnp.int32, (LANES, tm), 0) < HALF

    def rope(xs, table, shift):
        base = table * 3 * LANES
        c = tab_ref[:, base:base + LANES]
        s_up = tab_ref[:, base + LANES:base + 2 * LANES]
        s_dn = tab_ref[:, base + 2 * LANES:base + 3 * LANES]
        return xs * c + pltpu.roll(xs, LANES - shift, 1) * s_up + pltpu.roll(xs, shift, 1) * s_dn

    c_q = _rms(z[:, Z_CQ:Z_CQ + Q_LORA], gq_ref[...])
    q = jnp.dot(c_q.astype(BF16), wq_ref[...], preferred_element_type=F32)
    c_kv = _rms(z[:, Z_CKV:Z_CKV + KV_LORA], gkv_ref[...])
    kv = jnp.dot(c_kv.astype(BF16), wkv_ref[...], preferred_element_type=F32)
    k_rope = rope(z[:, Z_KROPE:Z_KROPE + LANES], 0, ROPE_A // 2)
    for hd in range(HA):
        qa_ref[0, hd] = (rope(q[:, hd * LANES:(hd + 1) * LANES], 0, ROPE_A // 2) * QSCALE_A).T.astype(BF16)
        ka_ref[0, hd] = (kv[:, hd * LANES:(hd + 1) * LANES] + k_rope).astype(BF16)
    for j in range(HA // 2):
        vt = kv[:, (HA + j) * LANES:(HA + j + 1) * LANES].T.astype(BF16)
        for half in range(2):
            va_ref[0, 2 * j + half, 0:HEAD_DIM, :] = vt[half * HALF:(half + 1) * HALF]
            va_ref[0, 2 * j + half, HEAD_DIM:VT_ROWS, :] = ones_t

    for j in range(HB // 2):
        s = rope(_rms_per_half(z[:, Z_BQ + j * LANES:Z_BQ + (j + 1) * LANES], gbq_ref[...]), 1, HEAD_DIM // 4)
        st = (s * QSCALE_B).T
        qb_ref[0, j] = jnp.where(top, st, 0.0).astype(BF16)
        qb_ref[0, HB // 2 + j] = jnp.where(top, 0.0, st).astype(BF16)
    kb_ref[0, 0] = rope(_rms_per_half(z[:, Z_BK:Z_BK + LANES], gbk_ref[...]), 1, HEAD_DIM // 4).astype(BF16)
    vt = z[:, Z_BV:Z_BV + LANES].T.astype(BF16)
    for g in range(KVB):
        vb_ref[0, g, 0:HEAD_DIM, :] = vt[g * HALF:(g + 1) * HALF]
        vb_ref[0, g, HEAD_DIM:VT_ROWS, :] = ones_t

    for j in range(HC // 2):
        s = rope(z[:, Z_CQS + j * LANES:Z_CQS + (j + 1) * LANES], 2, HEAD_DIM // 2)
        qc_ref[0, j] = jnp.where(low, s, 0.0).astype(BF16)
        qc_ref[0, HC // 2 + j] = jnp.where(low, 0.0, s).astype(BF16)
    kc_ref[0, 0] = rope(z[:, Z_CK:Z_CK + LANES], 2, HEAD_DIM // 2).astype(BF16)
    vc_ref[0, 0, :, 0:LANES] = z[:, Z_CV:Z_CV + LANES].astype(BF16)
    vc_ref[0, 0, :, LANES:2 * LANES] = ones


def _in_proj(x3, g, win, gq, wq, gkv, wkv, gbq, gbk, tabs):
    nst = SEQ // TM_IN
    const2 = lambda b, s: (0, 0)
    head_out = lambda n, w: (jax.ShapeDtypeStruct((BATCH, n, SEQ, w), BF16),
                             pl.BlockSpec((1, n, TM_IN, w), lambda b, s: (b, 0, s, 0)))
    head_out_t = lambda n, r: (jax.ShapeDtypeStruct((BATCH, n, r, SEQ), BF16),
                               pl.BlockSpec((1, n, r, TM_IN), lambda b, s: (b, 0, 0, s)))
    outs = [head_out_t(HA, LANES), head_out(HA, LANES), head_out_t(HA, VT_ROWS),
            head_out_t(HB, LANES), head_out(1, LANES), head_out_t(KVB, VT_ROWS),
            head_out(HC, LANES), head_out(1, LANES), head_out(1, 2 * LANES)]
    return pl.pallas_call(
        _in_proj_kernel,
        grid=(BATCH, nst),
        in_specs=[
            pl.BlockSpec((1, TM_IN, D_MODEL), lambda b, s: (b, s, 0)),
            pl.BlockSpec((1, D_MODEL), const2),
            pl.BlockSpec((D_MODEL, Z_COLS), const2),
            pl.BlockSpec((1, Q_LORA), const2),
            pl.BlockSpec((Q_LORA, HA * LANES), const2),
            pl.BlockSpec((1, KV_LORA), const2),
            pl.BlockSpec((KV_LORA, HA * LANES + (HA // 2) * LANES), const2),
            pl.BlockSpec((1, LANES), const2),
            pl.BlockSpec((1, LANES), const2),
            pl.BlockSpec((TM_IN, 9 * LANES), lambda b, s: (s, 0)),
        ],
        out_specs=[o[1] for o in outs],
        out_shape=[o[0] for o in outs],
        compiler_params=pltpu.CompilerParams(
            dimension_semantics=("arbitrary", "arbitrary"), vmem_limit_bytes=VMEM_LIMIT),
        name="in_proj",
    )(x3, g, win, gq, wq, gkv, wkv, gbq, gbk, tabs)


def _attn_kernel(q1_ref, q2_ref, k1_ref, k2_ref, v1_ref, v2_ref, o_ref,
                 st1a_ref, st1b_ref, st2a_ref, st2b_ref, pt1a_ref, pt1b_ref, pt2a_ref, pt2b_ref,
                 m1a_ref, m1b_ref, m2a_ref, m2b_ref):
    st_refs = ((st1a_ref, st1b_ref), (st2a_ref, st2b_ref))
    pt_refs = ((pt1a_ref, pt1b_ref), (pt2a_ref, pt2b_ref))
    m_refs = ((m1a_ref, m1b_ref), (m2a_ref, m2b_ref))
    q_refs, k_refs, v_refs = (q1_ref, q2_ref), (k1_ref, k2_ref), (v1_ref, v2_ref)
    n_sub = SEQ // TQ_SUB

    def scores(u, slot):
        cols = pl.ds(pl.multiple_of(u * TQ_SUB, TQ_SUB), TQ_SUB)
        for h in range(2):
            qt = q_refs[h][0, 0, :, cols]
            st = jnp.dot(k_refs[h][0, 0], qt, preferred_element_type=F32)
            st_refs[h][slot][...] = st
            m_refs[h][slot][...] = jnp.max(st, axis=0, keepdims=True)

    def probs(slot):
        for h in range(2):
            pt_refs[h][slot][...] = jnp.exp2((st_refs[h][slot][...] - m_refs[h][slot][...]).astype(BF16))

    def values(u, slot):
        outs = []
        for h in range(2):
            acc = jnp.dot(v_refs[h][0, 0], pt_refs[h][slot][...], preferred_element_type=F32)
            outs.append(acc[:HEAD_DIM] / acc[HEAD_DIM:HEAD_DIM + 1])
        rows = pl.ds(pl.multiple_of(u * TQ_SUB, TQ_SUB), TQ_SUB)
        o_ref[0, rows, :] = jnp.concatenate(outs, axis=0).T.astype(o_ref.dtype)

    scores(0, 0)
    scores(1, 1)
    probs(0)

    def body(g, carry):
        u = 2 * g
        scores(u, 0)
        probs(1)
        values(u - 2, 0)
        scores(u + 1, 1)
        probs(0)
        values(u - 1, 1)
        return carry

    lax.fori_loop(1, n_sub // 2, body, 0)
    probs(1)
    values(n_sub - 2, 0)
    values(n_sub - 1, 1)


def _dense_attention(qt, k, vt, n_heads, k_head, v_head, name):
    kernel = _attn_kernel
    q_spec = lambda off: pl.BlockSpec((1, 1, LANES, SEQ), lambda b, p: (b, 2 * p + off, 0, 0))
    k_spec = lambda off: pl.BlockSpec((1, 1, SEQ, LANES), lambda b, p: (b, k_head(2 * p + off), 0, 0))
    v_spec = lambda off: pl.BlockSpec((1, 1, VT_ROWS, SEQ), lambda b, p: (b, v_head(2 * p + off), 0, 0))
    return pl.pallas_call(
        kernel,
        grid=(BATCH, n_heads // 2),
        in_specs=[q_spec(0), q_spec(1), k_spec(0), k_spec(1), v_spec(0), v_spec(1)],
        out_specs=pl.BlockSpec((1, SEQ, LANES), lambda b, p: (b, 0, p)),
        out_shape=jax.ShapeDtypeStruct((BATCH, SEQ, n_heads * HEAD_DIM), BF16),
        scratch_shapes=([pltpu.VMEM((SEQ, TQ_SUB), F32)] * 4 + [pltpu.VMEM((SEQ, TQ_SUB), BF16)] * 4
                        + [pltpu.VMEM((1, TQ_SUB), F32)] * 4),
        compiler_params=pltpu.CompilerParams(
            dimension_semantics=("arbitrary", "arbitrary"), vmem_limit_bytes=VMEM_LIMIT),
        name=name,
    )(qt, qt, k, k, vt, vt)


def _win_kernel(sink_ref, q1_ref, q2_ref, k_ref, v_ref, o_ref, *, scale):
    pair = pl.program_id(1)
    nb = SEQ // WINDOW
    sink1 = sink_ref[pair]
    sink2 = sink_ref[HC // 2 + pair]
    low = _lane_is_low((WINDOW, LANES))
    rel = (lax.broadcasted_iota(jnp.int32, (WINDOW, 3 * WINDOW), 0)
           - lax.broadcasted_iota(jnp.int32, (WINDOW, 3 * WINDOW), 1))

    for blk in range(WIN_BLOCKS):
        n = pl.program_id(2) * WIN_BLOCKS + blk
        start = pl.multiple_of(jnp.clip(n - 1, 0, nb - 3) * WINDOW, WINDOW)
        k = k_ref[0, 0, pl.ds(start, 3 * WINDOW), :]
        v = v_ref[0, 0, pl.ds(start, 3 * WINDOW), :]
        band = jnp.abs(rel + (n * WINDOW - start)) <= WINDOW

        def one_head(q, sink):
            s = lax.dot_general(q, k, (((1,), (1,)), ((), ())), preferred_element_type=F32) * scale
            s = jnp.where(band, s, NEG_BIG)
            m = jnp.maximum(jnp.max(s, axis=-1, keepdims=True), sink)
            p = jnp.exp(s - m)
            acc = jnp.dot(p.astype(BF16), v, preferred_element_type=F32)
            return acc[:, :LANES] / (acc[:, LANES:] + jnp.exp(sink - m))

        rows = slice(blk * WINDOW, (blk + 1) * WINDOW)
        o1 = one_head(q1_ref[0, 0, rows, :], sink1)
        o2 = one_head(q2_ref[0, 0, rows, :], sink2)
        o_ref[0, rows, :] = jnp.where(low, o1, o2).astype(o_ref.dtype)


def _window_attention(sink, q, k, vext):
    kernel = functools.partial(_win_kernel, scale=HEAD_DIM ** -0.5)
    n_pairs = HC // 2
    return pl.pallas_call(
        kernel,
        grid=(BATCH, n_pairs, SEQ // (WINDOW * WIN_BLOCKS)),
        in_specs=[
            pl.BlockSpec(memory_space=pltpu.SMEM),
            pl.BlockSpec((1, 1, WINDOW * WIN_BLOCKS, LANES), lambda b, p, i: (b, p, i, 0)),
            pl.BlockSpec((1, 1, WINDOW * WIN_BLOCKS, LANES), lambda b, p, i: (b, n_pairs + p, i, 0)),
            pl.BlockSpec((1, 1, SEQ, LANES), lambda b, p, i: (b, 0, 0, 0)),
            pl.BlockSpec((1, 1, SEQ, 2 * LANES), lambda b, p, i: (b, 0, 0, 0)),
        ],
        out_specs=pl.BlockSpec((1, WINDOW * WIN_BLOCKS, LANES), lambda b, p, i: (b, i, p)),
        out_shape=jax.ShapeDtypeStruct((BATCH, SEQ, n_pairs * LANES), BF16),
        compiler_params=pltpu.CompilerParams(
            dimension_semantics=("arbitrary", "arbitrary", "arbitrary"), vmem_limit_bytes=VMEM_LIMIT),
        name="win_attn",
    )(sink, q, q, k, vext)


def _out_proj_kernel(x_ref, oa_ref, ob_ref, oc_ref, wa_ref, wb_ref, wc_ref, g_ref, rwt_ref, rb_ref,
                     x1_ref, hm_ref, idx_ref, gate_ref, rank_ref, cnt_ref, run_ref):
    @pl.when(pl.program_id(0) == 0)
    def _():
        run_ref[...] = jnp.zeros_like(run_ref)

    x1 = (x_ref[...]
          + jnp.dot(oa_ref[...], wa_ref[...], preferred_element_type=F32)
          + jnp.dot(ob_ref[...], wb_ref[...], preferred_element_type=F32)
          + jnp.dot(oc_ref[...], wc_ref[...], preferred_element_type=F32))
    x1_ref[...] = x1
    hm = _rms(x1, g_ref[...])
    _store_token_tiles(hm_ref, hm)
    tm = hm.shape[0]

    logits = lax.dot_general(rwt_ref[...], hm, (((1,), (1,)), ((), ())),
                             precision=lax.Precision.HIGHEST, preferred_element_type=F32) + rb_ref[...]
    eidx = lax.broadcasted_iota(jnp.int32, (N_EXPERTS, tm), 0)
    vals, sels, hots = [], [], []
    cur = logits
    for _ in range(TOP_K):
        mx = jnp.max(cur, axis=0, keepdims=True)
        sel = jnp.min(jnp.where(cur == mx, eidx, N_EXPERTS), axis=0, keepdims=True)
        hot = eidx == sel
        vals.append(mx)
        sels.append(sel)
        hots.append(hot)
        cur = jnp.where(hot, -jnp.inf, cur)
    exps = [jnp.exp(v - vals[0]) for v in vals]
    denom = exps[0] + exps[1] + exps[2] + exps[3]
    gates = jnp.concatenate([e / denom for e in exps] + [jnp.zeros((LANES - TOP_K, tm), F32)], axis=0)
    gate_ref[...] = gates.T
    idx_ref[...] = jnp.concatenate(sels, axis=0)

    hot_all = jnp.zeros((N_EXPERTS, tm), F32)
    for hot in hots:
        hot_all = hot_all + jnp.where(hot, 1.0, 0.0)
    row = lax.broadcasted_iota(jnp.int32, (tm, tm), 0)
    col = lax.broadcasted_iota(jnp.int32, (tm, tm), 1)
    before = jnp.where(row < col, 1.0, 0.0).astype(BF16)
    rank_full = jnp.dot(hot_all.astype(BF16), before, preferred_element_type=F32) + run_ref[...]
    ranks = [jnp.sum(jnp.where(hot, rank_full, 0.0), axis=0, keepdims=True) for hot in hots]
    rank_ref[...] = jnp.concatenate(ranks, axis=0).astype(jnp.int32)
    run_ref[...] = run_ref[...] + jnp.sum(hot_all, axis=1, keepdims=True)
    cnt_ref[...] = jnp.broadcast_to(run_ref[...], (N_EXPERTS, LANES)).astype(jnp.int32)


def _out_proj(x, oa, ob, oc, wa, wb, wc, g, rwt, rb):
    tm = TM_OUT
    const = lambda i: (0, 0)
    row_blk = lambda w: pl.BlockSpec((tm, w), lambda i: (i, 0))
    col_blk = pl.BlockSpec((TOP_K, tm), lambda i: (0, i))
    return pl.pallas_call(
        _out_proj_kernel,
        grid=(TOKENS // tm,),
        in_specs=[
            row_blk(D_MODEL), row_blk(oa.shape[1]), row_blk(ob.shape[1]), row_blk(oc.shape[1]),
            pl.BlockSpec(wa.shape, const), pl.BlockSpec(wb.shape, const), pl.BlockSpec(wc.shape, const),
            pl.BlockSpec((1, D_MODEL), const),
            pl.BlockSpec((N_EXPERTS, D_MODEL), const),
            pl.BlockSpec((N_EXPERTS, 1), const),
        ],
        out_specs=[row_blk(D_MODEL), _token_tile_spec(tm, lambda i: i), col_blk, row_blk(LANES), col_blk,
                   pl.BlockSpec((N_EXPERTS, LANES), const)],
        out_shape=[
            jax.ShapeDtypeStruct((TOKENS, D_MODEL), F32),
            jax.ShapeDtypeStruct((TOKENS * ROW_TILES, LANES), F32),
            jax.ShapeDtypeStruct((TOP_K, TOKENS), jnp.int32),
            jax.ShapeDtypeStruct((TOKENS, LANES), F32),
            jax.ShapeDtypeStruct((TOP_K, TOKENS), jnp.int32),
            jax.ShapeDtypeStruct((N_EXPERTS, LANES), jnp.int32),
        ],
        scratch_shapes=[pltpu.VMEM((N_EXPERTS, 1), F32)],
        compiler_params=pltpu.CompilerParams(
            dimension_semantics=("arbitrary",), vmem_limit_bytes=VMEM_LIMIT),
        name="out_proj",
    )(x, oa, ob, oc, wa, wb, wc, g, rwt, rb)


def _issue_row_copies(pos_smem, tm, copy):
    for chunk in range(tm // LANES):
        def issue(j, carry, chunk=chunk):
            for k in range(TOP_K):
                copy(chunk * LANES + j, k, pos_smem[k * (tm // LANES) + chunk, j])
            return carry
        lax.fori_loop(0, LANES, issue, 0, unroll=4)


def _dispatch_kernel(pad_start_ref, pad_count_ref, pos_hbm, hm_ref, xs_ref,
                     pos_smem, zero_ref, pos_sem, row_sems, pad_sem):
    i = pl.program_id(0)
    tm = hm_ref.shape[0] // ROW_TILES
    load = pltpu.make_async_copy(pos_hbm.at[i], pos_smem, pos_sem)
    load.start()

    @pl.when(i == 0)
    def _():
        zero_ref[...] = jnp.zeros_like(zero_ref)
        zero_row = _token_tile(zero_ref, 0)
        for e in range(N_EXPERTS):
            start = pad_start_ref[e]

            def fill(r, carry, start=start):
                pltpu.make_async_copy(zero_row, _token_tile(xs_ref, start + r), pad_sem).start()
                return carry

            def drain(r, carry):
                pltpu.make_async_copy(zero_row, _token_tile(xs_ref, 0), pad_sem).wait()
                return carry

            lax.fori_loop(0, pad_count_ref[e], fill, 0)
            lax.fori_loop(0, pad_count_ref[e], drain, 0)

        tail_start = pad_start_ref[N_EXPERTS]
        chunk = ZERO_ROWS * ROW_TILES

        def fill_tail(r, carry):
            dst = xs_ref.at[pl.ds(pl.multiple_of((tail_start + r * ZERO_ROWS) * ROW_TILES, chunk), chunk)]
            pltpu.make_async_copy(zero_ref, dst, pad_sem).start()
            return carry

        def drain_tail(r, carry):
            pltpu.make_async_copy(zero_ref, xs_ref.at[pl.ds(0, chunk)], pad_sem).wait()
            return carry

        lax.fori_loop(0, pad_count_ref[N_EXPERTS], fill_tail, 0)
        lax.fori_loop(0, pad_count_ref[N_EXPERTS], drain_tail, 0)

    load.wait()

    def copy(t, k, pos):
        pltpu.make_async_copy(_token_tile(hm_ref, t), _token_tile(xs_ref, pos),
                              row_sems.at[k]).start(priority=k % 2)

    _issue_row_copies(pos_smem, tm, copy)
    for k in range(TOP_K):
        pltpu.make_async_copy(hm_ref, _token_tiles(xs_ref, tm), row_sems.at[k]).wait()


def _dispatch(pad_start, pad_count, pos_tiles, hm):
    tm = TM_DISPATCH
    grid_spec = pltpu.PrefetchScalarGridSpec(
        num_scalar_prefetch=2,
        grid=(TOKENS // tm,),
        in_specs=[
            pl.BlockSpec(memory_space=pl.ANY),
            _token_tile_spec(tm, lambda i, ps, pc: i),
        ],
        out_specs=pl.BlockSpec(memory_space=pl.ANY),
        scratch_shapes=[
            pltpu.SMEM((TOP_K * tm // LANES, LANES), jnp.int32),
            pltpu.VMEM((ZERO_ROWS * ROW_TILES, LANES), F32),
            pltpu.SemaphoreType.DMA,
            pltpu.SemaphoreType.DMA((TOP_K,)),
            pltpu.SemaphoreType.DMA,
        ],
    )
    return pl.pallas_call(
        _dispatch_kernel,
        grid_spec=grid_spec,
        out_shape=jax.ShapeDtypeStruct((N_SORTED_ROWS * ROW_TILES, LANES), F32),
        compiler_params=pltpu.CompilerParams(
            dimension_semantics=("arbitrary",), vmem_limit_bytes=VMEM_LIMIT, has_side_effects=True),
        name="dispatch",
    )(pad_start, pad_count, pos_tiles, hm)


def _experts_kernel(te_ref, ts_ref, nr_ref, xs_ref, wup_ref, bup_ref, wdn_ref, bdn_ref, ys_ref,
                    wup_bf_ref, wdn_bf_ref):
    del ts_ref
    i = pl.program_id(0)
    n_rows = nr_ref[i]

    @pl.when(jnp.logical_or(i == 0, te_ref[i] != te_ref[jnp.maximum(i - 1, 0)]))
    def _():
        wup_bf_ref[...] = wup_ref[0, 0].astype(BF16)
        wdn_bf_ref[...] = wdn_ref[0, 0].astype(BF16)

    @pl.when(n_rows > 0)
    def _():
        x = _load_token_tiles(xs_ref).astype(BF16)
        gu = jnp.dot(x, wup_bf_ref[...], preferred_element_type=F32) + bup_ref[0, 0]
        x_glu = jnp.minimum(gu[:, :D_FF], SWIGLU_LIMIT)
        x_lin = jnp.clip(gu[:, D_FF:], -SWIGLU_LIMIT, SWIGLU_LIMIT)
        act = x_glu * jax.nn.sigmoid(SWIGLU_ALPHA * x_glu) * (x_lin + 1.0)
        _store_token_tiles(
            ys_ref, jnp.dot(act.astype(BF16), wdn_bf_ref[...], preferred_element_type=F32) + bdn_ref[0, 0])

    @pl.when(n_rows == 0)
    def _():
        ys_ref[...] = jnp.zeros_like(ys_ref)


def _experts(layer, tile_expert, tile_src, tile_rows, xs, wup, bup, wdn, bdn):
    tm = TM_EXPERT
    per_expert = lambda r, c: pl.BlockSpec((1, 1, r, c), lambda i, te, ts, nr: (layer, te[i], 0, 0))
    grid_spec = pltpu.PrefetchScalarGridSpec(
        num_scalar_prefetch=3,
        grid=(N_EXPERT_TILES,),
        in_specs=[
            _token_tile_spec(tm, lambda i, te, ts, nr: ts[i]),
            per_expert(D_MODEL, 2 * D_FF), per_expert(1, 2 * D_FF),
            per_expert(D_FF, D_MODEL), per_expert(1, D_MODEL),
        ],
        out_specs=_token_tile_spec(tm, lambda i, te, ts, nr: i),
        scratch_shapes=[pltpu.VMEM((D_MODEL, 2 * D_FF), BF16), pltpu.VMEM((D_FF, D_MODEL), BF16)],
    )
    return pl.pallas_call(
        _experts_kernel,
        grid_spec=grid_spec,
        out_shape=jax.ShapeDtypeStruct((N_SORTED_ROWS * ROW_TILES, LANES), F32),
        compiler_params=pltpu.CompilerParams(
            dimension_semantics=("arbitrary",), vmem_limit_bytes=VMEM_LIMIT),
        name="experts",
    )(tile_expert, tile_src, tile_rows, xs, wup, bup, wdn, bdn)


def _combine_kernel(pos_hbm, ys_hbm, x1_ref, gate_ref, p_ref, g_ref, wg_ref, wp_ref, gf_ref,
                    o_ref, pos_smem, rows_ref, pos_sem, row_sems, *, apply_final_norm):
    i = pl.program_id(0)
    tm = x1_ref.shape[0]
    slot = lax.rem(i, 2)

    def start_gather(tile, dst_slot):
        load = pltpu.make_async_copy(pos_hbm.at[tile], pos_smem, pos_sem)
        load.start()
        load.wait()

        def copy(t, k, pos):
            pltpu.make_async_copy(_token_tile(ys_hbm, pos), _token_tile(rows_ref.at[dst_slot, k], t),
                                  row_sems.at[dst_slot, k]).start(priority=k % 2)

        _issue_row_copies(pos_smem, tm, copy)

    @pl.when(i == 0)
    def _():
        start_gather(i, slot)

    @pl.when(i + 1 < pl.num_programs(0))
    def _():
        start_gather(i + 1, 1 - slot)

    for k in range(TOP_K):
        pltpu.make_async_copy(_token_tiles(ys_hbm, tm), rows_ref.at[slot, k], row_sems.at[slot, k]).wait()

    x2 = x1_ref[...]
    for k in range(TOP_K):
        x2 = x2 + _load_token_tiles(rows_ref.at[slot, k]) * gate_ref[:, k:k + 1]
    hp = _rms(x2, g_ref[...]).astype(BF16)
    gate = jax.nn.sigmoid(jnp.dot(hp, wg_ref[...], preferred_element_type=F32))
    pe = jnp.dot(p_ref[...].astype(BF16), wp_ref[...], preferred_element_type=F32)
    x3 = x2 + pe * gate
    if apply_final_norm:
        x3 = _rms(x3, gf_ref[...])
    o_ref[...] = x3


def _combine(pos_tiles, ys, x1, gates_t, p, g, wg, wp, gf, apply_final_norm):
    tm = TM_COMBINE
    const = lambda i: (0, 0)
    row_blk = lambda w: pl.BlockSpec((tm, w), lambda i: (i, 0))
    kernel = functools.partial(_combine_kernel, apply_final_norm=apply_final_norm)
    return pl.pallas_call(
        kernel,
        grid=(TOKENS // tm,),
        in_specs=[
            pl.BlockSpec(memory_space=pl.ANY),
            pl.BlockSpec(memory_space=pl.ANY),
            row_blk(D_MODEL), row_blk(LANES), row_blk(PLE_DIM),
            pl.BlockSpec((1, D_MODEL), const),
            pl.BlockSpec((D_MODEL, D_MODEL), const),
            pl.BlockSpec((PLE_DIM, D_MODEL), const),
            pl.BlockSpec((1, D_MODEL), const),
        ],
        out_specs=row_blk(D_MODEL),
        out_shape=jax.ShapeDtypeStruct((TOKENS, D_MODEL), F32),
        scratch_shapes=[
            pltpu.SMEM((TOP_K * tm // LANES, LANES), jnp.int32),
            pltpu.VMEM((2, TOP_K, tm * ROW_TILES, LANES), F32),
            pltpu.SemaphoreType.DMA,
            pltpu.SemaphoreType.DMA((2, TOP_K)),
        ],
        compiler_params=pltpu.CompilerParams(
            dimension_semantics=("arbitrary",), vmem_limit_bytes=VMEM_LIMIT),
        name="combine",
    )(pos_tiles, ys, x1, gates_t, p, g, wg, wp, gf)


def _in_proj_columns():
    src = np.full((Z_COLS,), -1, np.int64)
    src[Z_CQ:Z_CQ + Q_LORA] = np.arange(Q_LORA)
    src[Z_CKV:Z_CKV + KV_LORA] = Q_LORA + np.arange(KV_LORA)
    src[Z_KROPE + NOPE_A:Z_KROPE + NOPE_A + ROPE_A] = Q_LORA + KV_LORA + np.arange(ROPE_A)
    b0 = A_COLS
    for j in range(HB // 2):
        src[Z_BQ + j * LANES:Z_BQ + j * LANES + HALF] = b0 + j * HEAD_DIM + np.arange(HEAD_DIM)
        src[Z_BQ + j * LANES + HALF:Z_BQ + (j + 1) * LANES] = b0 + (HB // 2 + j) * HEAD_DIM + np.arange(HEAD_DIM)
    src[Z_BK:Z_BK + LANES] = b0 + HB * HEAD_DIM + np.arange(LANES)
    src[Z_BV:Z_BV + LANES] = b0 + (HB + KVB) * HEAD_DIM + np.arange(LANES)
    c0 = A_COLS + B_COLS
    for j in range(HC // 2):
        src[Z_CQS + j * LANES:Z_CQS + j * LANES + HALF] = c0 + j * HEAD_DIM + np.arange(HEAD_DIM)
        src[Z_CQS + j * LANES + HALF:Z_CQS + (j + 1) * LANES] = c0 + (HC // 2 + j) * HEAD_DIM + np.arange(HEAD_DIM)
    src[Z_CK:Z_CK + LANES] = c0 + HC * HEAD_DIM + np.arange(LANES)
    src[Z_CV:Z_CV + LANES] = c0 + (HC + KVC) * HEAD_DIM + np.arange(LANES)
    return src


def _wq_columns():
    src = np.full((HA * LANES,), -1, np.int64)
    dq = NOPE_A + ROPE_A
    for h in range(HA):
        src[h * LANES:h * LANES + dq] = h * dq + np.arange(dq)
    return src


def _wkv_columns():
    src = np.full((HA * LANES + (HA // 2) * LANES,), -1, np.int64)
    dkv = NOPE_A + V_A
    for h in range(HA):
        src[h * LANES:h * LANES + NOPE_A] = h * dkv + np.arange(NOPE_A)
        v0 = HA * LANES + (h // 2) * LANES + (h % 2) * HALF
        src[v0:v0 + V_A] = h * dkv + NOPE_A + np.arange(V_A)
    return src


def _paired_rows(base, n_heads):
    rows = []
    for j in range(n_heads // 2):
        rows.append(base + j * HEAD_DIM + np.arange(HEAD_DIM))
        rows.append(base + (n_heads // 2 + j) * HEAD_DIM + np.arange(HEAD_DIM))
    return np.concatenate(rows)


def _take_cols(w, src):
    cols = jnp.take(w, jnp.asarray(np.maximum(src, 0)), axis=1)
    return jnp.where(jnp.asarray(src >= 0)[None, :], cols, 0.0)


def _rope_tables():
    def cos_sin(pos, dim):
        inv = 1.0 / (ROPE_THETA ** (jnp.arange(0, dim, 2, dtype=F32) / dim))
        ang = pos.astype(F32)[:, None] * inv[None, :]
        return jnp.cos(ang), jnp.sin(ang)

    pos = jnp.arange(SEQ, dtype=jnp.int32)
    rows = pos // GRID_W
    cols = pos % GRID_W
    zeros = lambda w: jnp.zeros((SEQ, w), F32)
    ones = lambda w: jnp.ones((SEQ, w), F32)

    cos_a, sin_a = cos_sin(pos, ROPE_A)
    c_a = jnp.concatenate([ones(NOPE_A), cos_a, cos_a, ones(32)], axis=1)
    up_a = jnp.concatenate([zeros(NOPE_A), -sin_a, zeros(16), zeros(32)], axis=1)
    dn_a = jnp.concatenate([zeros(NOPE_A), zeros(16), sin_a, zeros(32)], axis=1)

    cos_r, sin_r = cos_sin(rows, HEAD_DIM // 2)
    cos_w, sin_w = cos_sin(cols, HEAD_DIM // 2)
    z16 = zeros(16)
    c_b = jnp.concatenate([cos_r, cos_r, cos_w, cos_w] * 2, axis=1)
    up_b = jnp.concatenate([-sin_r, z16, -sin_w, z16] * 2, axis=1)
    dn_b = jnp.concatenate([z16, sin_r, z16, sin_w] * 2, axis=1)

    cos_c, sin_c = cos_sin(pos, HEAD_DIM)
    z32 = zeros(32)
    c_c = jnp.concatenate([cos_c, cos_c] * 2, axis=1)
    up_c = jnp.concatenate([-sin_c, z32] * 2, axis=1)
    dn_c = jnp.concatenate([z32, sin_c] * 2, axis=1)
    return jnp.concatenate([c_a, up_a, dn_a, c_b, up_b, dn_b, c_c, up_c, dn_c], axis=1)


def _routing_tables(idx, rank, counts):
    tiles_per_expert = (counts + TM_EXPERT - 1) // TM_EXPERT
    tile_end = jnp.cumsum(tiles_per_expert)
    tile_start = tile_end - tiles_per_expert
    row_start = tile_start * TM_EXPERT
    experts = jnp.arange(N_EXPERTS, dtype=jnp.int32)
    pos = jnp.sum(jnp.where(idx[..., None] == experts, row_start, 0), axis=-1) + rank
    n_valid = tile_end[-1]
    tile_ids = jnp.arange(N_EXPERT_TILES, dtype=jnp.int32)
    tile_src = jnp.minimum(tile_ids, n_valid - 1)
    tile_expert = jnp.minimum(jnp.sum(tile_src[:, None] >= tile_end[None, :], axis=-1), N_EXPERTS - 1)
    hot = tile_expert[:, None] == experts
    rows_left = jnp.sum(jnp.where(hot, counts - (tile_src[:, None] - tile_start) * TM_EXPERT, 0), axis=-1)
    tile_rows = jnp.where(tile_ids < n_valid, jnp.clip(rows_left, 0, TM_EXPERT), 0)
    tail_start = n_valid * TM_EXPERT
    pad_start = jnp.concatenate([row_start + counts, tail_start[None]])
    pad_count = jnp.concatenate([tiles_per_expert * TM_EXPERT - counts,
                                 ((N_SORTED_ROWS - tail_start) // ZERO_ROWS)[None]])
    tables = (tile_expert, tile_src, tile_rows, pad_start, pad_count)
    return (pos,) + tuple(t.astype(jnp.int32) for t in tables)


def _pos_tiles(pos, tm):
    return pos.reshape(TOP_K, TOKENS // tm, tm).transpose(1, 0, 2).reshape(TOKENS // tm, TOP_K * tm // LANES, LANES)


def kernel(x, p, attn_norm, w_in, mla_q_norm, mla_wq_up, mla_kv_norm, mla_wkv_up, gqa_q_norm, gqa_k_norm,
           swa_sink, w_out, moe_norm, router_w, router_b, w_up, b_up, w_down, b_down, ple_norm, w_ple,
           w_ple_gate, final_norm):
    tabs = _rope_tables()
    in_cols, wq_cols, wkv_cols = _in_proj_columns(), _wq_columns(), _wkv_columns()
    rows_c = _paired_rows(HA * V_A + HB * HEAD_DIM, HC)

    xf = x.reshape(TOKENS, D_MODEL)
    for i in range(DEPTH):
        win = _take_cols(w_in[i], in_cols).astype(BF16)
        wq = _take_cols(mla_wq_up[i], wq_cols).astype(BF16)
        wkv = _take_cols(mla_wkv_up[i], wkv_cols).astype(BF16)
        gbq = jnp.tile(gqa_q_norm[i], 2)[None, :]
        gbk = jnp.tile(gqa_k_norm[i], 2)[None, :]
        qa, ka, va, qb, kb, vb, qc, kc, vc = _in_proj(
            xf.reshape(BATCH, SEQ, D_MODEL), attn_norm[i][None, :], win, mla_q_norm[i][None, :], wq,
            mla_kv_norm[i][None, :], wkv, gbq, gbk, tabs)

        oa = _dense_attention(qa, ka, va, HA, lambda h: h, lambda h: h, "attn_a")
        ob = _dense_attention(qb, kb, vb, HB, lambda h: h * 0, lambda h: h // (HB // KVB), "attn_b")
        oc = _window_attention(swa_sink[i], qc, kc, vc)

        wo = w_out[i]
        x1, hm, idx, gates_t, rank, cnt = _out_proj(
            xf, oa.reshape(TOKENS, -1), ob.reshape(TOKENS, -1), oc.reshape(TOKENS, -1),
            wo[:HA * V_A].astype(BF16), wo[HA * V_A:HA * V_A + HB * HEAD_DIM].astype(BF16),
            jnp.take(wo, jnp.asarray(rows_c), axis=0).astype(BF16),
            moe_norm[i][None, :], router_w[i].T, router_b[i][:, None])

        pos, tile_expert, tile_src, tile_rows, pad_start, pad_count = _routing_tables(idx, rank, cnt[:, 0])
        xs = _dispatch(pad_start, pad_count, _pos_tiles(pos, TM_DISPATCH), hm)
        ys = _experts(i, tile_expert, tile_src, tile_rows, xs, w_up, b_up[:, :, None, :],
                      w_down, b_down[:, :, None, :])
        xf = _combine(_pos_tiles(pos, TM_COMBINE), ys, x1, gates_t, p[i].reshape(TOKENS, PLE_DIM),
                      ple_norm[i][None, :], w_ple_gate[i].astype(BF16), w_ple[i].astype(BF16),
                      final_norm[None, :], apply_final_norm=(i == DEPTH - 1))
    return xf.reshape(BATCH, SEQ, D_MODEL)
```

```python
import functools
import math

import numpy as np
import jax
import jax.numpy as jnp
from jax import lax
from jax.experimental import pallas as pl
from jax.experimental.pallas import tpu as pltpu

F32 = jnp.float32
BF16 = jnp.bfloat16

D_MODEL = 1024
BATCH = 8
SEQ = 4096
DEPTH = 2
TOKENS = BATCH * SEQ
GRID_W = 64
PLE_DIM = 256
HEAD_DIM = 64
ROPE_THETA = 10000.0
EPS = 1e-6
HA, Q_LORA, KV_LORA, NOPE_A, ROPE_A, V_A = 6, 256, 128, 64, 32, 64
HB, KVB = 6, 2
HC, KVC, WINDOW = 4, 2, 128
A_COLS = Q_LORA + KV_LORA + ROPE_A
B_COLS = (HB + 2 * KVB) * HEAD_DIM
C_COLS = (HC + 2 * KVC) * HEAD_DIM
N_EXPERTS = 32
TOP_K = 4
D_FF = D_MODEL
SWIGLU_LIMIT = 7.0
SWIGLU_ALPHA = 1.702

LANES = 128
HALF = LANES // 2
BF16_SUBLANES = 16
ROW_TILES = D_MODEL // LANES

VT_ROWS = HEAD_DIM + BF16_SUBLANES

TM_IN = 512
TQ = 512
TQ_SUB = 256
WIN_BLOCKS = 4
TM_OUT = 512
TM_DISPATCH = 512
TM_EXPERT = 512
TM_COMBINE = 256
ZERO_ROWS = 64
N_EXPERT_TILES = TOKENS * TOP_K // TM_EXPERT + N_EXPERTS
N_SORTED_ROWS = N_EXPERT_TILES * TM_EXPERT
VMEM_LIMIT = 52 * 1024 * 1024

Z_CQ = 0
Z_CKV = Q_LORA
Z_KROPE = Z_CKV + KV_LORA
Z_BQ = Z_KROPE + LANES
Z_BK = Z_BQ + 3 * LANES
Z_BV = Z_BK + LANES
Z_CQS = Z_BV + LANES
Z_CK = Z_CQS + 2 * LANES
Z_CV = Z_CK + LANES
Z_COLS = Z_CV + LANES

NEG_BIG = -1e30
LOG2E = math.log2(math.e)
QSCALE_A = (NOPE_A + ROPE_A) ** -0.5 * LOG2E
QSCALE_B = HEAD_DIM ** -0.5 * LOG2E


def _rms(x, g):
    return x * lax.rsqrt(jnp.mean(x * x, axis=-1, keepdims=True) + EPS) * g


def _lane_is_low(shape):
    return lax.broadcasted_iota(jnp.int32, shape, len(shape) - 1) < HALF


def _rms_per_half(xs, g):
    low = _lane_is_low(xs.shape)
    x2 = xs * xs
    s_lo = jnp.sum(jnp.where(low, x2, 0.0), axis=-1, keepdims=True)
    s_hi = jnp.sum(jnp.where(low, 0.0, x2), axis=-1, keepdims=True)
    ms = jnp.where(low, s_lo, s_hi) * (1.0 / HEAD_DIM)
    return xs * lax.rsqrt(ms + EPS) * g


def _token_tile_spec(tm, index):
    return pl.BlockSpec((tm * ROW_TILES, LANES), lambda i, *prefetch: (index(i, *prefetch), 0))


def _token_tile(ref, row):
    return ref.at[pl.ds(pl.multiple_of(row * ROW_TILES, ROW_TILES), ROW_TILES)]


def _token_tiles(ref, n_rows):
    return ref.at[pl.ds(0, n_rows * ROW_TILES)]


def _store_token_tiles(ref, x):
    tm = x.shape[0]
    for c in range(ROW_TILES):
        ref[pl.ds(c, tm, stride=ROW_TILES), :] = x[:, c * LANES:(c + 1) * LANES]


def _load_token_tiles(ref):
    tm = ref.shape[0] // ROW_TILES
    return jnp.concatenate([ref[pl.ds(c, tm, stride=ROW_TILES), :] for c in range(ROW_TILES)], axis=1)


def _in_proj_kernel(x_ref, g_ref, win_ref, gq_ref, wq_ref, gkv_ref, wkv_ref, gbq_ref, gbk_ref, tab_ref,
                    qa_ref, ka_ref, va_ref, qb_ref, kb_ref, vb_ref, qc_ref, kc_ref, vc_ref):
    h = _rms(x_ref[0], g_ref[...])
    z = jnp.dot(h.astype(BF16), win_ref[...], preferred_element_type=F32)
    tm = z.shape[0]
    ones = jnp.ones((tm, LANES), BF16)
    ones_t = jnp.ones((VT_ROWS - HEAD_DIM, tm), BF16)
    low = _lane_is_low((tm, LANES))
    top = lax.broadcasted_iota(jnp.int32, (LANES, tm), 0) < HALF

    def rope(xs, table, shift):
        base = table * 3 * LANES
        c = tab_ref[:, base:base + LANES]
        s_up = tab_ref[:, base + LANES:base + 2 * LANES]
        s_dn = tab_ref[:, base + 2 * LANES:base + 3 * LANES]
        return xs * c + pltpu.roll(xs, LANES - shift, 1) * s_up + pltpu.roll(xs, shift, 1) * s_dn

    c_q = _rms(z[:, Z_CQ:Z_CQ + Q_LORA], gq_ref[...])
    q = jnp.dot(c_q.astype(BF16), wq_ref[...], preferred_element_type=F32)
    c_kv = _rms(z[:, Z_CKV:Z_CKV + KV_LORA], gkv_ref[...])
    kv = jnp.dot(c_kv.astype(BF16), wkv_ref[...], preferred_element_type=F32)
    k_rope = rope(z[:, Z_KROPE:Z_KROPE + LANES], 0, ROPE_A // 2)
    for hd in range(HA):
        qa_ref[0, hd] = (rope(q[:, hd * LANES:(hd + 1) * LANES], 0, ROPE_A // 2) * QSCALE_A).T.astype(BF16)
        ka_ref[0, hd] = (kv[:, hd * LANES:(hd + 1) * LANES] + k_rope).astype(BF16)
    for j in range(HA // 2):
        vt = kv[:, (HA + j) * LANES:(HA + j + 1) * LANES].T.astype(BF16)
        for half in range(2):
            va_ref[0, 2 * j + half, 0:HEAD_DIM, :] = vt[half * HALF:(half + 1) * HALF]
            va_ref[0, 2 * j + half, HEAD_DIM:VT_ROWS, :] = ones_t

    for j in range(HB // 2):
        s = rope(_rms_per_half(z[:, Z_BQ + j * LANES:Z_BQ + (j + 1) * LANES], gbq_ref[...]), 1, HEAD_DIM // 4)
        st = (s * QSCALE_B).T
        qb_ref[0, j] = jnp.where(top, st, 0.0).astype(BF16)
        qb_ref[0, HB // 2 + j] = jnp.where(top, 0.0, st).astype(BF16)
    kb_ref[0, 0] = rope(_rms_per_half(z[:, Z_BK:Z_BK + LANES], gbk_ref[...]), 1, HEAD_DIM // 4).astype(BF16)
    vt = z[:, Z_BV:Z_BV + LANES].T.astype(BF16)
    for g in range(KVB):
        vb_ref[0, g, 0:HEAD_DIM, :] = vt[g * HALF:(g + 1) * HALF]
        vb_ref[0, g, HEAD_DIM:VT_ROWS, :] = ones_t

    for j in range(HC // 2):
        s = rope(z[:, Z_CQS + j * LANES:Z_CQS + (j + 1) * LANES], 2, HEAD_DIM // 2)
        qc_ref[0, j] = jnp.where(low, s, 0.0).astype(BF16)
        qc_ref[0, HC // 2 + j] = jnp.where(low, 0.0, s).astype(BF16)
    kc_ref[0, 0] = rope(z[:, Z_CK:Z_CK + LANES], 2, HEAD_DIM // 2).astype(BF16)
    vc_ref[0, 0, :, 0:LANES] = z[:, Z_CV:Z_CV + LANES].astype(BF16)
    vc_ref[0, 0, :, LANES:2 * LANES] = ones


def _in_proj(x3, g, win, gq, wq, gkv, wkv, gbq, gbk, tabs):
    nst = SEQ // TM_IN
    const2 = lambda b, s: (0, 0)
    head_out = lambda n, w: (jax.ShapeDtypeStruct((BATCH, n, SEQ, w), BF16),
                             pl.BlockSpec((1, n, TM_IN, w), lambda b, s: (b, 0, s, 0)))
    head_out_t = lambda n, r: (jax.ShapeDtypeStruct((BATCH, n, r, SEQ), BF16),
                               pl.BlockSpec((1, n, r, TM_IN), lambda b, s: (b, 0, 0, s)))
    outs = [head_out_t(HA, LANES), head_out(HA, LANES), head_out_t(HA, VT_ROWS),
            head_out_t(HB, LANES), head_out(1, LANES), head_out_t(KVB, VT_ROWS),
            head_out(HC, LANES), head_out(1, LANES), head_out(1, 2 * LANES)]
    return pl.pallas_call(
        _in_proj_kernel,
        grid=(BATCH, nst),
        in_specs=[
            pl.BlockSpec((1, TM_IN, D_MODEL), lambda b, s: (b, s, 0)),
            pl.BlockSpec((1, D_MODEL), const2),
            pl.BlockSpec((D_MODEL, Z_COLS), const2),
            pl.BlockSpec((1, Q_LORA), const2),
            pl.BlockSpec((Q_LORA, HA * LANES), const2),
            pl.BlockSpec((1, KV_LORA), const2),
            pl.BlockSpec((KV_LORA, HA * LANES + (HA // 2) * LANES), const2),
            pl.BlockSpec((1, LANES), const2),
            pl.BlockSpec((1, LANES), const2),
            pl.BlockSpec((TM_IN, 9 * LANES), lambda b, s: (s, 0)),
        ],
        out_specs=[o[1] for o in outs],
        out_shape=[o[0] for o in outs],
        compiler_params=pltpu.CompilerParams(
            dimension_semantics=("arbitrary", "arbitrary"), vmem_limit_bytes=VMEM_LIMIT),
        name="in_proj",
    )(x3, g, win, gq, wq, gkv, wkv, gbq, gbk, tabs)


def _attn_kernel(q1_ref, q2_ref, k1_ref, k2_ref, v1_ref, v2_ref, o_ref,
                 st1a_ref, st1b_ref, st2a_ref, st2b_ref, pt1a_ref, pt1b_ref, pt2a_ref, pt2b_ref,
                 m1a_ref, m1b_ref, m2a_ref, m2b_ref):
    st_refs = ((st1a_ref, st1b_ref), (st2a_ref, st2b_ref))
    pt_refs = ((pt1a_ref, pt1b_ref), (pt2a_ref, pt2b_ref))
    m_refs = ((m1a_ref, m1b_ref), (m2a_ref, m2b_ref))
    q_refs, k_refs, v_refs = (q1_ref, q2_ref), (k1_ref, k2_ref), (v1_ref, v2_ref)
    n_sub = SEQ // TQ_SUB

    def scores(u, slot):
        cols = pl.ds(pl.multiple_of(u * TQ_SUB, TQ_SUB), TQ_SUB)
        for h in range(2):
            qt = q_refs[h][0, 0, :, cols]
            st = jnp.dot(k_refs[h][0, 0], qt, preferred_element_type=F32)
            st_refs[h][slot][...] = st
            m_refs[h][slot][...] = jnp.max(st, axis=0, keepdims=True)

    def probs(slot):
        for h in range(2):
            pt_refs[h][slot][...] = jnp.exp2((st_refs[h][slot][...] - m_refs[h][slot][...]).astype(BF16))

    def values(u, slot):
        outs = []
        for h in range(2):
            acc = jnp.dot(v_refs[h][0, 0], pt_refs[h][slot][...], preferred_element_type=F32)
            outs.append(acc[:HEAD_DIM] / acc[HEAD_DIM:HEAD_DIM + 1])
        rows = pl.ds(pl.multiple_of(u * TQ_SUB, TQ_SUB), TQ_SUB)
        o_ref[0, rows, :] = jnp.concatenate(outs, axis=0).T.astype(o_ref.dtype)

    scores(0, 0)
    scores(1, 1)
    probs(0)

    def body(g, carry):
        u = 2 * g
        scores(u, 0)
        probs(1)
        values(u - 2, 0)
        scores(u + 1, 1)
        probs(0)
        values(u - 1, 1)
        return carry

    lax.fori_loop(1, n_sub // 2, body, 0)
    probs(1)
    values(n_sub - 2, 0)
    values(n_sub - 1, 1)


def _dense_attention(qt, k, vt, n_heads, k_head, v_head, name):
    kernel = _attn_kernel
    q_spec = lambda off: pl.BlockSpec((1, 1, LANES, SEQ), lambda b, p: (b, 2 * p + off, 0, 0))
    k_spec = lambda off: pl.BlockSpec((1, 1, SEQ, LANES), lambda b, p: (b, k_head(2 * p + off), 0, 0))
    v_spec = lambda off: pl.BlockSpec((1, 1, VT_ROWS, SEQ), lambda b, p: (b, v_head(2 * p + off), 0, 0))
    return pl.pallas_call(
        kernel,
        grid=(BATCH, n_heads // 2),
        in_specs=[q_spec(0), q_spec(1), k_spec(0), k_spec(1), v_spec(0), v_spec(1)],
        out_specs=pl.BlockSpec((1, SEQ, LANES), lambda b, p: (b, 0, p)),
        out_shape=jax.ShapeDtypeStruct((BATCH, SEQ, n_heads * HEAD_DIM), BF16),
        scratch_shapes=([pltpu.VMEM((SEQ, TQ_SUB), F32)] * 4 + [pltpu.VMEM((SEQ, TQ_SUB), BF16)] * 4
                        + [pltpu.VMEM((1, TQ_SUB), F32)] * 4),
        compiler_params=pltpu.CompilerParams(
            dimension_semantics=("arbitrary", "arbitrary"), vmem_limit_bytes=VMEM_LIMIT),
        name=name,
    )(qt, qt, k, k, vt, vt)


def _win_kernel(sink_ref, q1_ref, q2_ref, k_ref, v_ref, o_ref, *, scale):
    pair = pl.program_id(1)
    nb = SEQ // WINDOW
    sink1 = sink_ref[pair]
    sink2 = sink_ref[HC // 2 + pair]
    low = _lane_is_low((WINDOW, LANES))
    rel = (lax.broadcasted_iota(jnp.int32, (WINDOW, 3 * WINDOW), 0)
           - lax.broadcasted_iota(jnp.int32, (WINDOW, 3 * WINDOW), 1))

    for blk in range(WIN_BLOCKS):
        n = pl.program_id(2) * WIN_BLOCKS + blk
        start = pl.multiple_of(jnp.clip(n - 1, 0, nb - 3) * WINDOW, WINDOW)
        k = k_ref[0, 0, pl.ds(start, 3 * WINDOW), :]
        v = v_ref[0, 0, pl.ds(start, 3 * WINDOW), :]
        band = jnp.abs(rel + (n * WINDOW - start)) <= WINDOW

        def one_head(q, sink):
            s = lax.dot_general(q, k, (((1,), (1,)), ((), ())), preferred_element_type=F32) * scale
            s = jnp.where(band, s, NEG_BIG)
            m = jnp.maximum(jnp.max(s, axis=-1, keepdims=True), sink)
            p = jnp.exp(s - m)
            acc = jnp.dot(p.astype(BF16), v, preferred_element_type=F32)
            return acc[:, :LANES] / (acc[:, LANES:] + jnp.exp(sink - m))

        rows = slice(blk * WINDOW, (blk + 1) * WINDOW)
        o1 = one_head(q1_ref[0, 0, rows, :], sink1)
        o2 = one_head(q2_ref[0, 0, rows, :], sink2)
        o_ref[0, rows, :] = jnp.where(low, o1, o2).astype(o_ref.dtype)


def _window_attention(sink, q, k, vext):
    kernel = functools.partial(_win_kernel, scale=HEAD_DIM ** -0.5)
    n_pairs = HC // 2
    return pl.pallas_call(
        kernel,
        grid=(BATCH, n_pairs, SEQ // (WINDOW * WIN_BLOCKS)),
        in_specs=[
            pl.BlockSpec(memory_space=pltpu.SMEM),
            pl.BlockSpec((1, 1, WINDOW * WIN_BLOCKS, LANES), lambda b, p, i: (b, p, i, 0)),
            pl.BlockSpec((1, 1, WINDOW * WIN_BLOCKS, LANES), lambda b, p, i: (b, n_pairs + p, i, 0)),
            pl.BlockSpec((1, 1, SEQ, LANES), lambda b, p, i: (b, 0, 0, 0)),
            pl.BlockSpec((1, 1, SEQ, 2 * LANES), lambda b, p, i: (b, 0, 0, 0)),
        ],
        out_specs=pl.BlockSpec((1, WINDOW * WIN_BLOCKS, LANES), lambda b, p, i: (b, i, p)),
        out_shape=jax.ShapeDtypeStruct((BATCH, SEQ, n_pairs * LANES), BF16),
        compiler_params=pltpu.CompilerParams(
            dimension_semantics=("arbitrary", "arbitrary", "arbitrary"), vmem_limit_bytes=VMEM_LIMIT),
        name="win_attn",
    )(sink, q, q, k, vext)


def _out_proj_kernel(x_ref, oa_ref, ob_ref, oc_ref, wa_ref, wb_ref, wc_ref, g_ref, rwt_ref, rb_ref,
                     x1_ref, hm_ref, idx_ref, gate_ref, rank_ref, cnt_ref, run_ref):
    @pl.when(pl.program_id(0) == 0)
    def _():
        run_ref[...] = jnp.zeros_like(run_ref)

    x1 = (x_ref[...]
          + jnp.dot(oa_ref[...], wa_ref[...], preferred_element_type=F32)
          + jnp.dot(ob_ref[...], wb_ref[...], preferred_element_type=F32)
          + jnp.dot(oc_ref[...], wc_ref[...], preferred_element_type=F32))
    x1_ref[...] = x1
    hm = _rms(x1, g_ref[...])
    _store_token_tiles(hm_ref, hm)
    tm = hm.shape[0]

    logits = lax.dot_general(rwt_ref[...], hm, (((1,), (1,)), ((), ())),
                             precision=lax.Precision.HIGHEST, preferred_element_type=F32) + rb_ref[...]
    eidx = lax.broadcasted_iota(jnp.int32, (N_EXPERTS, tm), 0)
    vals, sels, hots = [], [], []
    cur = logits
    for _ in range(TOP_K):
        mx = jnp.max(cur, axis=0, keepdims=True)
        sel = jnp.min(jnp.where(cur == mx, eidx, N_EXPERTS), axis=0, keepdims=True)
        hot = eidx == sel
        vals.append(mx)
        sels.append(sel)
        hots.append(hot)
        cur = jnp.where(hot, -jnp.inf, cur)
    exps = [jnp.exp(v - vals[0]) for v in vals]
    denom = exps[0] + exps[1] + exps[2] + exps[3]
    gates = jnp.concatenate([e / denom for e in exps] + [jnp.zeros((LANES - TOP_K, tm), F32)], axis=0)
    gate_ref[...] = gates.T
    idx_ref[...] = jnp.concatenate(sels, axis=0)

    hot_all = jnp.zeros((N_EXPERTS, tm), F32)
    for hot in hots:
        hot_all = hot_all + jnp.where(hot, 1.0, 0.0)
    row = lax.broadcasted_iota(jnp.int32, (tm, tm), 0)
    col = lax.broadcasted_iota(jnp.int32, (tm, tm), 1)
    before = jnp.where(row < col, 1.0, 0.0).astype(BF16)
    rank_full = jnp.dot(hot_all.astype(BF16), before, preferred_element_type=F32) + run_ref[...]
    ranks = [jnp.sum(jnp.where(hot, rank_full, 0.0), axis=0, keepdims=True) for hot in hots]
    rank_ref[...] = jnp.concatenate(ranks, axis=0).astype(jnp.int32)
    run_ref[...] = run_ref[...] + jnp.sum(hot_all, axis=1, keepdims=True)
    cnt_ref[...] = jnp.broadcast_to(run_ref[...], (N_EXPERTS, LANES)).astype(jnp.int32)


def _out_proj(x, oa, ob, oc, wa, wb, wc, g, rwt, rb):
    tm = TM_OUT
    const = lambda i: (0, 0)
    row_blk = lambda w: pl.BlockSpec((tm, w), lambda i: (i, 0))
    col_blk = pl.BlockSpec((TOP_K, tm), lambda i: (0, i))
    return pl.pallas_call(
        _out_proj_kernel,
        grid=(TOKENS // tm,),
        in_specs=[
            row_blk(D_MODEL), row_blk(oa.shape[1]), row_blk(ob.shape[1]), row_blk(oc.shape[1]),
            pl.BlockSpec(wa.shape, const), pl.BlockSpec(wb.shape, const), pl.BlockSpec(wc.shape, const),
            pl.BlockSpec((1, D_MODEL), const),
            pl.BlockSpec((N_EXPERTS, D_MODEL), const),
            pl.BlockSpec((N_EXPERTS, 1), const),
        ],
        out_specs=[row_blk(D_MODEL), _token_tile_spec(tm, lambda i: i), col_blk, row_blk(LANES), col_blk,
                   pl.BlockSpec((N_EXPERTS, LANES), const)],
        out_shape=[
            jax.ShapeDtypeStruct((TOKENS, D_MODEL), F32),
            jax.ShapeDtypeStruct((TOKENS * ROW_TILES, LANES), F32),
            jax.ShapeDtypeStruct((TOP_K, TOKENS), jnp.int32),
            jax.ShapeDtypeStruct((TOKENS, LANES), F32),
            jax.ShapeDtypeStruct((TOP_K, TOKENS), jnp.int32),
            jax.ShapeDtypeStruct((N_EXPERTS, LANES), jnp.int32),
        ],
        scratch_shapes=[pltpu.VMEM((N_EXPERTS, 1), F32)],
        compiler_params=pltpu.CompilerParams(
            dimension_semantics=("arbitrary",), vmem_limit_bytes=VMEM_LIMIT),
        name="out_proj",
    )(x, oa, ob, oc, wa, wb, wc, g, rwt, rb)


def _issue_row_copies(pos_smem, tm, copy):
    for t in range(tm):
        for k in range(TOP_K):
            copy(t, k, pos_smem[k * (tm // LANES) + t // LANES, t % LANES])


def _dispatch_kernel(pad_start_ref, pad_count_ref, pos_hbm, hm_ref, xs_ref,
                     pos_smem, zero_ref, pos_sem, row_sems, pad_sem):
    i = pl.program_id(0)
    tm = hm_ref.shape[0] // ROW_TILES
    load = pltpu.make_async_copy(pos_hbm.at[i], pos_smem, pos_sem)
    load.start()

    @pl.when(i == 0)
    def _():
        zero_ref[...] = jnp.zeros_like(zero_ref)
        zero_row = _token_tile(zero_ref, 0)
        for e in range(N_EXPERTS):
            start = pad_start_ref[e]

            def fill(r, carry, start=start):
                pltpu.make_async_copy(zero_row, _token_tile(xs_ref, start + r), pad_sem).start()
                return carry

            def drain(r, carry):
                pltpu.make_async_copy(zero_row, _token_tile(xs_ref, 0), pad_sem).wait()
                return carry

            lax.fori_loop(0, pad_count_ref[e], fill, 0)
            lax.fori_loop(0, pad_count_ref[e], drain, 0)

        tail_start = pad_start_ref[N_EXPERTS]
        chunk = ZERO_ROWS * ROW_TILES

        def fill_tail(r, carry):
            dst = xs_ref.at[pl.ds(pl.multiple_of((tail_start + r * ZERO_ROWS) * ROW_TILES, chunk), chunk)]
            pltpu.make_async_copy(zero_ref, dst, pad_sem).start()
            return carry

        def drain_tail(r, carry):
            pltpu.make_async_copy(zero_ref, xs_ref.at[pl.ds(0, chunk)], pad_sem).wait()
            return carry

        lax.fori_loop(0, pad_count_ref[N_EXPERTS], fill_tail, 0)
        lax.fori_loop(0, pad_count_ref[N_EXPERTS], drain_tail, 0)

    load.wait()

    def copy(t, k, pos):
        pltpu.make_async_copy(_token_tile(hm_ref, t), _token_tile(xs_ref, pos),
                              row_sems.at[k]).start(priority=k % 2)

    _issue_row_copies(pos_smem, tm, copy)
    for k in range(TOP_K):
        pltpu.make_async_copy(hm_ref, _token_tiles(xs_ref, tm), row_sems.at[k]).wait()


def _dispatch(pad_start, pad_count, pos_tiles, hm):
    tm = TM_DISPATCH
    grid_spec = pltpu.PrefetchScalarGridSpec(
        num_scalar_prefetch=2,
        grid=(TOKENS // tm,),
        in_specs=[
            pl.BlockSpec(memory_space=pl.ANY),
            _token_tile_spec(tm, lambda i, ps, pc: i),
        ],
        out_specs=pl.BlockSpec(memory_space=pl.ANY),
        scratch_shapes=[
            pltpu.SMEM((TOP_K * tm // LANES, LANES), jnp.int32),
            pltpu.VMEM((ZERO_ROWS * ROW_TILES, LANES), F32),
            pltpu.SemaphoreType.DMA,
            pltpu.SemaphoreType.DMA((TOP_K,)),
            pltpu.SemaphoreType.DMA,
        ],
    )
    return pl.pallas_call(
        _dispatch_kernel,
        grid_spec=grid_spec,
        out_shape=jax.ShapeDtypeStruct((N_SORTED_ROWS * ROW_TILES, LANES), F32),
        compiler_params=pltpu.CompilerParams(
            dimension_semantics=("arbitrary",), vmem_limit_bytes=VMEM_LIMIT, has_side_effects=True),
        name="dispatch",
    )(pad_start, pad_count, pos_tiles, hm)


def _experts_kernel(te_ref, ts_ref, nr_ref, xs_ref, wup_ref, bup_ref, wdn_ref, bdn_ref, ys_ref,
                    wup_bf_ref, wdn_bf_ref):
    del ts_ref
    i = pl.program_id(0)
    n_rows = nr_ref[i]

    @pl.when(jnp.logical_or(i == 0, te_ref[i] != te_ref[jnp.maximum(i - 1, 0)]))
    def _():
        wup_bf_ref[...] = wup_ref[0, 0].astype(BF16)
        wdn_bf_ref[...] = wdn_ref[0, 0].astype(BF16)

    @pl.when(n_rows > 0)
    def _():
        x = _load_token_tiles(xs_ref).astype(BF16)
        gu = jnp.dot(x, wup_bf_ref[...], preferred_element_type=F32) + bup_ref[0, 0]
        x_glu = jnp.minimum(gu[:, :D_FF], SWIGLU_LIMIT)
        x_lin = jnp.clip(gu[:, D_FF:], -SWIGLU_LIMIT, SWIGLU_LIMIT)
        act = x_glu * jax.nn.sigmoid(SWIGLU_ALPHA * x_glu) * (x_lin + 1.0)
        _store_token_tiles(
            ys_ref, jnp.dot(act.astype(BF16), wdn_bf_ref[...], preferred_element_type=F32) + bdn_ref[0, 0])

    @pl.when(n_rows == 0)
    def _():
        ys_ref[...] = jnp.zeros_like(ys_ref)


def _experts(layer, tile_expert, tile_src, tile_rows, xs, wup, bup, wdn, bdn):
    tm = TM_EXPERT
    per_expert = lambda r, c: pl.BlockSpec((1, 1, r, c), lambda i, te, ts, nr: (layer, te[i], 0, 0))
    grid_spec = pltpu.PrefetchScalarGridSpec(
        num_scalar_prefetch=3,
        grid=(N_EXPERT_TILES,),
        in_specs=[
            _token_tile_spec(tm, lambda i, te, ts, nr: ts[i]),
            per_expert(D_MODEL, 2 * D_FF), per_expert(1, 2 * D_FF),
            per_expert(D_FF, D_MODEL), per_expert(1, D_MODEL),
        ],
        out_specs=_token_tile_spec(tm, lambda i, te, ts, nr: i),
        scratch_shapes=[pltpu.VMEM((D_MODEL, 2 * D_FF), BF16), pltpu.VMEM((D_FF, D_MODEL), BF16)],
    )
    return pl.pallas_call(
        _experts_kernel,
        grid_spec=grid_spec,
        out_shape=jax.ShapeDtypeStruct((N_SORTED_ROWS * ROW_TILES, LANES), F32),
        compiler_params=pltpu.CompilerParams(
            dimension_semantics=("arbitrary",), vmem_limit_bytes=VMEM_LIMIT),
        name="experts",
    )(tile_expert, tile_src, tile_rows, xs, wup, bup, wdn, bdn)


def _combine_kernel(pos_hbm, ys_hbm, x1_ref, gate_ref, p_ref, g_ref, wg_ref, wp_ref, gf_ref,
                    o_ref, pos_smem, rows_ref, pos_sem, row_sems, *, apply_final_norm):
    i = pl.program_id(0)
    tm = x1_ref.shape[0]
    slot = lax.rem(i, 2)

    def start_gather(tile, dst_slot):
        load = pltpu.make_async_copy(pos_hbm.at[tile], pos_smem, pos_sem)
        load.start()
        load.wait()

        def copy(t, k, pos):
            pltpu.make_async_copy(_token_tile(ys_hbm, pos), _token_tile(rows_ref.at[dst_slot, k], t),
                                  row_sems.at[dst_slot, k]).start(priority=k % 2)

        _issue_row_copies(pos_smem, tm, copy)

    @pl.when(i == 0)
    def _():
        start_gather(i, slot)

    @pl.when(i + 1 < pl.num_programs(0))
    def _():
        start_gather(i + 1, 1 - slot)

    for k in range(TOP_K):
        pltpu.make_async_copy(_token_tiles(ys_hbm, tm), rows_ref.at[slot, k], row_sems.at[slot, k]).wait()

    x2 = x1_ref[...]
    for k in range(TOP_K):
        x2 = x2 + _load_token_tiles(rows_ref.at[slot, k]) * gate_ref[:, k:k + 1]
    hp = _rms(x2, g_ref[...]).astype(BF16)
    gate = jax.nn.sigmoid(jnp.dot(hp, wg_ref[...], preferred_element_type=F32))
    pe = jnp.dot(p_ref[...].astype(BF16), wp_ref[...], preferred_element_type=F32)
    x3 = x2 + pe * gate
    if apply_final_norm:
        x3 = _rms(x3, gf_ref[...])
    o_ref[...] = x3


def _combine(pos_tiles, ys, x1, gates_t, p, g, wg, wp, gf, apply_final_norm):
    tm = TM_COMBINE
    const = lambda i: (0, 0)
    row_blk = lambda w: pl.BlockSpec((tm, w), lambda i: (i, 0))
    kernel = functools.partial(_combine_kernel, apply_final_norm=apply_final_norm)
    return pl.pallas_call(
        kernel,
        grid=(TOKENS // tm,),
        in_specs=[
            pl.BlockSpec(memory_space=pl.ANY),
            pl.BlockSpec(memory_space=pl.ANY),
            row_blk(D_MODEL), row_blk(LANES), row_blk(PLE_DIM),
            pl.BlockSpec((1, D_MODEL), const),
            pl.BlockSpec((D_MODEL, D_MODEL), const),
            pl.BlockSpec((PLE_DIM, D_MODEL), const),
            pl.BlockSpec((1, D_MODEL), const),
        ],
        out_specs=row_blk(D_MODEL),
        out_shape=jax.ShapeDtypeStruct((TOKENS, D_MODEL), F32),
        scratch_shapes=[
            pltpu.SMEM((TOP_K * tm // LANES, LANES), jnp.int32),
            pltpu.VMEM((2, TOP_K, tm * ROW_TILES, LANES), F32),
            pltpu.SemaphoreType.DMA,
            pltpu.SemaphoreType.DMA((2, TOP_K)),
        ],
        compiler_params=pltpu.CompilerParams(
            dimension_semantics=("arbitrary",), vmem_limit_bytes=VMEM_LIMIT),
        name="combine",
    )(pos_tiles, ys, x1, gates_t, p, g, wg, wp, gf)


def _in_proj_columns():
    src = np.full((Z_COLS,), -1, np.int64)
    src[Z_CQ:Z_CQ + Q_LORA] = np.arange(Q_LORA)
    src[Z_CKV:Z_CKV + KV_LORA] = Q_LORA + np.arange(KV_LORA)
    src[Z_KROPE + NOPE_A:Z_KROPE + NOPE_A + ROPE_A] = Q_LORA + KV_LORA + np.arange(ROPE_A)
    b0 = A_COLS
    for j in range(HB // 2):
        src[Z_BQ + j * LANES:Z_BQ + j * LANES + HALF] = b0 + j * HEAD_DIM + np.arange(HEAD_DIM)
        src[Z_BQ + j * LANES + HALF:Z_BQ + (j + 1) * LANES] = b0 + (HB // 2 + j) * HEAD_DIM + np.arange(HEAD_DIM)
    src[Z_BK:Z_BK + LANES] = b0 + HB * HEAD_DIM + np.arange(LANES)
    src[Z_BV:Z_BV + LANES] = b0 + (HB + KVB) * HEAD_DIM + np.arange(LANES)
    c0 = A_COLS + B_COLS
    for j in range(HC // 2):
        src[Z_CQS + j * LANES:Z_CQS + j * LANES + HALF] = c0 + j * HEAD_DIM + np.arange(HEAD_DIM)
        src[Z_CQS + j * LANES + HALF:Z_CQS + (j + 1) * LANES] = c0 + (HC // 2 + j) * HEAD_DIM + np.arange(HEAD_DIM)
    src[Z_CK:Z_CK + LANES] = c0 + HC * HEAD_DIM + np.arange(LANES)
    src[Z_CV:Z_CV + LANES] = c0 + (HC + KVC) * HEAD_DIM + np.arange(LANES)
    return src


def _wq_columns():
    src = np.full((HA * LANES,), -1, np.int64)
    dq = NOPE_A + ROPE_A
    for h in range(HA):
        src[h * LANES:h * LANES + dq] = h * dq + np.arange(dq)
    return src


def _wkv_columns():
    src = np.full((HA * LANES + (HA // 2) * LANES,), -1, np.int64)
    dkv = NOPE_A + V_A
    for h in range(HA):
        src[h * LANES:h * LANES + NOPE_A] = h * dkv + np.arange(NOPE_A)
        v0 = HA * LANES + (h // 2) * LANES + (h % 2) * HALF
        src[v0:v0 + V_A] = h * dkv + NOPE_A + np.arange(V_A)
    return src


def _paired_rows(base, n_heads):
    rows = []
    for j in range(n_heads // 2):
        rows.append(base + j * HEAD_DIM + np.arange(HEAD_DIM))
        rows.append(base + (n_heads // 2 + j) * HEAD_DIM + np.arange(HEAD_DIM))
    return np.concatenate(rows)


def _take_cols(w, src):
    cols = jnp.take(w, jnp.asarray(np.maximum(src, 0)), axis=1)
    return jnp.where(jnp.asarray(src >= 0)[None, :], cols, 0.0)


def _rope_tables():
    def cos_sin(pos, dim):
        inv = 1.0 / (ROPE_THETA ** (jnp.arange(0, dim, 2, dtype=F32) / dim))
        ang = pos.astype(F32)[:, None] * inv[None, :]
        return jnp.cos(ang), jnp.sin(ang)

    pos = jnp.arange(SEQ, dtype=jnp.int32)
    rows = pos // GRID_W
    cols = pos % GRID_W
    zeros = lambda w: jnp.zeros((SEQ, w), F32)
    ones = lambda w: jnp.ones((SEQ, w), F32)

    cos_a, sin_a = cos_sin(pos, ROPE_A)
    c_a = jnp.concatenate([ones(NOPE_A), cos_a, cos_a, ones(32)], axis=1)
    up_a = jnp.concatenate([zeros(NOPE_A), -sin_a, zeros(16), zeros(32)], axis=1)
    dn_a = jnp.concatenate([zeros(NOPE_A), zeros(16), sin_a, zeros(32)], axis=1)

    cos_r, sin_r = cos_sin(rows, HEAD_DIM // 2)
    cos_w, sin_w = cos_sin(cols, HEAD_DIM // 2)
    z16 = zeros(16)
    c_b = jnp.concatenate([cos_r, cos_r, cos_w, cos_w] * 2, axis=1)
    up_b = jnp.concatenate([-sin_r, z16, -sin_w, z16] * 2, axis=1)
    dn_b = jnp.concatenate([z16, sin_r, z16, sin_w] * 2, axis=1)

    cos_c, sin_c = cos_sin(pos, HEAD_DIM)
    z32 = zeros(32)
    c_c = jnp.concatenate([cos_c, cos_c] * 2, axis=1)
    up_c = jnp.concatenate([-sin_c, z32] * 2, axis=1)
    dn_c = jnp.concatenate([z32, sin_c] * 2, axis=1)
    return jnp.concatenate([c_a, up_a, dn_a, c_b, up_b, dn_b, c_c, up_c, dn_c], axis=1)


def _routing_tables(idx, rank, counts):
    tiles_per_expert = (counts + TM_EXPERT - 1) // TM_EXPERT
    tile_end = jnp.cumsum(tiles_per_expert)
    tile_start = tile_end - tiles_per_expert
    row_start = tile_start * TM_EXPERT
    experts = jnp.arange(N_EXPERTS, dtype=jnp.int32)
    pos = jnp.sum(jnp.where(idx[..., None] == experts, row_start, 0), axis=-1) + rank
    n_valid = tile_end[-1]
    tile_ids = jnp.arange(N_EXPERT_TILES, dtype=jnp.int32)
    tile_src = jnp.minimum(tile_ids, n_valid - 1)
    tile_expert = jnp.minimum(jnp.sum(tile_src[:, None] >= tile_end[None, :], axis=-1), N_EXPERTS - 1)
    hot = tile_expert[:, None] == experts
    rows_left = jnp.sum(jnp.where(hot, counts - (tile_src[:, None] - tile_start) * TM_EXPERT, 0), axis=-1)
    tile_rows = jnp.where(tile_ids < n_valid, jnp.clip(rows_left, 0, TM_EXPERT), 0)
    tail_start = n_valid * TM_EXPERT
    pad_start = jnp.concatenate([row_start + counts, tail_start[None]])
    pad_count = jnp.concatenate([tiles_per_expert * TM_EXPERT - counts,
                                 ((N_SORTED_ROWS - tail_start) // ZERO_ROWS)[None]])
    tables = (tile_expert, tile_src, tile_rows, pad_start, pad_count)
    return (pos,) + tuple(t.astype(jnp.int32) for t in tables)


def _pos_tiles(pos, tm):
    return pos.reshape(TOP_K, TOKENS // tm, tm).transpose(1, 0, 2).reshape(TOKENS // tm, TOP_K * tm // LANES, LANES)


def kernel(x, p, attn_norm, w_in, mla_q_norm, mla_wq_up, mla_kv_norm, mla_wkv_up, gqa_q_norm, gqa_k_norm,
           swa_sink, w_out, moe_norm, router_w, router_b, w_up, b_up, w_down, b_down, ple_norm, w_ple,
           w_ple_gate, final_norm):
    tabs = _rope_tables()
    in_cols, wq_cols, wkv_cols = _in_proj_columns(), _wq_columns(), _wkv_columns()
    rows_c = _paired_rows(HA * V_A + HB * HEAD_DIM, HC)

    xf = x.reshape(TOKENS, D_MODEL)
    for i in range(DEPTH):
        win = _take_cols(w_in[i], in_cols).astype(BF16)
        wq = _take_cols(mla_wq_up[i], wq_cols).astype(BF16)
        wkv = _take_cols(mla_wkv_up[i], wkv_cols).astype(BF16)
        gbq = jnp.tile(gqa_q_norm[i], 2)[None, :]
        gbk = jnp.tile(gqa_k_norm[i], 2)[None, :]
        qa, ka, va, qb, kb, vb, qc, kc, vc = _in_proj(
            xf.reshape(BATCH, SEQ, D_MODEL), attn_norm[i][None, :], win, mla_q_norm[i][None, :], wq,
            mla_kv_norm[i][None, :], wkv, gbq, gbk, tabs)

        oa = _dense_attention(qa, ka, va, HA, lambda h: h, lambda h: h, "attn_a")
        ob = _dense_attention(qb, kb, vb, HB, lambda h: h * 0, lambda h: h // (HB // KVB), "attn_b")
        oc = _window_attention(swa_sink[i], qc, kc, vc)

        wo = w_out[i]
        x1, hm, idx, gates_t, rank, cnt = _out_proj(
            xf, oa.reshape(TOKENS, -1), ob.reshape(TOKENS, -1), oc.reshape(TOKENS, -1),
            wo[:HA * V_A].astype(BF16), wo[HA * V_A:HA * V_A + HB * HEAD_DIM].astype(BF16),
            jnp.take(wo, jnp.asarray(rows_c), axis=0).astype(BF16),
            moe_norm[i][None, :], router_w[i].T, router_b[i][:, None])

        pos, tile_expert, tile_src, tile_rows, pad_start, pad_count = _routing_tables(idx, rank, cnt[:, 0])
        xs = _dispatch(pad_start, pad_count, _pos_tiles(pos, TM_DISPATCH), hm)
        ys = _experts(i, tile_expert, tile_src, tile_rows, xs, w_up, b_up[:, :, None, :],
                      w_down, b_down[:, :, None, :])
        xf = _combine(_pos_tiles(pos, TM_COMBINE), ys, x1, gates_t, p[i].reshape(TOKENS, PLE_DIM),
                      ple_norm[i][None, :], w_ple_gate[i].astype(BF16), w_ple[i].astype(BF16),
                      final_norm[None, :], apply_final_norm=(i == DEPTH - 1))
    return xf.reshape(BATCH, SEQ, D_MODEL)
```

```python
import functools
import math

import numpy as np
import jax
import jax.numpy as jnp
from jax import lax
from jax.experimental import pallas as pl
from jax.experimental.pallas import tpu as pltpu

F32 = jnp.float32
BF16 = jnp.bfloat16

D_MODEL = 1024
BATCH = 8
SEQ = 4096
DEPTH = 2
TOKENS = BATCH * SEQ
GRID_W = 64
PLE_DIM = 256
HEAD_DIM = 64
ROPE_THETA = 10000.0
EPS = 1e-6
HA, Q_LORA, KV_LORA, NOPE_A, ROPE_A, V_A = 6, 256, 128, 64, 32, 64
HB, KVB = 6, 2
HC, KVC, WINDOW = 4, 2, 128
A_COLS = Q_LORA + KV_LORA + ROPE_A
B_COLS = (HB + 2 * KVB) * HEAD_DIM
C_COLS = (HC + 2 * KVC) * HEAD_DIM
N_EXPERTS = 32
TOP_K = 4
D_FF = D_MODEL
SWIGLU_LIMIT = 7.0
SWIGLU_ALPHA = 1.702

LANES = 128
HALF = LANES // 2
BF16_SUBLANES = 16
ROW_TILES = D_MODEL // LANES

VT_ROWS = HEAD_DIM + BF16_SUBLANES

TM_IN = 512
TQ = 512
TQ_SUB = 256
WIN_BLOCKS = 4
TM_OUT = 512
TM_DISPATCH = 512
TM_EXPERT = 512
TM_COMBINE = 256
ZERO_ROWS = 64
N_EXPERT_TILES = TOKENS * TOP_K // TM_EXPERT + N_EXPERTS
N_SORTED_ROWS = N_EXPERT_TILES * TM_EXPERT
VMEM_LIMIT = 52 * 1024 * 1024

Z_CQ = 0
Z_CKV = Q_LORA
Z_KROPE = Z_CKV + KV_LORA
Z_BQ = Z_KROPE + LANES
Z_BK = Z_BQ + 3 * LANES
Z_BV = Z_BK + LANES
Z_CQS = Z_BV + LANES
Z_CK = Z_CQS + 2 * LANES
Z_CV = Z_CK + LANES
Z_COLS = Z_CV + LANES

NEG_BIG = -1e30
LOG2E = math.log2(math.e)
QSCALE_A = (NOPE_A + ROPE_A) ** -0.5 * LOG2E
QSCALE_B = HEAD_DIM ** -0.5 * LOG2E


def _rms(x, g):
    return x * lax.rsqrt(jnp.mean(x * x, axis=-1, keepdims=True) + EPS) * g


def _lane_is_low(shape):
    return lax.broadcasted_iota(jnp.int32, shape, len(shape) - 1) < HALF


def _rms_per_half(xs, g):
    low = _lane_is_low(xs.shape)
    x2 = xs * xs
    s_lo = jnp.sum(jnp.where(low, x2, 0.0), axis=-1, keepdims=True)
    s_hi = jnp.sum(jnp.where(low, 0.0, x2), axis=-1, keepdims=True)
    ms = jnp.where(low, s_lo, s_hi) * (1.0 / HEAD_DIM)
    return xs * lax.rsqrt(ms + EPS) * g


def _token_tile_spec(tm, index):
    return pl.BlockSpec((tm * ROW_TILES, LANES), lambda i, *prefetch: (index(i, *prefetch), 0))


def _token_tile(ref, row):
    return ref.at[pl.ds(pl.multiple_of(row * ROW_TILES, ROW_TILES), ROW_TILES)]


def _token_tiles(ref, n_rows):
    return ref.at[pl.ds(0, n_rows * ROW_TILES)]


def _store_token_tiles(ref, x):
    tm = x.shape[0]
    for c in range(ROW_TILES):
        ref[pl.ds(c, tm, stride=ROW_TILES), :] = x[:, c * LANES:(c + 1) * LANES]


def _load_token_tiles(ref):
    tm = ref.shape[0] // ROW_TILES
    return jnp.concatenate([ref[pl.ds(c, tm, stride=ROW_TILES), :] for c in range(ROW_TILES)], axis=1)


def _in_proj_kernel(x_ref, g_ref, win_ref, gq_ref, wq_ref, gkv_ref, wkv_ref, gbq_ref, gbk_ref, tab_ref,
                    qa_ref, ka_ref, va_ref, qb_ref, kb_ref, vb_ref, qc_ref, kc_ref, vc_ref):
    h = _rms(x_ref[0], g_ref[...])
    z = jnp.dot(h.astype(BF16), win_ref[...], preferred_element_type=F32)
    tm = z.shape[0]
    ones = jnp.ones((tm, LANES), BF16)
    ones_t = jnp.ones((VT_ROWS - HEAD_DIM, tm), BF16)
    low = _lane_is_low((tm, LANES))
    top = lax.broadcasted_iota(jnp.int32, (LANES, tm), 0) < HALF

    def rope(xs, table, shift):
        base = table * 3 * LANES
        c = tab_ref[:, base:base + LANES]
        s_up = tab_ref[:, base + LANES:base + 2 * LANES]
        s_dn = tab_ref[:, base + 2 * LANES:base + 3 * LANES]
        return xs * c + pltpu.roll(xs, LANES - shift, 1) * s_up + pltpu.roll(xs, shift, 1) * s_dn

    c_q = _rms(z[:, Z_CQ:Z_CQ + Q_LORA], gq_ref[...])
    q = jnp.dot(c_q.astype(BF16), wq_ref[...], preferred_element_type=F32)
    c_kv = _rms(z[:, Z_CKV:Z_CKV + KV_LORA], gkv_ref[...])
    kv = jnp.dot(c_kv.astype(BF16), wkv_ref[...], preferred_element_type=F32)
    k_rope = rope(z[:, Z_KROPE:Z_KROPE + LANES], 0, ROPE_A // 2)
    for hd in range(HA):
        qa_ref[0, hd] = (rope(q[:, hd * LANES:(hd + 1) * LANES], 0, ROPE_A // 2) * QSCALE_A).T.astype(BF16)
        ka_ref[0, hd] = (kv[:, hd * LANES:(hd + 1) * LANES] + k_rope).astype(BF16)
    for j in range(HA // 2):
        vt = kv[:, (HA + j) * LANES:(HA + j + 1) * LANES].T.astype(BF16)
        for half in range(2):
            va_ref[0, 2 * j + half, 0:HEAD_DIM, :] = vt[half * HALF:(half + 1) * HALF]
            va_ref[0, 2 * j + half, HEAD_DIM:VT_ROWS, :] = ones_t

    for j in range(HB // 2):
        s = rope(_rms_per_half(z[:, Z_BQ + j * LANES:Z_BQ + (j + 1) * LANES], gbq_ref[...]), 1, HEAD_DIM // 4)
        st = (s * QSCALE_B).T
        qb_ref[0, j] = jnp.where(top, st, 0.0).astype(BF16)
        qb_ref[0, HB // 2 + j] = jnp.where(top, 0.0, st).astype(BF16)
    kb_ref[0, 0] = rope(_rms_per_half(z[:, Z_BK:Z_BK + LANES], gbk_ref[...]), 1, HEAD_DIM // 4).astype(BF16)
    vt = z[:, Z_BV:Z_BV + LANES].T.astype(BF16)
    for g in range(KVB):
        vb_ref[0, g, 0:HEAD_DIM, :] = vt[g * HALF:(g + 1) * HALF]
        vb_ref[0, g, HEAD_DIM:VT_ROWS, :] = ones_t

    for j in range(HC // 2):
        s = rope(z[:, Z_CQS + j * LANES:Z_CQS + (j + 1) * LANES], 2, HEAD_DIM // 2)
        qc_ref[0, j] = jnp.where(low, s, 0.0).astype(BF16)
        qc_ref[0, HC // 2 + j] = jnp.where(low, 0.0, s).astype(BF16)
    kc_ref[0, 0] = rope(z[:, Z_CK:Z_CK + LANES], 2, HEAD_DIM // 2).astype(BF16)
    vc_ref[0, 0, :, 0:LANES] = z[:, Z_CV:Z_CV + LANES].astype(BF16)
    vc_ref[0, 0, :, LANES:2 * LANES] = ones


def _in_proj(x3, g, win, gq, wq, gkv, wkv, gbq, gbk, tabs):
    nst = SEQ // TM_IN
    const2 = lambda b, s: (0, 0)
    head_out = lambda n, w: (jax.ShapeDtypeStruct((BATCH, n, SEQ, w), BF16),
                             pl.BlockSpec((1, n, TM_IN, w), lambda b, s: (b, 0, s, 0)))
    head_out_t = lambda n, r: (jax.ShapeDtypeStruct((BATCH, n, r, SEQ), BF16),
                               pl.BlockSpec((1, n, r, TM_IN), lambda b, s: (b, 0, 0, s)))
    outs = [head_out_t(HA, LANES), head_out(HA, LANES), head_out_t(HA, VT_ROWS),
            head_out_t(HB, LANES), head_out(1, LANES), head_out_t(KVB, VT_ROWS),
            head_out(HC, LANES), head_out(1, LANES), head_out(1, 2 * LANES)]
    return pl.pallas_call(
        _in_proj_kernel,
        grid=(BATCH, nst),
        in_specs=[
            pl.BlockSpec((1, TM_IN, D_MODEL), lambda b, s: (b, s, 0)),
            pl.BlockSpec((1, D_MODEL), const2),
            pl.BlockSpec((D_MODEL, Z_COLS), const2),
            pl.BlockSpec((1, Q_LORA), const2),
            pl.BlockSpec((Q_LORA, HA * LANES), const2),
            pl.BlockSpec((1, KV_LORA), const2),
            pl.BlockSpec((KV_LORA, HA * LANES + (HA // 2) * LANES), const2),
            pl.BlockSpec((1, LANES), const2),
            pl.BlockSpec((1, LANES), const2),
            pl.BlockSpec((TM_IN, 9 * LANES), lambda b, s: (s, 0)),
        ],
        out_specs=[o[1] for o in outs],
        out_shape=[o[0] for o in outs],
        compiler_params=pltpu.CompilerParams(
            dimension_semantics=("arbitrary", "arbitrary"), vmem_limit_bytes=VMEM_LIMIT),
        name="in_proj",
    )(x3, g, win, gq, wq, gkv, wkv, gbq, gbk, tabs)


def _attn_kernel(q1_ref, q2_ref, k1_ref, k2_ref, v1_ref, v2_ref, o_ref,
                 st1a_ref, st1b_ref, st2a_ref, st2b_ref, pt1a_ref, pt1b_ref, pt2a_ref, pt2b_ref,
                 m1a_ref, m1b_ref, m2a_ref, m2b_ref):
    st_refs = ((st1a_ref, st1b_ref), (st2a_ref, st2b_ref))
    pt_refs = ((pt1a_ref, pt1b_ref), (pt2a_ref, pt2b_ref))
    m_refs = ((m1a_ref, m1b_ref), (m2a_ref, m2b_ref))
    q_refs, k_refs, v_refs = (q1_ref, q2_ref), (k1_ref, k2_ref), (v1_ref, v2_ref)
    n_sub = SEQ // TQ_SUB

    def scores(u, slot):
        cols = pl.ds(pl.multiple_of(u * TQ_SUB, TQ_SUB), TQ_SUB)
        for h in range(2):
            qt = q_refs[h][0, 0, :, cols]
            st = jnp.dot(k_refs[h][0, 0], qt, preferred_element_type=F32)
            st_refs[h][slot][...] = st
            m_refs[h][slot][...] = jnp.max(st, axis=0, keepdims=True)

    def probs(slot):
        for h in range(2):
            pt_refs[h][slot][...] = jnp.exp2((st_refs[h][slot][...] - m_refs[h][slot][...]).astype(BF16))

    def values(u, slot):
        outs = []
        for h in range(2):
            acc = jnp.dot(v_refs[h][0, 0], pt_refs[h][slot][...], preferred_element_type=F32)
            outs.append(acc[:HEAD_DIM] / acc[HEAD_DIM:HEAD_DIM + 1])
        rows = pl.ds(pl.multiple_of(u * TQ_SUB, TQ_SUB), TQ_SUB)
        o_ref[0, rows, :] = jnp.concatenate(outs, axis=0).T.astype(o_ref.dtype)

    scores(0, 0)
    scores(1, 1)
    probs(0)

    def body(g, carry):
        u = 2 * g
        scores(u, 0)
        probs(1)
        values(u - 2, 0)
        scores(u + 1, 1)
        probs(0)
        values(u - 1, 1)
        return carry

    lax.fori_loop(1, n_sub // 2, body, 0)
    probs(1)
    values(n_sub - 2, 0)
    values(n_sub - 1, 1)


def _dense_attention(qt, k, vt, n_heads, k_head, v_head, name):
    kernel = _attn_kernel
    q_spec = lambda off: pl.BlockSpec((1, 1, LANES, SEQ), lambda b, p: (b, 2 * p + off, 0, 0))
    k_spec = lambda off: pl.BlockSpec((1, 1, SEQ, LANES), lambda b, p: (b, k_head(2 * p + off), 0, 0))
    v_spec = lambda off: pl.BlockSpec((1, 1, VT_ROWS, SEQ), lambda b, p: (b, v_head(2 * p + off), 0, 0))
    return pl.pallas_call(
        kernel,
        grid=(BATCH, n_heads // 2),
        in_specs=[q_spec(0), q_spec(1), k_spec(0), k_spec(1), v_spec(0), v_spec(1)],
        out_specs=pl.BlockSpec((1, SEQ, LANES), lambda b, p: (b, 0, p)),
        out_shape=jax.ShapeDtypeStruct((BATCH, SEQ, n_heads * HEAD_DIM), BF16),
        scratch_shapes=([pltpu.VMEM((SEQ, TQ_SUB), F32)] * 4 + [pltpu.VMEM((SEQ, TQ_SUB), BF16)] * 4
                        + [pltpu.VMEM((1, TQ_SUB), F32)] * 4),
        compiler_params=pltpu.CompilerParams(
            dimension_semantics=("arbitrary", "arbitrary"), vmem_limit_bytes=VMEM_LIMIT),
        name=name,
    )(qt, qt, k, k, vt, vt)


def _win_kernel(sink_ref, q1_ref, q2_ref, k_ref, v_ref, o_ref, *, scale):
    pair = pl.program_id(1)
    nb = SEQ // WINDOW
    sink1 = sink_ref[pair]
    sink2 = sink_ref[HC // 2 + pair]
    low = _lane_is_low((WINDOW, LANES))
    rel = (lax.broadcasted_iota(jnp.int32, (WINDOW, 3 * WINDOW), 0)
           - lax.broadcasted_iota(jnp.int32, (WINDOW, 3 * WINDOW), 1))

    for blk in range(WIN_BLOCKS):
        n = pl.program_id(2) * WIN_BLOCKS + blk
        start = pl.multiple_of(jnp.clip(n - 1, 0, nb - 3) * WINDOW, WINDOW)
        k = k_ref[0, 0, pl.ds(start, 3 * WINDOW), :]
        v = v_ref[0, 0, pl.ds(start, 3 * WINDOW), :]
        band = jnp.abs(rel + (n * WINDOW - start)) <= WINDOW

        def one_head(q, sink):
            s = lax.dot_general(q, k, (((1,), (1,)), ((), ())), preferred_element_type=F32) * scale
            s = jnp.where(band, s, NEG_BIG)
            m = jnp.maximum(jnp.max(s, axis=-1, keepdims=True), sink)
            p = jnp.exp(s - m)
            acc = jnp.dot(p.astype(BF16), v, preferred_element_type=F32)
            return acc[:, :LANES] / (acc[:, LANES:] + jnp.exp(sink - m))

        rows = slice(blk * WINDOW, (blk + 1) * WINDOW)
        o1 = one_head(q1_ref[0, 0, rows, :], sink1)
        o2 = one_head(q2_ref[0, 0, rows, :], sink2)
        o_ref[0, rows, :] = jnp.where(low, o1, o2).astype(o_ref.dtype)


def _window_attention(sink, q, k, vext):
    kernel = functools.partial(_win_kernel, scale=HEAD_DIM ** -0.5)
    n_pairs = HC // 2
    return pl.pallas_call(
        kernel,
        grid=(BATCH, n_pairs, SEQ // (WINDOW * WIN_BLOCKS)),
        in_specs=[
            pl.BlockSpec(memory_space=pltpu.SMEM),
            pl.BlockSpec((1, 1, WINDOW * WIN_BLOCKS, LANES), lambda b, p, i: (b, p, i, 0)),
            pl.BlockSpec((1, 1, WINDOW * WIN_BLOCKS, LANES), lambda b, p, i: (b, n_pairs + p, i, 0)),
            pl.BlockSpec((1, 1, SEQ, LANES), lambda b, p, i: (b, 0, 0, 0)),
            pl.BlockSpec((1, 1, SEQ, 2 * LANES), lambda b, p, i: (b, 0, 0, 0)),
        ],
        out_specs=pl.BlockSpec((1, WINDOW * WIN_BLOCKS, LANES), lambda b, p, i: (b, i, p)),
        out_shape=jax.ShapeDtypeStruct((BATCH, SEQ, n_pairs * LANES), BF16),
        compiler_params=pltpu.CompilerParams(
            dimension_semantics=("arbitrary", "arbitrary", "arbitrary"), vmem_limit_bytes=VMEM_LIMIT),
        name="win_attn",
    )(sink, q, q, k, vext)


def _out_proj_kernel(x_ref, oa_ref, ob_ref, oc_ref, wa_ref, wb_ref, wc_ref, g_ref, rwt_ref, rb_ref,
                     x1_ref, hm_ref, idx_ref, gate_ref, rank_ref, cnt_ref, run_ref):
    @pl.when(pl.program_id(0) == 0)
    def _():
        run_ref[...] = jnp.zeros_like(run_ref)

    x1 = (x_ref[...]
          + jnp.dot(oa_ref[...], wa_ref[...], preferred_element_type=F32)
          + jnp.dot(ob_ref[...], wb_ref[...], preferred_element_type=F32)
          + jnp.dot(oc_ref[...], wc_ref[...], preferred_element_type=F32))
    x1_ref[...] = x1
    hm = _rms(x1, g_ref[...])
    _store_token_tiles(hm_ref, hm)
    tm = hm.shape[0]

    logits = lax.dot_general(rwt_ref[...], hm, (((1,), (1,)), ((), ())),
                             precision=lax.Precision.HIGHEST, preferred_element_type=F32) + rb_ref[...]
    eidx = lax.broadcasted_iota(jnp.int32, (N_EXPERTS, tm), 0)
    vals, sels, hots = [], [], []
    cur = logits
    for _ in range(TOP_K):
        mx = jnp.max(cur, axis=0, keepdims=True)
        sel = jnp.min(jnp.where(cur == mx, eidx, N_EXPERTS), axis=0, keepdims=True)
        hot = eidx == sel
        vals.append(mx)
        sels.append(sel)
        hots.append(hot)
        cur = jnp.where(hot, -jnp.inf, cur)
    exps = [jnp.exp(v - vals[0]) for v in vals]
    denom = exps[0] + exps[1] + exps[2] + exps[3]
    gates = jnp.concatenate([e / denom for e in exps] + [jnp.zeros((LANES - TOP_K, tm), F32)], axis=0)
    gate_ref[...] = gates.T
    idx_ref[...] = jnp.concatenate(sels, axis=0)

    hot_all = jnp.zeros((N_EXPERTS, tm), F32)
    for hot in hots:
        hot_all = hot_all + jnp.where(hot, 1.0, 0.0)
    row = lax.broadcasted_iota(jnp.int32, (tm, tm), 0)
    col = lax.broadcasted_iota(jnp.int32, (tm, tm), 1)
    before = jnp.where(row < col, 1.0, 0.0).astype(BF16)
    rank_full = jnp.dot(hot_all.astype(BF16), before, preferred_element_type=F32) + run_ref[...]
    ranks = [jnp.sum(jnp.where(hot, rank_full, 0.0), axis=0, keepdims=True) for hot in hots]
    rank_ref[...] = jnp.concatenate(ranks, axis=0).astype(jnp.int32)
    run_ref[...] = run_ref[...] + jnp.sum(hot_all, axis=1, keepdims=True)
    cnt_ref[...] = jnp.broadcast_to(run_ref[...], (N_EXPERTS, LANES)).astype(jnp.int32)


def _out_proj(x, oa, ob, oc, wa, wb, wc, g, rwt, rb):
    tm = TM_OUT
    const = lambda i: (0, 0)
    row_blk = lambda w: pl.BlockSpec((tm, w), lambda i: (i, 0))
    col_blk = pl.BlockSpec((TOP_K, tm), lambda i: (0, i))
    return pl.pallas_call(
        _out_proj_kernel,
        grid=(TOKENS // tm,),
        in_specs=[
            row_blk(D_MODEL), row_blk(oa.shape[1]), row_blk(ob.shape[1]), row_blk(oc.shape[1]),
            pl.BlockSpec(wa.shape, const), pl.BlockSpec(wb.shape, const), pl.BlockSpec(wc.shape, const),
            pl.BlockSpec((1, D_MODEL), const),
            pl.BlockSpec((N_EXPERTS, D_MODEL), const),
            pl.BlockSpec((N_EXPERTS, 1), const),
        ],
        out_specs=[row_blk(D_MODEL), _token_tile_spec(tm, lambda i: i), col_blk, row_blk(LANES), col_blk,
                   pl.BlockSpec((N_EXPERTS, LANES), const)],
        out_shape=[
            jax.ShapeDtypeStruct((TOKENS, D_MODEL), F32),
            jax.ShapeDtypeStruct((TOKENS * ROW_TILES, LANES), F32),
            jax.ShapeDtypeStruct((TOP_K, TOKENS), jnp.int32),
            jax.ShapeDtypeStruct((TOKENS, LANES), F32),
            jax.ShapeDtypeStruct((TOP_K, TOKENS), jnp.int32),
            jax.ShapeDtypeStruct((N_EXPERTS, LANES), jnp.int32),
        ],
        scratch_shapes=[pltpu.VMEM((N_EXPERTS, 1), F32)],
        compiler_params=pltpu.CompilerParams(
            dimension_semantics=("arbitrary",), vmem_limit_bytes=VMEM_LIMIT),
        name="out_proj",
    )(x, oa, ob, oc, wa, wb, wc, g, rwt, rb)


def _issue_row_copies(pos_smem, tm, copy):
    for t in range(tm):
        for k in range(TOP_K):
            copy(t, k, pos_smem[k * (tm // LANES) + t // LANES, t % LANES])


def _dispatch_kernel(pad_start_ref, pad_count_ref, pos_hbm, hm_ref, xs_ref,
                     pos_smem, zero_ref, pos_sem, row_sems, pad_sem):
    i = pl.program_id(0)
    tm = hm_ref.shape[0] // ROW_TILES
    load = pltpu.make_async_copy(pos_hbm.at[i], pos_smem, pos_sem)
    load.start()

    @pl.when(i == 0)
    def _():
        zero_ref[...] = jnp.zeros_like(zero_ref)
        zero_row = _token_tile(zero_ref, 0)
        for e in range(N_EXPERTS):
            start = pad_start_ref[e]

            def fill(r, carry, start=start):
                pltpu.make_async_copy(zero_row, _token_tile(xs_ref, start + r), pad_sem).start()
                return carry

            def drain(r, carry):
                pltpu.make_async_copy(zero_row, _token_tile(xs_ref, 0), pad_sem).wait()
                return carry

            lax.fori_loop(0, pad_count_ref[e], fill, 0)
            lax.fori_loop(0, pad_count_ref[e], drain, 0)

        tail_start = pad_start_ref[N_EXPERTS]
        chunk = ZERO_ROWS * ROW_TILES

        def fill_tail(r, carry):
            dst = xs_ref.at[pl.ds(pl.multiple_of((tail_start + r * ZERO_ROWS) * ROW_TILES, chunk), chunk)]
            pltpu.make_async_copy(zero_ref, dst, pad_sem).start()
            return carry

        def drain_tail(r, carry):
            pltpu.make_async_copy(zero_ref, xs_ref.at[pl.ds(0, chunk)], pad_sem).wait()
            return carry

        lax.fori_loop(0, pad_count_ref[N_EXPERTS], fill_tail, 0)
        lax.fori_loop(0, pad_count_ref[N_EXPERTS], drain_tail, 0)

    load.wait()

    def copy(t, k, pos):
        pltpu.make_async_copy(_token_tile(hm_ref, t), _token_tile(xs_ref, pos),
                              row_sems.at[k]).start(priority=k % 2)

    _issue_row_copies(pos_smem, tm, copy)
    for k in range(TOP_K):
        pltpu.make_async_copy(hm_ref, _token_tiles(xs_ref, tm), row_sems.at[k]).wait()


def _dispatch(pad_start, pad_count, pos_tiles, hm):
    tm = TM_DISPATCH
    grid_spec = pltpu.PrefetchScalarGridSpec(
        num_scalar_prefetch=2,
        grid=(TOKENS // tm,),
        in_specs=[
            pl.BlockSpec(memory_space=pl.ANY),
            _token_tile_spec(tm, lambda i, ps, pc: i),
        ],
        out_specs=pl.BlockSpec(memory_space=pl.ANY),
        scratch_shapes=[
            pltpu.SMEM((TOP_K * tm // LANES, LANES), jnp.int32),
            pltpu.VMEM((ZERO_ROWS * ROW_TILES, LANES), F32),
            pltpu.SemaphoreType.DMA,
            pltpu.SemaphoreType.DMA((TOP_K,)),
            pltpu.SemaphoreType.DMA,
        ],
    )
    return pl.pallas_call(
        _dispatch_kernel,
        grid_spec=grid_spec,
        out_shape=jax.ShapeDtypeStruct((N_SORTED_ROWS * ROW_TILES, LANES), F32),
        compiler_params=pltpu.CompilerParams(
            dimension_semantics=("arbitrary",), vmem_limit_bytes=VMEM_LIMIT, has_side_effects=True),
        name="dispatch",
    )(pad_start, pad_count, pos_tiles, hm)


def _experts_kernel(te_ref, ts_ref, nr_ref, xs_ref, wup_ref, bup_ref, wdn_ref, bdn_ref, ys_ref,
                    wup_bf_ref, wdn_bf_ref):
    del ts_ref
    i = pl.program_id(0)
    n_rows = nr_ref[i]

    @pl.when(jnp.logical_or(i == 0, te_ref[i] != te_ref[jnp.maximum(i - 1, 0)]))
    def _():
        wup_bf_ref[...] = wup_ref[0, 0].astype(BF16)
        wdn_bf_ref[...] = wdn_ref[0, 0].astype(BF16)

    @pl.when(n_rows > 0)
    def _():
        x = _load_token_tiles(xs_ref).astype(BF16)
        gu = jnp.dot(x, wup_bf_ref[...], preferred_element_type=F32) + bup_ref[0, 0]
        x_glu = jnp.minimum(gu[:, :D_FF], SWIGLU_LIMIT)
        x_lin = jnp.clip(gu[:, D_FF:], -SWIGLU_LIMIT, SWIGLU_LIMIT)
        act = x_glu * jax.nn.sigmoid(SWIGLU_ALPHA * x_glu) * (x_lin + 1.0)
        _store_token_tiles(
            ys_ref, jnp.dot(act.astype(BF16), wdn_bf_ref[...], preferred_element_type=F32) + bdn_ref[0, 0])

    @pl.when(n_rows == 0)
    def _():
        ys_ref[...] = jnp.zeros_like(ys_ref)


def _experts(layer, tile_expert, tile_src, tile_rows, xs, wup, bup, wdn, bdn):
    tm = TM_EXPERT
    per_expert = lambda r, c: pl.BlockSpec((1, 1, r, c), lambda i, te, ts, nr: (layer, te[i], 0, 0))
    grid_spec = pltpu.PrefetchScalarGridSpec(
        num_scalar_prefetch=3,
        grid=(N_EXPERT_TILES,),
        in_specs=[
            _token_tile_spec(tm, lambda i, te, ts, nr: ts[i]),
            per_expert(D_MODEL, 2 * D_FF), per_expert(1, 2 * D_FF),
            per_expert(D_FF, D_MODEL), per_expert(1, D_MODEL),
        ],
        out_specs=_token_tile_spec(tm, lambda i, te, ts, nr: i),
        scratch_shapes=[pltpu.VMEM((D_MODEL, 2 * D_FF), BF16), pltpu.VMEM((D_FF, D_MODEL), BF16)],
    )
    return pl.pallas_call(
        _experts_kernel,
        grid_spec=grid_spec,
        out_shape=jax.ShapeDtypeStruct((N_SORTED_ROWS * ROW_TILES, LANES), F32),
        compiler_params=pltpu.CompilerParams(
            dimension_semantics=("arbitrary",), vmem_limit_bytes=VMEM_LIMIT),
        name="experts",
    )(tile_expert, tile_src, tile_rows, xs, wup, bup, wdn, bdn)


def _combine_kernel(pos_hbm, ys_hbm, x1_ref, gate_ref, p_ref, g_ref, wg_ref, wp_ref, gf_ref,
                    o_ref, pos_smem, rows_a_ref, rows_b_ref, pos_sem, row_sems, *, n_steps, apply_final_norm):
    i = pl.program_id(0)
    tm = x1_ref.shape[0]
    rows_refs = (rows_a_ref, rows_b_ref)

    def load_positions(tile):
        load = pltpu.make_async_copy(pos_hbm.at[tile], pos_smem, pos_sem)
        load.start()
        load.wait()

    def start_gather(dst):
        def copy(t, k, pos):
            pltpu.make_async_copy(_token_tile(ys_hbm, pos), _token_tile(rows_refs[dst].at[k], t),
                                  row_sems.at[dst, k]).start(priority=k % 2)

        _issue_row_copies(pos_smem, tm, copy)

    def wait_gather(src):
        for k in range(TOP_K):
            pltpu.make_async_copy(_token_tiles(ys_hbm, tm), rows_refs[src].at[k], row_sems.at[src, k]).wait()

    def combine_from(src):
        x2 = x1_ref[...]
        for k in range(TOP_K):
            x2 = x2 + _load_token_tiles(rows_refs[src].at[k]) * gate_ref[:, k:k + 1]
        hp = _rms(x2, g_ref[...]).astype(BF16)
        gate = jax.nn.sigmoid(jnp.dot(hp, wg_ref[...], preferred_element_type=F32))
        pe = jnp.dot(p_ref[...].astype(BF16), wp_ref[...], preferred_element_type=F32)
        x3 = x2 + pe * gate
        if apply_final_norm:
            x3 = _rms(x3, gf_ref[...])
        o_ref[...] = x3

    @pl.when(i == 0)
    def _():
        load_positions(0)
        start_gather(0)

    for parity in range(2):
        @pl.when(lax.rem(i, 2) == parity)
        def _(parity=parity):
            load_positions(jnp.minimum(i + 1, n_steps - 1))
            wait_gather(parity)
            start_gather(1 - parity)
            combine_from(parity)

    @pl.when(i == n_steps - 1)
    def _():
        wait_gather(n_steps % 2)


def _combine(pos_tiles, ys, x1, gates_t, p, g, wg, wp, gf, apply_final_norm):
    tm = TM_COMBINE
    const = lambda i: (0, 0)
    row_blk = lambda w: pl.BlockSpec((tm, w), lambda i: (i, 0))
    n_steps = TOKENS // tm
    kernel = functools.partial(_combine_kernel, n_steps=n_steps, apply_final_norm=apply_final_norm)
    return pl.pallas_call(
        kernel,
        grid=(n_steps,),
        in_specs=[
            pl.BlockSpec(memory_space=pl.ANY),
            pl.BlockSpec(memory_space=pl.ANY),
            row_blk(D_MODEL), row_blk(LANES), row_blk(PLE_DIM),
            pl.BlockSpec((1, D_MODEL), const),
            pl.BlockSpec((D_MODEL, D_MODEL), const),
            pl.BlockSpec((PLE_DIM, D_MODEL), const),
            pl.BlockSpec((1, D_MODEL), const),
        ],
        out_specs=row_blk(D_MODEL),
        out_shape=jax.ShapeDtypeStruct((TOKENS, D_MODEL), F32),
        scratch_shapes=[
            pltpu.SMEM((TOP_K * tm // LANES, LANES), jnp.int32),
            pltpu.VMEM((TOP_K, tm * ROW_TILES, LANES), F32),
            pltpu.VMEM((TOP_K, tm * ROW_TILES, LANES), F32),
            pltpu.SemaphoreType.DMA,
            pltpu.SemaphoreType.DMA((2, TOP_K)),
        ],
        compiler_params=pltpu.CompilerParams(
            dimension_semantics=("arbitrary",), vmem_limit_bytes=VMEM_LIMIT),
        name="combine",
    )(pos_tiles, ys, x1, gates_t, p, g, wg, wp, gf)


def _in_proj_columns():
    src = np.full((Z_COLS,), -1, np.int64)
    src[Z_CQ:Z_CQ + Q_LORA] = np.arange(Q_LORA)
    src[Z_CKV:Z_CKV + KV_LORA] = Q_LORA + np.arange(KV_LORA)
    src[Z_KROPE + NOPE_A:Z_KROPE + NOPE_A + ROPE_A] = Q_LORA + KV_LORA + np.arange(ROPE_A)
    b0 = A_COLS
    for j in range(HB // 2):
        src[Z_BQ + j * LANES:Z_BQ + j * LANES + HALF] = b0 + j * HEAD_DIM + np.arange(HEAD_DIM)
        src[Z_BQ + j * LANES + HALF:Z_BQ + (j + 1) * LANES] = b0 + (HB // 2 + j) * HEAD_DIM + np.arange(HEAD_DIM)
    src[Z_BK:Z_BK + LANES] = b0 + HB * HEAD_DIM + np.arange(LANES)
    src[Z_BV:Z_BV + LANES] = b0 + (HB + KVB) * HEAD_DIM + np.arange(LANES)
    c0 = A_COLS + B_COLS
    for j in range(HC // 2):
        src[Z_CQS + j * LANES:Z_CQS + j * LANES + HALF] = c0 + j * HEAD_DIM + np.arange(HEAD_DIM)
        src[Z_CQS + j * LANES + HALF:Z_CQS + (j + 1) * LANES] = c0 + (HC // 2 + j) * HEAD_DIM + np.arange(HEAD_DIM)
    src[Z_CK:Z_CK + LANES] = c0 + HC * HEAD_DIM + np.arange(LANES)
    src[Z_CV:Z_CV + LANES] = c0 + (HC + KVC) * HEAD_DIM + np.arange(LANES)
    return src


def _wq_columns():
    src = np.full((HA * LANES,), -1, np.int64)
    dq = NOPE_A + ROPE_A
    for h in range(HA):
        src[h * LANES:h * LANES + dq] = h * dq + np.arange(dq)
    return src


def _wkv_columns():
    src = np.full((HA * LANES + (HA // 2) * LANES,), -1, np.int64)
    dkv = NOPE_A + V_A
    for h in range(HA):
        src[h * LANES:h * LANES + NOPE_A] = h * dkv + np.arange(NOPE_A)
        v0 = HA * LANES + (h // 2) * LANES + (h % 2) * HALF
        src[v0:v0 + V_A] = h * dkv + NOPE_A + np.arange(V_A)
    return src


def _paired_rows(base, n_heads):
    rows = []
    for j in range(n_heads // 2):
        rows.append(base + j * HEAD_DIM + np.arange(HEAD_DIM))
        rows.append(base + (n_heads // 2 + j) * HEAD_DIM + np.arange(HEAD_DIM))
    return np.concatenate(rows)


def _take_cols(w, src):
    cols = jnp.take(w, jnp.asarray(np.maximum(src, 0)), axis=1)
    return jnp.where(jnp.asarray(src >= 0)[None, :], cols, 0.0)


def _rope_tables():
    f32 = np.float32

    def cos_sin(pos, dim):
        inv = (f32(1.0) / (f32(ROPE_THETA) ** (np.arange(0, dim, 2, dtype=f32) / f32(dim)))).astype(f32)
        ang = pos.astype(f32)[:, None] * inv[None, :]
        return np.cos(ang).astype(f32), np.sin(ang).astype(f32)

    pos = np.arange(SEQ, dtype=np.int32)
    rows = pos // GRID_W
    cols = pos % GRID_W
    zeros = lambda w: np.zeros((SEQ, w), f32)
    ones = lambda w: np.ones((SEQ, w), f32)

    cos_a, sin_a = cos_sin(pos, ROPE_A)
    c_a = np.concatenate([ones(NOPE_A), cos_a, cos_a, ones(32)], axis=1)
    up_a = np.concatenate([zeros(NOPE_A), -sin_a, zeros(16), zeros(32)], axis=1)
    dn_a = np.concatenate([zeros(NOPE_A), zeros(16), sin_a, zeros(32)], axis=1)

    cos_r, sin_r = cos_sin(rows, HEAD_DIM // 2)
    cos_w, sin_w = cos_sin(cols, HEAD_DIM // 2)
    z16 = zeros(16)
    c_b = np.concatenate([cos_r, cos_r, cos_w, cos_w] * 2, axis=1)
    up_b = np.concatenate([-sin_r, z16, -sin_w, z16] * 2, axis=1)
    dn_b = np.concatenate([z16, sin_r, z16, sin_w] * 2, axis=1)

    cos_c, sin_c = cos_sin(pos, HEAD_DIM)
    z32 = zeros(32)
    c_c = np.concatenate([cos_c, cos_c] * 2, axis=1)
    up_c = np.concatenate([-sin_c, z32] * 2, axis=1)
    dn_c = np.concatenate([z32, sin_c] * 2, axis=1)
    return jnp.asarray(np.concatenate([c_a, up_a, dn_a, c_b, up_b, dn_b, c_c, up_c, dn_c], axis=1))


def _routing_tables(idx, rank, counts):
    tiles_per_expert = (counts + TM_EXPERT - 1) // TM_EXPERT
    tile_end = jnp.cumsum(tiles_per_expert)
    tile_start = tile_end - tiles_per_expert
    row_start = tile_start * TM_EXPERT
    experts = jnp.arange(N_EXPERTS, dtype=jnp.int32)
    pos = jnp.sum(jnp.where(idx[..., None] == experts, row_start, 0), axis=-1) + rank
    n_valid = tile_end[-1]
    tile_ids = jnp.arange(N_EXPERT_TILES, dtype=jnp.int32)
    tile_src = jnp.minimum(tile_ids, n_valid - 1)
    tile_expert = jnp.minimum(jnp.sum(tile_src[:, None] >= tile_end[None, :], axis=-1), N_EXPERTS - 1)
    hot = tile_expert[:, None] == experts
    rows_left = jnp.sum(jnp.where(hot, counts - (tile_src[:, None] - tile_start) * TM_EXPERT, 0), axis=-1)
    tile_rows = jnp.where(tile_ids < n_valid, jnp.clip(rows_left, 0, TM_EXPERT), 0)
    tail_start = n_valid * TM_EXPERT
    pad_start = jnp.concatenate([row_start + counts, tail_start[None]])
    pad_count = jnp.concatenate([tiles_per_expert * TM_EXPERT - counts,
                                 ((N_SORTED_ROWS - tail_start) // ZERO_ROWS)[None]])
    tables = (tile_expert, tile_src, tile_rows, pad_start, pad_count)
    return (pos,) + tuple(t.astype(jnp.int32) for t in tables)


def _pos_tiles(pos, tm):
    return pos.reshape(TOP_K, TOKENS // tm, tm).transpose(1, 0, 2).reshape(TOKENS // tm, TOP_K * tm // LANES, LANES)


def kernel(x, p, attn_norm, w_in, mla_q_norm, mla_wq_up, mla_kv_norm, mla_wkv_up, gqa_q_norm, gqa_k_norm,
           swa_sink, w_out, moe_norm, router_w, router_b, w_up, b_up, w_down, b_down, ple_norm, w_ple,
           w_ple_gate, final_norm):
    tabs = _rope_tables()
    in_cols, wq_cols, wkv_cols = _in_proj_columns(), _wq_columns(), _wkv_columns()
    rows_c = _paired_rows(HA * V_A + HB * HEAD_DIM, HC)

    xf = x.reshape(TOKENS, D_MODEL)
    for i in range(DEPTH):
        win = _take_cols(w_in[i], in_cols).astype(BF16)
        wq = _take_cols(mla_wq_up[i], wq_cols).astype(BF16)
        wkv = _take_cols(mla_wkv_up[i], wkv_cols).astype(BF16)
        gbq = jnp.tile(gqa_q_norm[i], 2)[None, :]
        gbk = jnp.tile(gqa_k_norm[i], 2)[None, :]
        qa, ka, va, qb, kb, vb, qc, kc, vc = _in_proj(
            xf.reshape(BATCH, SEQ, D_MODEL), attn_norm[i][None, :], win, mla_q_norm[i][None, :], wq,
            mla_kv_norm[i][None, :], wkv, gbq, gbk, tabs)

        oa = _dense_attention(qa, ka, va, HA, lambda h: h, lambda h: h, "attn_a")
        ob = _dense_attention(qb, kb, vb, HB, lambda h: h * 0, lambda h: h // (HB // KVB), "attn_b")
        oc = _window_attention(swa_sink[i], qc, kc, vc)

        wo = w_out[i]
        x1, hm, idx, gates_t, rank, cnt = _out_proj(
            xf, oa.reshape(TOKENS, -1), ob.reshape(TOKENS, -1), oc.reshape(TOKENS, -1),
            wo[:HA * V_A].astype(BF16), wo[HA * V_A:HA * V_A + HB * HEAD_DIM].astype(BF16),
            jnp.take(wo, jnp.asarray(rows_c), axis=0).astype(BF16),
            moe_norm[i][None, :], router_w[i].T, router_b[i][:, None])

        pos, tile_expert, tile_src, tile_rows, pad_start, pad_count = _routing_tables(idx, rank, cnt[:, 0])
        xs = _dispatch(pad_start, pad_count, _pos_tiles(pos, TM_DISPATCH), hm)
        ys = _experts(i, tile_expert, tile_src, tile_rows, xs, w_up, b_up[:, :, None, :],
                      w_down, b_down[:, :, None, :])
        xf = _combine(_pos_tiles(pos, TM_COMBINE), ys, x1, gates_t, p[i].reshape(TOKENS, PLE_DIM),
                      ple_norm[i][None, :], w_ple_gate[i].astype(BF16), w_ple[i].astype(BF16),
                      final_norm[None, :], apply_final_norm=(i == DEPTH - 1))
    return xf.reshape(BATCH, SEQ, D_MODEL)
```

```python
import functools
import math

import numpy as np
import jax
import jax.numpy as jnp
from jax import lax
from jax.experimental import pallas as pl
from jax.experimental.pallas import tpu as pltpu

F32 = jnp.float32
BF16 = jnp.bfloat16

D_MODEL = 1024
BATCH = 8
SEQ = 4096
DEPTH = 2
TOKENS = BATCH * SEQ
GRID_W = 64
PLE_DIM = 256
HEAD_DIM = 64
ROPE_THETA = 10000.0
EPS = 1e-6
HA, Q_LORA, KV_LORA, NOPE_A, ROPE_A, V_A = 6, 256, 128, 64, 32, 64
HB, KVB = 6, 2
HC, KVC, WINDOW = 4, 2, 128
A_COLS = Q_LORA + KV_LORA + ROPE_A
B_COLS = (HB + 2 * KVB) * HEAD_DIM
C_COLS = (HC + 2 * KVC) * HEAD_DIM
N_EXPERTS = 32
TOP_K = 4
D_FF = D_MODEL
SWIGLU_LIMIT = 7.0
SWIGLU_ALPHA = 1.702

LANES = 128
HALF = LANES // 2
BF16_SUBLANES = 16
ROW_TILES = D_MODEL // LANES

VT_ROWS = HEAD_DIM + BF16_SUBLANES

TM_IN = 512
TQ_SUB = 256
WIN_BLOCKS = 8
TM_OUT = 1024
TM_DISPATCH = 1024
TM_EXPERT = 512
TM_COMBINE = 256
ZERO_ROWS = 64
N_EXPERT_TILES = TOKENS * TOP_K // TM_EXPERT + N_EXPERTS
N_SORTED_ROWS = N_EXPERT_TILES * TM_EXPERT
VMEM_LIMIT = 52 * 1024 * 1024

Z_CQ = 0
Z_CKV = Q_LORA
Z_KROPE = Z_CKV + KV_LORA
Z_BQ = Z_KROPE + LANES
Z_BK = Z_BQ + 3 * LANES
Z_BV = Z_BK + LANES
Z_CQS = Z_BV + LANES
Z_CK = Z_CQS + 2 * LANES
Z_CV = Z_CK + LANES
Z_COLS = Z_CV + LANES

NEG_BIG = -1e30
LOG2E = math.log2(math.e)
QSCALE_A = (NOPE_A + ROPE_A) ** -0.5 * LOG2E
QSCALE_B = HEAD_DIM ** -0.5 * LOG2E


def _rms(x, g):
    return x * lax.rsqrt(jnp.mean(x * x, axis=-1, keepdims=True) + EPS) * g


def _lane_is_low(shape):
    return lax.broadcasted_iota(jnp.int32, shape, len(shape) - 1) < HALF


def _rms_per_half(xs, g):
    low = _lane_is_low(xs.shape)
    x2 = xs * xs
    s_lo = jnp.sum(jnp.where(low, x2, 0.0), axis=-1, keepdims=True)
    s_hi = jnp.sum(jnp.where(low, 0.0, x2), axis=-1, keepdims=True)
    ms = jnp.where(low, s_lo, s_hi) * (1.0 / HEAD_DIM)
    return xs * lax.rsqrt(ms + EPS) * g


def _token_tile_spec(tm, index):
    return pl.BlockSpec((tm * ROW_TILES, LANES), lambda i, *prefetch: (index(i, *prefetch), 0))


def _token_tile(ref, row):
    return ref.at[pl.ds(pl.multiple_of(row * ROW_TILES, ROW_TILES), ROW_TILES)]


def _token_tiles(ref, n_rows):
    return ref.at[pl.ds(0, n_rows * ROW_TILES)]


def _store_token_tiles(ref, x):
    tm = x.shape[0]
    for c in range(ROW_TILES):
        ref[pl.ds(c, tm, stride=ROW_TILES), :] = x[:, c * LANES:(c + 1) * LANES]


def _load_token_tiles(ref):
    tm = ref.shape[0] // ROW_TILES
    return jnp.concatenate([ref[pl.ds(c, tm, stride=ROW_TILES), :] for c in range(ROW_TILES)], axis=1)


def _in_proj_kernel(x_ref, g_ref, win_ref, gq_ref, wq_ref, gkv_ref, wkv_ref, gbq_ref, gbk_ref, tab_ref,
                    qa_ref, ka_ref, va_ref, qb_ref, kb_ref, vb_ref, qc_ref, kc_ref, vc_ref):
    h = _rms(x_ref[0], g_ref[...])
    z = jnp.dot(h.astype(BF16), win_ref[...], preferred_element_type=F32)
    tm = z.shape[0]
    ones = jnp.ones((tm, LANES), BF16)
    ones_t = jnp.ones((VT_ROWS - HEAD_DIM, tm), BF16)
    low = _lane_is_low((tm, LANES))
    top = lax.broadcasted_iota(jnp.int32, (LANES, tm), 0) < HALF

    def rope(xs, table, shift):
        base = table * 3 * LANES
        c = tab_ref[:, base:base + LANES]
        s_up = tab_ref[:, base + LANES:base + 2 * LANES]
        s_dn = tab_ref[:, base + 2 * LANES:base + 3 * LANES]
        return xs * c + pltpu.roll(xs, LANES - shift, 1) * s_up + pltpu.roll(xs, shift, 1) * s_dn

    c_q = _rms(z[:, Z_CQ:Z_CQ + Q_LORA], gq_ref[...])
    q = jnp.dot(c_q.astype(BF16), wq_ref[...], preferred_element_type=F32)
    c_kv = _rms(z[:, Z_CKV:Z_CKV + KV_LORA], gkv_ref[...])
    kv = jnp.dot(c_kv.astype(BF16), wkv_ref[...], preferred_element_type=F32)
    k_rope = rope(z[:, Z_KROPE:Z_KROPE + LANES], 0, ROPE_A // 2)
    for hd in range(HA):
        qa_ref[0, hd] = (rope(q[:, hd * LANES:(hd + 1) * LANES], 0, ROPE_A // 2) * QSCALE_A).T.astype(BF16)
        ka_ref[0, hd] = (kv[:, hd * LANES:(hd + 1) * LANES] + k_rope).astype(BF16)
    for j in range(HA // 2):
        vt = kv[:, (HA + j) * LANES:(HA + j + 1) * LANES].T.astype(BF16)
        for half in range(2):
            va_ref[0, 2 * j + half, 0:HEAD_DIM, :] = vt[half * HALF:(half + 1) * HALF]
            va_ref[0, 2 * j + half, HEAD_DIM:VT_ROWS, :] = ones_t

    for j in range(HB // 2):
        s = rope(_rms_per_half(z[:, Z_BQ + j * LANES:Z_BQ + (j + 1) * LANES], gbq_ref[...]), 1, HEAD_DIM // 4)
        st = (s * QSCALE_B).T
        qb_ref[0, j] = jnp.where(top, st, 0.0).astype(BF16)
        qb_ref[0, HB // 2 + j] = jnp.where(top, 0.0, st).astype(BF16)
    kb_ref[0, 0] = rope(_rms_per_half(z[:, Z_BK:Z_BK + LANES], gbk_ref[...]), 1, HEAD_DIM // 4).astype(BF16)
    vt = z[:, Z_BV:Z_BV + LANES].T.astype(BF16)
    for g in range(KVB):
        vb_ref[0, g, 0:HEAD_DIM, :] = vt[g * HALF:(g + 1) * HALF]
        vb_ref[0, g, HEAD_DIM:VT_ROWS, :] = ones_t

    for j in range(HC // 2):
        s = rope(z[:, Z_CQS + j * LANES:Z_CQS + (j + 1) * LANES], 2, HEAD_DIM // 2)
        qc_ref[0, j] = jnp.where(low, s, 0.0).astype(BF16)
        qc_ref[0, HC // 2 + j] = jnp.where(low, 0.0, s).astype(BF16)
    kc_ref[0, 0] = rope(z[:, Z_CK:Z_CK + LANES], 2, HEAD_DIM // 2).astype(BF16)
    vc_ref[0, 0, :, 0:LANES] = z[:, Z_CV:Z_CV + LANES].astype(BF16)
    vc_ref[0, 0, :, LANES:2 * LANES] = ones


def _in_proj(x3, g, win, gq, wq, gkv, wkv, gbq, gbk, tabs):
    nst = SEQ // TM_IN
    const2 = lambda b, s: (0, 0)
    head_out = lambda n, w: (jax.ShapeDtypeStruct((BATCH, n, SEQ, w), BF16),
                             pl.BlockSpec((1, n, TM_IN, w), lambda b, s: (b, 0, s, 0)))
    head_out_t = lambda n, r: (jax.ShapeDtypeStruct((BATCH, n, r, SEQ), BF16),
                               pl.BlockSpec((1, n, r, TM_IN), lambda b, s: (b, 0, 0, s)))
    outs = [head_out_t(HA, LANES), head_out(HA, LANES), head_out_t(HA, VT_ROWS),
            head_out_t(HB, LANES), head_out(1, LANES), head_out_t(KVB, VT_ROWS),
            head_out(HC, LANES), head_out(1, LANES), head_out(1, 2 * LANES)]
    return pl.pallas_call(
        _in_proj_kernel,
        grid=(BATCH, nst),
        in_specs=[
            pl.BlockSpec((1, TM_IN, D_MODEL), lambda b, s: (b, s, 0)),
            pl.BlockSpec((1, D_MODEL), const2),
            pl.BlockSpec((D_MODEL, Z_COLS), const2),
            pl.BlockSpec((1, Q_LORA), const2),
            pl.BlockSpec((Q_LORA, HA * LANES), const2),
            pl.BlockSpec((1, KV_LORA), const2),
            pl.BlockSpec((KV_LORA, HA * LANES + (HA // 2) * LANES), const2),
            pl.BlockSpec((1, LANES), const2),
            pl.BlockSpec((1, LANES), const2),
            pl.BlockSpec((TM_IN, 9 * LANES), lambda b, s: (s, 0)),
        ],
        out_specs=[o[1] for o in outs],
        out_shape=[o[0] for o in outs],
        compiler_params=pltpu.CompilerParams(
            dimension_semantics=("arbitrary", "arbitrary"), vmem_limit_bytes=VMEM_LIMIT),
        name="in_proj",
    )(x3, g, win, gq, wq, gkv, wkv, gbq, gbk, tabs)


def _attn_kernel(q1_ref, q2_ref, k1_ref, k2_ref, v1_ref, v2_ref, o_ref,
                 st1a_ref, st1b_ref, st2a_ref, st2b_ref, pt1a_ref, pt1b_ref, pt2a_ref, pt2b_ref,
                 m1a_ref, m1b_ref, m2a_ref, m2b_ref):
    st_refs = ((st1a_ref, st1b_ref), (st2a_ref, st2b_ref))
    pt_refs = ((pt1a_ref, pt1b_ref), (pt2a_ref, pt2b_ref))
    m_refs = ((m1a_ref, m1b_ref), (m2a_ref, m2b_ref))
    q_refs, k_refs, v_refs = (q1_ref, q2_ref), (k1_ref, k2_ref), (v1_ref, v2_ref)
    n_sub = SEQ // TQ_SUB

    def scores(u, slot):
        cols = pl.ds(pl.multiple_of(u * TQ_SUB, TQ_SUB), TQ_SUB)
        for h in range(2):
            qt = q_refs[h][0, 0, :, cols]
            st = jnp.dot(k_refs[h][0, 0], qt, preferred_element_type=F32)
            st_refs[h][slot][...] = st
            m_refs[h][slot][...] = jnp.max(st, axis=0, keepdims=True)

    def probs(slot):
        for h in range(2):
            pt_refs[h][slot][...] = jnp.exp2((st_refs[h][slot][...] - m_refs[h][slot][...]).astype(BF16))

    def values(u, slot):
        outs = []
        for h in range(2):
            acc = jnp.dot(v_refs[h][0, 0], pt_refs[h][slot][...], preferred_element_type=F32)
            outs.append(acc[:HEAD_DIM] / acc[HEAD_DIM:HEAD_DIM + 1])
        rows = pl.ds(pl.multiple_of(u * TQ_SUB, TQ_SUB), TQ_SUB)
        o_ref[0, rows, :] = jnp.concatenate(outs, axis=0).T.astype(o_ref.dtype)

    scores(0, 0)
    scores(1, 1)
    probs(0)

    def body(g, carry):
        u = 2 * g
        scores(u, 0)
        probs(1)
        values(u - 2, 0)
        scores(u + 1, 1)
        probs(0)
        values(u - 1, 1)
        return carry

    lax.fori_loop(1, n_sub // 2, body, 0)
    probs(1)
    values(n_sub - 2, 0)
    values(n_sub - 1, 1)


def _dense_attention(qt, k, vt, n_heads, k_head, v_head, name):
    kernel = _attn_kernel
    q_spec = lambda off: pl.BlockSpec((1, 1, LANES, SEQ), lambda b, p: (b, 2 * p + off, 0, 0))
    k_spec = lambda off: pl.BlockSpec((1, 1, SEQ, LANES), lambda b, p: (b, k_head(2 * p + off), 0, 0))
    v_spec = lambda off: pl.BlockSpec((1, 1, VT_ROWS, SEQ), lambda b, p: (b, v_head(2 * p + off), 0, 0))
    return pl.pallas_call(
        kernel,
        grid=(BATCH, n_heads // 2),
        in_specs=[q_spec(0), q_spec(1), k_spec(0), k_spec(1), v_spec(0), v_spec(1)],
        out_specs=pl.BlockSpec((1, SEQ, LANES), lambda b, p: (b, 0, p)),
        out_shape=jax.ShapeDtypeStruct((BATCH, SEQ, n_heads * HEAD_DIM), BF16),
        scratch_shapes=([pltpu.VMEM((SEQ, TQ_SUB), F32)] * 4 + [pltpu.VMEM((SEQ, TQ_SUB), BF16)] * 4
                        + [pltpu.VMEM((1, TQ_SUB), F32)] * 4),
        compiler_params=pltpu.CompilerParams(
            dimension_semantics=("arbitrary", "arbitrary"), vmem_limit_bytes=VMEM_LIMIT),
        name=name,
    )(qt, qt, k, k, vt, vt)


def _win_kernel(sink_ref, q1_ref, q2_ref, k_ref, v_ref, o_ref, *, scale):
    pair = pl.program_id(1)
    nb = SEQ // WINDOW
    sink1 = sink_ref[pair]
    sink2 = sink_ref[HC // 2 + pair]
    low = _lane_is_low((WINDOW, LANES))
    rel = (lax.broadcasted_iota(jnp.int32, (WINDOW, 3 * WINDOW), 0)
           - lax.broadcasted_iota(jnp.int32, (WINDOW, 3 * WINDOW), 1))

    for blk in range(WIN_BLOCKS):
        n = pl.program_id(2) * WIN_BLOCKS + blk
        start = pl.multiple_of(jnp.clip(n - 1, 0, nb - 3) * WINDOW, WINDOW)
        k = k_ref[0, 0, pl.ds(start, 3 * WINDOW), :]
        v = v_ref[0, 0, pl.ds(start, 3 * WINDOW), :]
        band = jnp.abs(rel + (n * WINDOW - start)) <= WINDOW

        def one_head(q, sink):
            s = lax.dot_general(q, k, (((1,), (1,)), ((), ())), preferred_element_type=F32) * scale
            s = jnp.where(band, s, NEG_BIG)
            m = jnp.maximum(jnp.max(s, axis=-1, keepdims=True), sink)
            p = jnp.exp(s - m)
            acc = jnp.dot(p.astype(BF16), v, preferred_element_type=F32)
            return acc[:, :LANES] / (acc[:, LANES:] + jnp.exp(sink - m))

        rows = slice(blk * WINDOW, (blk + 1) * WINDOW)
        o1 = one_head(q1_ref[0, 0, rows, :], sink1)
        o2 = one_head(q2_ref[0, 0, rows, :], sink2)
        o_ref[0, rows, :] = jnp.where(low, o1, o2).astype(o_ref.dtype)


def _window_attention(sink, q, k, vext):
    kernel = functools.partial(_win_kernel, scale=HEAD_DIM ** -0.5)
    n_pairs = HC // 2
    return pl.pallas_call(
        kernel,
        grid=(BATCH, n_pairs, SEQ // (WINDOW * WIN_BLOCKS)),
        in_specs=[
            pl.BlockSpec(memory_space=pltpu.SMEM),
            pl.BlockSpec((1, 1, WINDOW * WIN_BLOCKS, LANES), lambda b, p, i: (b, p, i, 0)),
            pl.BlockSpec((1, 1, WINDOW * WIN_BLOCKS, LANES), lambda b, p, i: (b, n_pairs + p, i, 0)),
            pl.BlockSpec((1, 1, SEQ, LANES), lambda b, p, i: (b, 0, 0, 0)),
            pl.BlockSpec((1, 1, SEQ, 2 * LANES), lambda b, p, i: (b, 0, 0, 0)),
        ],
        out_specs=pl.BlockSpec((1, WINDOW * WIN_BLOCKS, LANES), lambda b, p, i: (b, i, p)),
        out_shape=jax.ShapeDtypeStruct((BATCH, SEQ, n_pairs * LANES), BF16),
        compiler_params=pltpu.CompilerParams(
            dimension_semantics=("arbitrary", "arbitrary", "arbitrary"), vmem_limit_bytes=VMEM_LIMIT),
        name="win_attn",
    )(sink, q, q, k, vext)


def _out_proj_kernel(x_ref, oa_ref, ob_ref, oc_ref, wa_ref, wb_ref, wc_ref, g_ref, rwt_ref, rb_ref,
                     x1_ref, hm_ref, idx_ref, gate_ref, rank_ref, cnt_ref, run_ref):
    @pl.when(pl.program_id(0) == 0)
    def _():
        run_ref[...] = jnp.zeros_like(run_ref)

    x1 = (x_ref[...]
          + jnp.dot(oa_ref[...], wa_ref[...], preferred_element_type=F32)
          + jnp.dot(ob_ref[...], wb_ref[...], preferred_element_type=F32)
          + jnp.dot(oc_ref[...], wc_ref[...], preferred_element_type=F32))
    x1_ref[...] = x1
    hm = _rms(x1, g_ref[...])
    _store_token_tiles(hm_ref, hm)
    tm = hm.shape[0]

    logits = lax.dot_general(rwt_ref[...], hm, (((1,), (1,)), ((), ())),
                             precision=lax.Precision.HIGHEST, preferred_element_type=F32) + rb_ref[...]
    eidx = lax.broadcasted_iota(jnp.int32, (N_EXPERTS, tm), 0)
    vals, sels, hots = [], [], []
    cur = logits
    for _ in range(TOP_K):
        mx = jnp.max(cur, axis=0, keepdims=True)
        sel = jnp.min(jnp.where(cur == mx, eidx, N_EXPERTS), axis=0, keepdims=True)
        hot = eidx == sel
        vals.append(mx)
        sels.append(sel)
        hots.append(hot)
        cur = jnp.where(hot, -jnp.inf, cur)
    exps = [jnp.exp(v - vals[0]) for v in vals]
    denom = exps[0] + exps[1] + exps[2] + exps[3]
    gates = jnp.concatenate([e / denom for e in exps] + [jnp.zeros((LANES - TOP_K, tm), F32)], axis=0)
    gate_ref[...] = gates.T
    idx_ref[...] = jnp.concatenate(sels, axis=0)

    hot_all = jnp.zeros((N_EXPERTS, tm), F32)
    for hot in hots:
        hot_all = hot_all + jnp.where(hot, 1.0, 0.0)
    row = lax.broadcasted_iota(jnp.int32, (tm, tm), 0)
    col = lax.broadcasted_iota(jnp.int32, (tm, tm), 1)
    before = jnp.where(row < col, 1.0, 0.0).astype(BF16)
    rank_full = jnp.dot(hot_all.astype(BF16), before, preferred_element_type=F32) + run_ref[...]
    ranks = [jnp.sum(jnp.where(hot, rank_full, 0.0), axis=0, keepdims=True) for hot in hots]
    rank_ref[...] = jnp.concatenate(ranks, axis=0).astype(jnp.int32)
    run_ref[...] = run_ref[...] + jnp.sum(hot_all, axis=1, keepdims=True)
    cnt_ref[...] = jnp.broadcast_to(run_ref[...], (N_EXPERTS, LANES)).astype(jnp.int32)


def _out_proj(x, oa, ob, oc, wa, wb, wc, g, rwt, rb):
    tm = TM_OUT
    const = lambda i: (0, 0)
    row_blk = lambda w: pl.BlockSpec((tm, w), lambda i: (i, 0))
    col_blk = pl.BlockSpec((TOP_K, tm), lambda i: (0, i))
    return pl.pallas_call(
        _out_proj_kernel,
        grid=(TOKENS // tm,),
        in_specs=[
            row_blk(D_MODEL), row_blk(oa.shape[1]), row_blk(ob.shape[1]), row_blk(oc.shape[1]),
            pl.BlockSpec(wa.shape, const), pl.BlockSpec(wb.shape, const), pl.BlockSpec(wc.shape, const),
            pl.BlockSpec((1, D_MODEL), const),
            pl.BlockSpec((N_EXPERTS, D_MODEL), const),
            pl.BlockSpec((N_EXPERTS, 1), const),
        ],
        out_specs=[row_blk(D_MODEL), _token_tile_spec(tm, lambda i: i), col_blk, row_blk(LANES), col_blk,
                   pl.BlockSpec((N_EXPERTS, LANES), const)],
        out_shape=[
            jax.ShapeDtypeStruct((TOKENS, D_MODEL), F32),
            jax.ShapeDtypeStruct((TOKENS * ROW_TILES, LANES), F32),
            jax.ShapeDtypeStruct((TOP_K, TOKENS), jnp.int32),
            jax.ShapeDtypeStruct((TOKENS, LANES), F32),
            jax.ShapeDtypeStruct((TOP_K, TOKENS), jnp.int32),
            jax.ShapeDtypeStruct((N_EXPERTS, LANES), jnp.int32),
        ],
        scratch_shapes=[pltpu.VMEM((N_EXPERTS, 1), F32)],
        compiler_params=pltpu.CompilerParams(
            dimension_semantics=("arbitrary",), vmem_limit_bytes=VMEM_LIMIT),
        name="out_proj",
    )(x, oa, ob, oc, wa, wb, wc, g, rwt, rb)


def _issue_row_copies(pos_smem, tm, copy):
    for t in range(tm):
        for k in range(TOP_K):
            copy(t, k, pos_smem[k * (tm // LANES) + t // LANES, t % LANES])


def _issue_row_copies_looped(pos_smem, tm, copy):
    for chunk in range(tm // LANES):
        def issue(j, carry, chunk=chunk):
            for k in range(TOP_K):
                copy(chunk * LANES + j, k, pos_smem[k * (tm // LANES) + chunk, j])
            return carry
        lax.fori_loop(0, LANES, issue, 0, unroll=16)


def _dispatch_kernel(pad_start_ref, pad_count_ref, pos_hbm, hm_ref, xs_ref,
                     pos_smem, zero_ref, pos_sem, row_sems, pad_sem):
    i = pl.program_id(0)
    tm = hm_ref.shape[0] // ROW_TILES
    load = pltpu.make_async_copy(pos_hbm.at[i], pos_smem, pos_sem)
    load.start()

    @pl.when(i == 0)
    def _():
        zero_ref[...] = jnp.zeros_like(zero_ref)
        zero_row = _token_tile(zero_ref, 0)
        for e in range(N_EXPERTS):
            start = pad_start_ref[e]

            def fill(r, carry, start=start):
                pltpu.make_async_copy(zero_row, _token_tile(xs_ref, start + r), pad_sem).start()
                return carry

            def drain(r, carry):
                pltpu.make_async_copy(zero_row, _token_tile(xs_ref, 0), pad_sem).wait()
                return carry

            lax.fori_loop(0, pad_count_ref[e], fill, 0)
            lax.fori_loop(0, pad_count_ref[e], drain, 0)

        tail_start = pad_start_ref[N_EXPERTS]
        chunk = ZERO_ROWS * ROW_TILES

        def fill_tail(r, carry):
            dst = xs_ref.at[pl.ds(pl.multiple_of((tail_start + r * ZERO_ROWS) * ROW_TILES, chunk), chunk)]
            pltpu.make_async_copy(zero_ref, dst, pad_sem).start()
            return carry

        def drain_tail(r, carry):
            pltpu.make_async_copy(zero_ref, xs_ref.at[pl.ds(0, chunk)], pad_sem).wait()
            return carry

        lax.fori_loop(0, pad_count_ref[N_EXPERTS], fill_tail, 0)
        lax.fori_loop(0, pad_count_ref[N_EXPERTS], drain_tail, 0)

    load.wait()

    def copy(t, k, pos):
        pltpu.make_async_copy(_token_tile(hm_ref, t), _token_tile(xs_ref, pos),
                              row_sems.at[k]).start(priority=k % 2)

    _issue_row_copies_looped(pos_smem, tm, copy)
    for k in range(TOP_K):
        pltpu.make_async_copy(hm_ref, _token_tiles(xs_ref, tm), row_sems.at[k]).wait()


def _dispatch(pad_start, pad_count, pos_tiles, hm):
    tm = TM_DISPATCH
    grid_spec = pltpu.PrefetchScalarGridSpec(
        num_scalar_prefetch=2,
        grid=(TOKENS // tm,),
        in_specs=[
            pl.BlockSpec(memory_space=pl.ANY),
            _token_tile_spec(tm, lambda i, ps, pc: i),
        ],
        out_specs=pl.BlockSpec(memory_space=pl.ANY),
        scratch_shapes=[
            pltpu.SMEM((TOP_K * tm // LANES, LANES), jnp.int32),
            pltpu.VMEM((ZERO_ROWS * ROW_TILES, LANES), F32),
            pltpu.SemaphoreType.DMA,
            pltpu.SemaphoreType.DMA((TOP_K,)),
            pltpu.SemaphoreType.DMA,
        ],
    )
    return pl.pallas_call(
        _dispatch_kernel,
        grid_spec=grid_spec,
        out_shape=jax.ShapeDtypeStruct((N_SORTED_ROWS * ROW_TILES, LANES), F32),
        compiler_params=pltpu.CompilerParams(
            dimension_semantics=("arbitrary",), vmem_limit_bytes=VMEM_LIMIT, has_side_effects=True),
        name="dispatch",
    )(pad_start, pad_count, pos_tiles, hm)


def _experts_kernel(te_ref, ts_ref, nr_ref, xs_ref, wup_ref, bup_ref, wdn_ref, bdn_ref, ys_ref,
                    wup_bf_ref, wdn_bf_ref):
    del ts_ref
    i = pl.program_id(0)
    n_rows = nr_ref[i]

    @pl.when(jnp.logical_or(i == 0, te_ref[i] != te_ref[jnp.maximum(i - 1, 0)]))
    def _():
        wup_bf_ref[...] = wup_ref[0, 0].astype(BF16)
        wdn_bf_ref[...] = wdn_ref[0, 0].astype(BF16)

    @pl.when(n_rows > 0)
    def _():
        x = _load_token_tiles(xs_ref).astype(BF16)
        gu = jnp.dot(x, wup_bf_ref[...], preferred_element_type=F32) + bup_ref[0, 0]
        x_glu = jnp.minimum(gu[:, :D_FF], SWIGLU_LIMIT)
        x_lin = jnp.clip(gu[:, D_FF:], -SWIGLU_LIMIT, SWIGLU_LIMIT)
        act = x_glu * jax.nn.sigmoid(SWIGLU_ALPHA * x_glu) * (x_lin + 1.0)
        _store_token_tiles(
            ys_ref, jnp.dot(act.astype(BF16), wdn_bf_ref[...], preferred_element_type=F32) + bdn_ref[0, 0])

    @pl.when(n_rows == 0)
    def _():
        ys_ref[...] = jnp.zeros_like(ys_ref)


def _experts(layer, tile_expert, tile_src, tile_rows, xs, wup, bup, wdn, bdn):
    tm = TM_EXPERT
    per_expert = lambda r, c: pl.BlockSpec((1, 1, r, c), lambda i, te, ts, nr: (layer, te[i], 0, 0))
    grid_spec = pltpu.PrefetchScalarGridSpec(
        num_scalar_prefetch=3,
        grid=(N_EXPERT_TILES,),
        in_specs=[
            _token_tile_spec(tm, lambda i, te, ts, nr: ts[i]),
            per_expert(D_MODEL, 2 * D_FF), per_expert(1, 2 * D_FF),
            per_expert(D_FF, D_MODEL), per_expert(1, D_MODEL),
        ],
        out_specs=_token_tile_spec(tm, lambda i, te, ts, nr: i),
        scratch_shapes=[pltpu.VMEM((D_MODEL, 2 * D_FF), BF16), pltpu.VMEM((D_FF, D_MODEL), BF16)],
    )
    return pl.pallas_call(
        _experts_kernel,
        grid_spec=grid_spec,
        out_shape=jax.ShapeDtypeStruct((N_SORTED_ROWS * ROW_TILES, LANES), F32),
        compiler_params=pltpu.CompilerParams(
            dimension_semantics=("arbitrary",), vmem_limit_bytes=VMEM_LIMIT),
        name="experts",
    )(tile_expert, tile_src, tile_rows, xs, wup, bup, wdn, bdn)


def _combine_kernel(pos_hbm, ys_hbm, x1_ref, gate_ref, p_ref, g_ref, wg_ref, wp_ref, gf_ref,
                    o_ref, pos_smem, rows_ref, pos_sem, row_sems, *, apply_final_norm):
    i = pl.program_id(0)
    tm = x1_ref.shape[0]
    slot = lax.rem(i, 2)

    def start_gather(tile, dst_slot):
        load = pltpu.make_async_copy(pos_hbm.at[tile], pos_smem, pos_sem)
        load.start()
        load.wait()

        def copy(t, k, pos):
            pltpu.make_async_copy(_token_tile(ys_hbm, pos), _token_tile(rows_ref.at[dst_slot, k], t),
                                  row_sems.at[dst_slot, k]).start(priority=k % 2)

        _issue_row_copies(pos_smem, tm, copy)

    @pl.when(i == 0)
    def _():
        start_gather(i, slot)

    @pl.when(i + 1 < pl.num_programs(0))
    def _():
        start_gather(i + 1, 1 - slot)

    for k in range(TOP_K):
        pltpu.make_async_copy(_token_tiles(ys_hbm, tm), rows_ref.at[slot, k], row_sems.at[slot, k]).wait()

    x2 = x1_ref[...]
    for k in range(TOP_K):
        x2 = x2 + _load_token_tiles(rows_ref.at[slot, k]) * gate_ref[:, k:k + 1]
    hp = _rms(x2, g_ref[...]).astype(BF16)
    gate = jax.nn.sigmoid(jnp.dot(hp, wg_ref[...], preferred_element_type=F32))
    pe = jnp.dot(p_ref[...].astype(BF16), wp_ref[...], preferred_element_type=F32)
    x3 = x2 + pe * gate
    if apply_final_norm:
        x3 = _rms(x3, gf_ref[...])
    o_ref[...] = x3


def _combine(pos_tiles, ys, x1, gates_t, p, g, wg, wp, gf, apply_final_norm):
    tm = TM_COMBINE
    const = lambda i: (0, 0)
    row_blk = lambda w: pl.BlockSpec((tm, w), lambda i: (i, 0))
    kernel = functools.partial(_combine_kernel, apply_final_norm=apply_final_norm)
    return pl.pallas_call(
        kernel,
        grid=(TOKENS // tm,),
        in_specs=[
            pl.BlockSpec(memory_space=pl.ANY),
            pl.BlockSpec(memory_space=pl.ANY),
            row_blk(D_MODEL), row_blk(LANES), row_blk(PLE_DIM),
            pl.BlockSpec((1, D_MODEL), const),
            pl.BlockSpec((D_MODEL, D_MODEL), const),
            pl.BlockSpec((PLE_DIM, D_MODEL), const),
            pl.BlockSpec((1, D_MODEL), const),
        ],
        out_specs=row_blk(D_MODEL),
        out_shape=jax.ShapeDtypeStruct((TOKENS, D_MODEL), F32),
        scratch_shapes=[
            pltpu.SMEM((TOP_K * tm // LANES, LANES), jnp.int32),
            pltpu.VMEM((2, TOP_K, tm * ROW_TILES, LANES), F32),
            pltpu.SemaphoreType.DMA,
            pltpu.SemaphoreType.DMA((2, TOP_K)),
        ],
        compiler_params=pltpu.CompilerParams(
            dimension_semantics=("arbitrary",), vmem_limit_bytes=VMEM_LIMIT),
        name="combine",
    )(pos_tiles, ys, x1, gates_t, p, g, wg, wp, gf)


def _in_proj_columns():
    src = np.full((Z_COLS,), -1, np.int64)
    src[Z_CQ:Z_CQ + Q_LORA] = np.arange(Q_LORA)
    src[Z_CKV:Z_CKV + KV_LORA] = Q_LORA + np.arange(KV_LORA)
    src[Z_KROPE + NOPE_A:Z_KROPE + NOPE_A + ROPE_A] = Q_LORA + KV_LORA + np.arange(ROPE_A)
    b0 = A_COLS
    for j in range(HB // 2):
        src[Z_BQ + j * LANES:Z_BQ + j * LANES + HALF] = b0 + j * HEAD_DIM + np.arange(HEAD_DIM)
        src[Z_BQ + j * LANES + HALF:Z_BQ + (j + 1) * LANES] = b0 + (HB // 2 + j) * HEAD_DIM + np.arange(HEAD_DIM)
    src[Z_BK:Z_BK + LANES] = b0 + HB * HEAD_DIM + np.arange(LANES)
    src[Z_BV:Z_BV + LANES] = b0 + (HB + KVB) * HEAD_DIM + np.arange(LANES)
    c0 = A_COLS + B_COLS
    for j in range(HC // 2):
        src[Z_CQS + j * LANES:Z_CQS + j * LANES + HALF] = c0 + j * HEAD_DIM + np.arange(HEAD_DIM)
        src[Z_CQS + j * LANES + HALF:Z_CQS + (j + 1) * LANES] = c0 + (HC // 2 + j) * HEAD_DIM + np.arange(HEAD_DIM)
    src[Z_CK:Z_CK + LANES] = c0 + HC * HEAD_DIM + np.arange(LANES)
    src[Z_CV:Z_CV + LANES] = c0 + (HC + KVC) * HEAD_DIM + np.arange(LANES)
    return src


def _wq_columns():
    src = np.full((HA * LANES,), -1, np.int64)
    dq = NOPE_A + ROPE_A
    for h in range(HA):
        src[h * LANES:h * LANES + dq] = h * dq + np.arange(dq)
    return src


def _wkv_columns():
    src = np.full((HA * LANES + (HA // 2) * LANES,), -1, np.int64)
    dkv = NOPE_A + V_A
    for h in range(HA):
        src[h * LANES:h * LANES + NOPE_A] = h * dkv + np.arange(NOPE_A)
        v0 = HA * LANES + (h // 2) * LANES + (h % 2) * HALF
        src[v0:v0 + V_A] = h * dkv + NOPE_A + np.arange(V_A)
    return src


def _paired_rows(base, n_heads):
    rows = []
    for j in range(n_heads // 2):
        rows.append(base + j * HEAD_DIM + np.arange(HEAD_DIM))
        rows.append(base + (n_heads // 2 + j) * HEAD_DIM + np.arange(HEAD_DIM))
    return np.concatenate(rows)


def _take_cols(w, src):
    cols = jnp.take(w, jnp.asarray(np.maximum(src, 0)), axis=1)
    return jnp.where(jnp.asarray(src >= 0)[None, :], cols, 0.0)


def _rope_tables():
    f32 = np.float32

    def cos_sin(pos, dim):
        inv = (f32(1.0) / (f32(ROPE_THETA) ** (np.arange(0, dim, 2, dtype=f32) / f32(dim)))).astype(f32)
        ang = pos.astype(f32)[:, None] * inv[None, :]
        return np.cos(ang).astype(f32), np.sin(ang).astype(f32)

    pos = np.arange(SEQ, dtype=np.int32)
    rows = pos // GRID_W
    cols = pos % GRID_W
    zeros = lambda w: np.zeros((SEQ, w), f32)
    ones = lambda w: np.ones((SEQ, w), f32)

    cos_a, sin_a = cos_sin(pos, ROPE_A)
    c_a = np.concatenate([ones(NOPE_A), cos_a, cos_a, ones(32)], axis=1)
    up_a = np.concatenate([zeros(NOPE_A), -sin_a, zeros(16), zeros(32)], axis=1)
    dn_a = np.concatenate([zeros(NOPE_A), zeros(16), sin_a, zeros(32)], axis=1)

    cos_r, sin_r = cos_sin(rows, HEAD_DIM // 2)
    cos_w, sin_w = cos_sin(cols, HEAD_DIM // 2)
    z16 = zeros(16)
    c_b = np.concatenate([cos_r, cos_r, cos_w, cos_w] * 2, axis=1)
    up_b = np.concatenate([-sin_r, z16, -sin_w, z16] * 2, axis=1)
    dn_b = np.concatenate([z16, sin_r, z16, sin_w] * 2, axis=1)

    cos_c, sin_c = cos_sin(pos, HEAD_DIM)
    z32 = zeros(32)
    c_c = np.concatenate([cos_c, cos_c] * 2, axis=1)
    up_c = np.concatenate([-sin_c, z32] * 2, axis=1)
    dn_c = np.concatenate([z32, sin_c] * 2, axis=1)
    return jnp.asarray(np.concatenate([c_a, up_a, dn_a, c_b, up_b, dn_b, c_c, up_c, dn_c], axis=1))


def _routing_tables(idx, rank, counts):
    tiles_per_expert = (counts + TM_EXPERT - 1) // TM_EXPERT
    tile_end = jnp.cumsum(tiles_per_expert)
    tile_start = tile_end - tiles_per_expert
    row_start = tile_start * TM_EXPERT
    experts = jnp.arange(N_EXPERTS, dtype=jnp.int32)
    pos = jnp.sum(jnp.where(idx[..., None] == experts, row_start, 0), axis=-1) + rank
    n_valid = tile_end[-1]
    tile_ids = jnp.arange(N_EXPERT_TILES, dtype=jnp.int32)
    tile_src = jnp.minimum(tile_ids, n_valid - 1)
    tile_expert = jnp.minimum(jnp.sum(tile_src[:, None] >= tile_end[None, :], axis=-1), N_EXPERTS - 1)
    hot = tile_expert[:, None] == experts
    rows_left = jnp.sum(jnp.where(hot, counts - (tile_src[:, None] - tile_start) * TM_EXPERT, 0), axis=-1)
    tile_rows = jnp.where(tile_ids < n_valid, jnp.clip(rows_left, 0, TM_EXPERT), 0)
    tail_start = n_valid * TM_EXPERT
    pad_start = jnp.concatenate([row_start + counts, tail_start[None]])
    pad_count = jnp.concatenate([tiles_per_expert * TM_EXPERT - counts,
                                 ((N_SORTED_ROWS - tail_start) // ZERO_ROWS)[None]])
    tables = (tile_expert, tile_src, tile_rows, pad_start, pad_count)
    return (pos,) + tuple(t.astype(jnp.int32) for t in tables)


def _pos_tiles(pos, tm):
    return pos.reshape(TOP_K, TOKENS // tm, tm).transpose(1, 0, 2).reshape(TOKENS // tm, TOP_K * tm // LANES, LANES)


def kernel(x, p, attn_norm, w_in, mla_q_norm, mla_wq_up, mla_kv_norm, mla_wkv_up, gqa_q_norm, gqa_k_norm,
           swa_sink, w_out, moe_norm, router_w, router_b, w_up, b_up, w_down, b_down, ple_norm, w_ple,
           w_ple_gate, final_norm):
    tabs = _rope_tables()
    in_cols, wq_cols, wkv_cols = _in_proj_columns(), _wq_columns(), _wkv_columns()
    rows_c = _paired_rows(HA * V_A + HB * HEAD_DIM, HC)

    xf = x.reshape(TOKENS, D_MODEL)
    for i in range(DEPTH):
        win = _take_cols(w_in[i], in_cols).astype(BF16)
        wq = _take_cols(mla_wq_up[i], wq_cols).astype(BF16)
        wkv = _take_cols(mla_wkv_up[i], wkv_cols).astype(BF16)
        gbq = jnp.tile(gqa_q_norm[i], 2)[None, :]
        gbk = jnp.tile(gqa_k_norm[i], 2)[None, :]
        qa, ka, va, qb, kb, vb, qc, kc, vc = _in_proj(
            xf.reshape(BATCH, SEQ, D_MODEL), attn_norm[i][None, :], win, mla_q_norm[i][None, :], wq,
            mla_kv_norm[i][None, :], wkv, gbq, gbk, tabs)

        oa = _dense_attention(qa, ka, va, HA, lambda h: h, lambda h: h, "attn_a")
        ob = _dense_attention(qb, kb, vb, HB, lambda h: h * 0, lambda h: h // (HB // KVB), "attn_b")
        oc = _window_attention(swa_sink[i], qc, kc, vc)

        wo = w_out[i]
        x1, hm, idx, gates_t, rank, cnt = _out_proj(
            xf, oa.reshape(TOKENS, -1), ob.reshape(TOKENS, -1), oc.reshape(TOKENS, -1),
            wo[:HA * V_A].astype(BF16), wo[HA * V_A:HA * V_A + HB * HEAD_DIM].astype(BF16),
            jnp.take(wo, jnp.asarray(rows_c), axis=0).astype(BF16),
            moe_norm[i][None, :], router_w[i].T, router_b[i][:, None])

        pos, tile_expert, tile_src, tile_rows, pad_start, pad_count = _routing_tables(idx, rank, cnt[:, 0])
        xs = _dispatch(pad_start, pad_count, _pos_tiles(pos, TM_DISPATCH), hm)
        ys = _experts(i, tile_expert, tile_src, tile_rows, xs, w_up, b_up[:, :, None, :],
                      w_down, b_down[:, :, None, :])
        xf = _combine(_pos_tiles(pos, TM_COMBINE), ys, x1, gates_t, p[i].reshape(TOKENS, PLE_DIM),
                      ple_norm[i][None, :], w_ple_gate[i].astype(BF16), w_ple[i].astype(BF16),
                      final_norm[None, :], apply_final_norm=(i == DEPTH - 1))
    return xf.reshape(BATCH, SEQ, D_MODEL)
```

```python
import functools
import math

import numpy as np
import jax
import jax.numpy as jnp
from jax import lax
from jax.experimental import pallas as pl
from jax.experimental.pallas import tpu as pltpu

F32 = jnp.float32
BF16 = jnp.bfloat16

D_MODEL = 1024
BATCH = 8
SEQ = 4096
DEPTH = 2
TOKENS = BATCH * SEQ
GRID_W = 64
PLE_DIM = 256
HEAD_DIM = 64
ROPE_THETA = 10000.0
EPS = 1e-6
HA, Q_LORA, KV_LORA, NOPE_A, ROPE_A, V_A = 6, 256, 128, 64, 32, 64
HB, KVB = 6, 2
HC, KVC, WINDOW = 4, 2, 128
A_COLS = Q_LORA + KV_LORA + ROPE_A
B_COLS = (HB + 2 * KVB) * HEAD_DIM
C_COLS = (HC + 2 * KVC) * HEAD_DIM
N_EXPERTS = 32
TOP_K = 4
D_FF = D_MODEL
SWIGLU_LIMIT = 7.0
SWIGLU_ALPHA = 1.702

LANES = 128
HALF = LANES // 2
BF16_SUBLANES = 16
ROW_TILES = D_MODEL // LANES

VT_ROWS = HEAD_DIM + BF16_SUBLANES

TM_IN = 512
TQ_SUB = 256
WIN_BLOCKS = 8
TM_OUT = 1024
TM_DISPATCH = 1024
TM_EXPERT = 512
TM_COMBINE = 256
ZERO_ROWS = 64
N_EXPERT_TILES = TOKENS * TOP_K // TM_EXPERT + N_EXPERTS
N_SORTED_ROWS = N_EXPERT_TILES * TM_EXPERT
VMEM_LIMIT = 52 * 1024 * 1024

Z_CQ = 0
Z_CKV = Q_LORA
Z_KROPE = Z_CKV + KV_LORA
Z_BQ = Z_KROPE + LANES
Z_BK = Z_BQ + 3 * LANES
Z_BV = Z_BK + LANES
Z_CQS = Z_BV + LANES
Z_CK = Z_CQS + 2 * LANES
Z_CV = Z_CK + LANES
Z_COLS = Z_CV + LANES

NEG_BIG = -1e30
LOG2E = math.log2(math.e)
QSCALE_A = (NOPE_A + ROPE_A) ** -0.5 * LOG2E
QSCALE_B = HEAD_DIM ** -0.5 * LOG2E


def _rms(x, g):
    return x * lax.rsqrt(jnp.mean(x * x, axis=-1, keepdims=True) + EPS) * g


def _lane_is_low(shape):
    return lax.broadcasted_iota(jnp.int32, shape, len(shape) - 1) < HALF


def _rms_per_half(xs, g):
    low = _lane_is_low(xs.shape)
    x2 = xs * xs
    s_lo = jnp.sum(jnp.where(low, x2, 0.0), axis=-1, keepdims=True)
    s_hi = jnp.sum(jnp.where(low, 0.0, x2), axis=-1, keepdims=True)
    ms = jnp.where(low, s_lo, s_hi) * (1.0 / HEAD_DIM)
    return xs * lax.rsqrt(ms + EPS) * g


def _token_tile_spec(tm, index):
    return pl.BlockSpec((tm * ROW_TILES, LANES), lambda i, *prefetch: (index(i, *prefetch), 0))


def _token_tile(ref, row):
    return ref.at[pl.ds(pl.multiple_of(row * ROW_TILES, ROW_TILES), ROW_TILES)]


def _token_tiles(ref, n_rows):
    return ref.at[pl.ds(0, n_rows * ROW_TILES)]


def _store_token_tiles(ref, x):
    tm = x.shape[0]
    for c in range(ROW_TILES):
        ref[pl.ds(c, tm, stride=ROW_TILES), :] = x[:, c * LANES:(c + 1) * LANES]


def _load_token_tiles(ref):
    tm = ref.shape[0] // ROW_TILES
    return jnp.concatenate([ref[pl.ds(c, tm, stride=ROW_TILES), :] for c in range(ROW_TILES)], axis=1)


def _in_proj_kernel(x_ref, g_ref, win_ref, gq_ref, wq_ref, gkv_ref, wkv_ref, gbq_ref, gbk_ref, tab_ref,
                    qa_ref, ka_ref, va_ref, qb_ref, kb_ref, vb_ref, qc_ref, kc_ref, vc_ref):
    h = _rms(x_ref[0], g_ref[...])
    z = jnp.dot(h.astype(BF16), win_ref[...], preferred_element_type=F32)
    tm = z.shape[0]
    ones = jnp.ones((tm, LANES), BF16)
    ones_t = jnp.ones((VT_ROWS - HEAD_DIM, tm), BF16)
    low = _lane_is_low((tm, LANES))
    top = lax.broadcasted_iota(jnp.int32, (LANES, tm), 0) < HALF

    def rope(xs, table, shift):
        base = table * 3 * LANES
        c = tab_ref[:, base:base + LANES]
        s_up = tab_ref[:, base + LANES:base + 2 * LANES]
        s_dn = tab_ref[:, base + 2 * LANES:base + 3 * LANES]
        return xs * c + pltpu.roll(xs, LANES - shift, 1) * s_up + pltpu.roll(xs, shift, 1) * s_dn

    c_q = _rms(z[:, Z_CQ:Z_CQ + Q_LORA], gq_ref[...])
    q = jnp.dot(c_q.astype(BF16), wq_ref[...], preferred_element_type=F32)
    c_kv = _rms(z[:, Z_CKV:Z_CKV + KV_LORA], gkv_ref[...])
    kv = jnp.dot(c_kv.astype(BF16), wkv_ref[...], preferred_element_type=F32)
    k_rope = rope(z[:, Z_KROPE:Z_KROPE + LANES], 0, ROPE_A // 2)
    for hd in range(HA):
        qa_ref[0, hd] = (rope(q[:, hd * LANES:(hd + 1) * LANES], 0, ROPE_A // 2) * QSCALE_A).T.astype(BF16)
        ka_ref[0, hd] = (kv[:, hd * LANES:(hd + 1) * LANES] + k_rope).astype(BF16)
    for j in range(HA // 2):
        vt = kv[:, (HA + j) * LANES:(HA + j + 1) * LANES].T.astype(BF16)
        for half in range(2):
            va_ref[0, 2 * j + half, 0:HEAD_DIM, :] = vt[half * HALF:(half + 1) * HALF]
            va_ref[0, 2 * j + half, HEAD_DIM:VT_ROWS, :] = ones_t

    for j in range(HB // 2):
        s = rope(_rms_per_half(z[:, Z_BQ + j * LANES:Z_BQ + (j + 1) * LANES], gbq_ref[...]), 1, HEAD_DIM // 4)
        st = (s * QSCALE_B).T
        qb_ref[0, j] = jnp.where(top, st, 0.0).astype(BF16)
        qb_ref[0, HB // 2 + j] = jnp.where(top, 0.0, st).astype(BF16)
    kb_ref[0, 0] = rope(_rms_per_half(z[:, Z_BK:Z_BK + LANES], gbk_ref[...]), 1, HEAD_DIM // 4).astype(BF16)
    vt = z[:, Z_BV:Z_BV + LANES].T.astype(BF16)
    for g in range(KVB):
        vb_ref[0, g, 0:HEAD_DIM, :] = vt[g * HALF:(g + 1) * HALF]
        vb_ref[0, g, HEAD_DIM:VT_ROWS, :] = ones_t

    for j in range(HC // 2):
        s = rope(z[:, Z_CQS + j * LANES:Z_CQS + (j + 1) * LANES], 2, HEAD_DIM // 2)
        qc_ref[0, j] = jnp.where(low, s, 0.0).astype(BF16)
        qc_ref[0, HC // 2 + j] = jnp.where(low, 0.0, s).astype(BF16)
    kc_ref[0, 0] = rope(z[:, Z_CK:Z_CK + LANES], 2, HEAD_DIM // 2).astype(BF16)
    vc_ref[0, 0, :, 0:LANES] = z[:, Z_CV:Z_CV + LANES].astype(BF16)
    vc_ref[0, 0, :, LANES:2 * LANES] = ones


def _in_proj(x3, g, win, gq, wq, gkv, wkv, gbq, gbk, tabs):
    nst = SEQ // TM_IN
    const2 = lambda b, s: (0, 0)
    head_out = lambda n, w: (jax.ShapeDtypeStruct((BATCH, n, SEQ, w), BF16),
                             pl.BlockSpec((1, n, TM_IN, w), lambda b, s: (b, 0, s, 0)))
    head_out_t = lambda n, r: (jax.ShapeDtypeStruct((BATCH, n, r, SEQ), BF16),
                               pl.BlockSpec((1, n, r, TM_IN), lambda b, s: (b, 0, 0, s)))
    outs = [head_out_t(HA, LANES), head_out(HA, LANES), head_out_t(HA, VT_ROWS),
            head_out_t(HB, LANES), head_out(1, LANES), head_out_t(KVB, VT_ROWS),
            head_out(HC, LANES), head_out(1, LANES), head_out(1, 2 * LANES)]
    return pl.pallas_call(
        _in_proj_kernel,
        grid=(BATCH, nst),
        in_specs=[
            pl.BlockSpec((1, TM_IN, D_MODEL), lambda b, s: (b, s, 0)),
            pl.BlockSpec((1, D_MODEL), const2),
            pl.BlockSpec((D_MODEL, Z_COLS), const2),
            pl.BlockSpec((1, Q_LORA), const2),
            pl.BlockSpec((Q_LORA, HA * LANES), const2),
            pl.BlockSpec((1, KV_LORA), const2),
            pl.BlockSpec((KV_LORA, HA * LANES + (HA // 2) * LANES), const2),
            pl.BlockSpec((1, LANES), const2),
            pl.BlockSpec((1, LANES), const2),
            pl.BlockSpec((TM_IN, 9 * LANES), lambda b, s: (s, 0)),
        ],
        out_specs=[o[1] for o in outs],
        out_shape=[o[0] for o in outs],
        compiler_params=pltpu.CompilerParams(
            dimension_semantics=("arbitrary", "arbitrary"), vmem_limit_bytes=VMEM_LIMIT),
        name="in_proj",
    )(x3, g, win, gq, wq, gkv, wkv, gbq, gbk, tabs)


def _attn_kernel(q1_ref, q2_ref, k1_ref, k2_ref, v1_ref, v2_ref, o_ref,
                 st1a_ref, st1b_ref, st2a_ref, st2b_ref, pt1a_ref, pt1b_ref, pt2a_ref, pt2b_ref,
                 m1a_ref, m1b_ref, m2a_ref, m2b_ref):
    st_refs = ((st1a_ref, st1b_ref), (st2a_ref, st2b_ref))
    pt_refs = ((pt1a_ref, pt1b_ref), (pt2a_ref, pt2b_ref))
    m_refs = ((m1a_ref, m1b_ref), (m2a_ref, m2b_ref))
    q_refs, k_refs, v_refs = (q1_ref, q2_ref), (k1_ref, k2_ref), (v1_ref, v2_ref)
    n_sub = SEQ // TQ_SUB

    def scores(u, slot):
        cols = pl.ds(pl.multiple_of(u * TQ_SUB, TQ_SUB), TQ_SUB)
        for h in range(2):
            qt = q_refs[h][0, 0, :, cols]
            st = jnp.dot(k_refs[h][0, 0], qt, preferred_element_type=F32)
            st_refs[h][slot][...] = st
            m_refs[h][slot][...] = jnp.max(st, axis=0, keepdims=True)

    def probs(slot):
        for h in range(2):
            pt_refs[h][slot][...] = jnp.exp2((st_refs[h][slot][...] - m_refs[h][slot][...]).astype(BF16))

    def values(u, slot):
        outs = []
        for h in range(2):
            acc = jnp.dot(v_refs[h][0, 0], pt_refs[h][slot][...], preferred_element_type=F32)
            outs.append(acc[:HEAD_DIM] / acc[HEAD_DIM:HEAD_DIM + 1])
        rows = pl.ds(pl.multiple_of(u * TQ_SUB, TQ_SUB), TQ_SUB)
        o_ref[0, rows, :] = jnp.concatenate(outs, axis=0).T.astype(o_ref.dtype)

    scores(0, 0)
    scores(1, 1)
    probs(0)

    def body(g, carry):
        u = 2 * g
        scores(u, 0)
        probs(1)
        values(u - 2, 0)
        scores(u + 1, 1)
        probs(0)
        values(u - 1, 1)
        return carry

    lax.fori_loop(1, n_sub // 2, body, 0)
    probs(1)
    values(n_sub - 2, 0)
    values(n_sub - 1, 1)


def _dense_attention(qt, k, vt, n_heads, k_head, v_head, name):
    kernel = _attn_kernel
    q_spec = lambda off: pl.BlockSpec((1, 1, LANES, SEQ), lambda b, p: (b, 2 * p + off, 0, 0))
    k_spec = lambda off: pl.BlockSpec((1, 1, SEQ, LANES), lambda b, p: (b, k_head(2 * p + off), 0, 0))
    v_spec = lambda off: pl.BlockSpec((1, 1, VT_ROWS, SEQ), lambda b, p: (b, v_head(2 * p + off), 0, 0))
    return pl.pallas_call(
        kernel,
        grid=(BATCH, n_heads // 2),
        in_specs=[q_spec(0), q_spec(1), k_spec(0), k_spec(1), v_spec(0), v_spec(1)],
        out_specs=pl.BlockSpec((1, SEQ, LANES), lambda b, p: (b, 0, p)),
        out_shape=jax.ShapeDtypeStruct((BATCH, SEQ, n_heads * HEAD_DIM), BF16),
        scratch_shapes=([pltpu.VMEM((SEQ, TQ_SUB), F32)] * 4 + [pltpu.VMEM((SEQ, TQ_SUB), BF16)] * 4
                        + [pltpu.VMEM((1, TQ_SUB), F32)] * 4),
        compiler_params=pltpu.CompilerParams(
            dimension_semantics=("arbitrary", "arbitrary"), vmem_limit_bytes=VMEM_LIMIT),
        name=name,
    )(qt, qt, k, k, vt, vt)


def _win_kernel(sink_ref, q1_ref, q2_ref, k_ref, v_ref, o_ref, *, scale):
    pair = pl.program_id(1)
    nb = SEQ // WINDOW
    sink1 = sink_ref[pair]
    sink2 = sink_ref[HC // 2 + pair]
    low = _lane_is_low((WINDOW, LANES))
    rel = (lax.broadcasted_iota(jnp.int32, (WINDOW, 3 * WINDOW), 0)
           - lax.broadcasted_iota(jnp.int32, (WINDOW, 3 * WINDOW), 1))

    for blk in range(WIN_BLOCKS):
        n = pl.program_id(2) * WIN_BLOCKS + blk
        start = pl.multiple_of(jnp.clip(n - 1, 0, nb - 3) * WINDOW, WINDOW)
        k = k_ref[0, 0, pl.ds(start, 3 * WINDOW), :]
        v = v_ref[0, 0, pl.ds(start, 3 * WINDOW), :]
        band = jnp.abs(rel + (n * WINDOW - start)) <= WINDOW

        def one_head(q, sink):
            s = lax.dot_general(q, k, (((1,), (1,)), ((), ())), preferred_element_type=F32) * scale
            s = jnp.where(band, s, NEG_BIG)
            m = jnp.maximum(jnp.max(s, axis=-1, keepdims=True), sink)
            p = jnp.exp(s - m)
            acc = jnp.dot(p.astype(BF16), v, preferred_element_type=F32)
            return acc[:, :LANES] / (acc[:, LANES:] + jnp.exp(sink - m))

        rows = slice(blk * WINDOW, (blk + 1) * WINDOW)
        o1 = one_head(q1_ref[0, 0, rows, :], sink1)
        o2 = one_head(q2_ref[0, 0, rows, :], sink2)
        o_ref[0, rows, :] = jnp.where(low, o1, o2).astype(o_ref.dtype)


def _window_attention(sink, q, k, vext):
    kernel = functools.partial(_win_kernel, scale=HEAD_DIM ** -0.5)
    n_pairs = HC // 2
    return pl.pallas_call(
        kernel,
        grid=(BATCH, n_pairs, SEQ // (WINDOW * WIN_BLOCKS)),
        in_specs=[
            pl.BlockSpec(memory_space=pltpu.SMEM),
            pl.BlockSpec((1, 1, WINDOW * WIN_BLOCKS, LANES), lambda b, p, i: (b, p, i, 0)),
            pl.BlockSpec((1, 1, WINDOW * WIN_BLOCKS, LANES), lambda b, p, i: (b, n_pairs + p, i, 0)),
            pl.BlockSpec((1, 1, SEQ, LANES), lambda b, p, i: (b, 0, 0, 0)),
            pl.BlockSpec((1, 1, SEQ, 2 * LANES), lambda b, p, i: (b, 0, 0, 0)),
        ],
        out_specs=pl.BlockSpec((1, WINDOW * WIN_BLOCKS, LANES), lambda b, p, i: (b, i, p)),
        out_shape=jax.ShapeDtypeStruct((BATCH, SEQ, n_pairs * LANES), BF16),
        compiler_params=pltpu.CompilerParams(
            dimension_semantics=("arbitrary", "arbitrary", "arbitrary"), vmem_limit_bytes=VMEM_LIMIT),
        name="win_attn",
    )(sink, q, q, k, vext)


def _out_proj_kernel(x_ref, oa_ref, ob_ref, oc_ref, wa_ref, wb_ref, wc_ref, g_ref, rwt_ref, rb_ref,
                     x1_ref, hm_ref, idx_ref, gate_ref, rank_ref, cnt_ref, run_ref):
    @pl.when(pl.program_id(0) == 0)
    def _():
        run_ref[...] = jnp.zeros_like(run_ref)

    x1 = (x_ref[...]
          + jnp.dot(oa_ref[...], wa_ref[...], preferred_element_type=F32)
          + jnp.dot(ob_ref[...], wb_ref[...], preferred_element_type=F32)
          + jnp.dot(oc_ref[...], wc_ref[...], preferred_element_type=F32))
    x1_ref[...] = x1
    hm = _rms(x1, g_ref[...])
    _store_token_tiles(hm_ref, hm)
    tm = hm.shape[0]

    logits = lax.dot_general(rwt_ref[...], hm, (((1,), (1,)), ((), ())),
                             precision=lax.Precision.HIGHEST, preferred_element_type=F32) + rb_ref[...]
    eidx = lax.broadcasted_iota(jnp.int32, (N_EXPERTS, tm), 0)
    vals, sels, hots = [], [], []
    cur = logits
    for _ in range(TOP_K):
        mx = jnp.max(cur, axis=0, keepdims=True)
        sel = jnp.min(jnp.where(cur == mx, eidx, N_EXPERTS), axis=0, keepdims=True)
        hot = eidx == sel
        vals.append(mx)
        sels.append(sel)
        hots.append(hot)
        cur = jnp.where(hot, -jnp.inf, cur)
    exps = [jnp.exp(v - vals[0]) for v in vals]
    denom = exps[0] + exps[1] + exps[2] + exps[3]
    gates = jnp.concatenate([e / denom for e in exps] + [jnp.zeros((LANES - TOP_K, tm), F32)], axis=0)
    gate_ref[...] = gates.T
    idx_ref[...] = jnp.concatenate(sels, axis=0)

    hot_all = jnp.zeros((N_EXPERTS, tm), F32)
    for hot in hots:
        hot_all = hot_all + jnp.where(hot, 1.0, 0.0)
    row = lax.broadcasted_iota(jnp.int32, (tm, tm), 0)
    col = lax.broadcasted_iota(jnp.int32, (tm, tm), 1)
    before = jnp.where(row < col, 1.0, 0.0).astype(BF16)
    rank_full = jnp.dot(hot_all.astype(BF16), before, preferred_element_type=F32) + run_ref[...]
    ranks = [jnp.sum(jnp.where(hot, rank_full, 0.0), axis=0, keepdims=True) for hot in hots]
    rank_ref[...] = jnp.concatenate(ranks, axis=0).astype(jnp.int32)
    run_ref[...] = run_ref[...] + jnp.sum(hot_all, axis=1, keepdims=True)
    cnt_ref[...] = jnp.broadcast_to(run_ref[...], (N_EXPERTS, LANES)).astype(jnp.int32)


def _out_proj(x, oa, ob, oc, wa, wb, wc, g, rwt, rb):
    tm = TM_OUT
    const = lambda i: (0, 0)
    row_blk = lambda w: pl.BlockSpec((tm, w), lambda i: (i, 0))
    col_blk = pl.BlockSpec((TOP_K, tm), lambda i: (0, i))
    return pl.pallas_call(
        _out_proj_kernel,
        grid=(TOKENS // tm,),
        in_specs=[
            row_blk(D_MODEL), row_blk(oa.shape[1]), row_blk(ob.shape[1]), row_blk(oc.shape[1]),
            pl.BlockSpec(wa.shape, const), pl.BlockSpec(wb.shape, const), pl.BlockSpec(wc.shape, const),
            pl.BlockSpec((1, D_MODEL), const),
            pl.BlockSpec((N_EXPERTS, D_MODEL), const),
            pl.BlockSpec((N_EXPERTS, 1), const),
        ],
        out_specs=[row_blk(D_MODEL), _token_tile_spec(tm, lambda i: i), col_blk, row_blk(LANES), col_blk,
                   pl.BlockSpec((N_EXPERTS, LANES), const)],
        out_shape=[
            jax.ShapeDtypeStruct((TOKENS, D_MODEL), F32),
            jax.ShapeDtypeStruct((TOKENS * ROW_TILES, LANES), F32),
            jax.ShapeDtypeStruct((TOP_K, TOKENS), jnp.int32),
            jax.ShapeDtypeStruct((TOKENS, LANES), F32),
            jax.ShapeDtypeStruct((TOP_K, TOKENS), jnp.int32),
            jax.ShapeDtypeStruct((N_EXPERTS, LANES), jnp.int32),
        ],
        scratch_shapes=[pltpu.VMEM((N_EXPERTS, 1), F32)],
        compiler_params=pltpu.CompilerParams(
            dimension_semantics=("arbitrary",), vmem_limit_bytes=VMEM_LIMIT),
        name="out_proj",
    )(x, oa, ob, oc, wa, wb, wc, g, rwt, rb)


def _issue_row_copies(pos_smem, tm, copy):
    for t in range(tm):
        for k in range(TOP_K):
            copy(t, k, pos_smem[k * (tm // LANES) + t // LANES, t % LANES])


def _issue_row_copies_looped(pos_smem, tm, copy):
    for chunk in range(tm // LANES):
        def issue(j, carry, chunk=chunk):
            for k in range(TOP_K):
                copy(chunk * LANES + j, k, pos_smem[k * (tm // LANES) + chunk, j])
            return carry
        lax.fori_loop(0, LANES, issue, 0, unroll=16)


def _dispatch_kernel(pad_start_ref, pad_count_ref, pos_hbm, hm_ref, xs_ref,
                     pos_smem, zero_ref, pos_sem, row_sems, pad_sem):
    i = pl.program_id(0)
    tm = hm_ref.shape[0] // ROW_TILES
    load = pltpu.make_async_copy(pos_hbm.at[i], pos_smem, pos_sem)
    load.start()

    @pl.when(i == 0)
    def _():
        zero_ref[...] = jnp.zeros_like(zero_ref)
        zero_row = _token_tile(zero_ref, 0)
        for e in range(N_EXPERTS):
            start = pad_start_ref[e]

            def fill(r, carry, start=start):
                pltpu.make_async_copy(zero_row, _token_tile(xs_ref, start + r), pad_sem).start()
                return carry

            def drain(r, carry):
                pltpu.make_async_copy(zero_row, _token_tile(xs_ref, 0), pad_sem).wait()
                return carry

            lax.fori_loop(0, pad_count_ref[e], fill, 0)
            lax.fori_loop(0, pad_count_ref[e], drain, 0)

        tail_start = pad_start_ref[N_EXPERTS]
        chunk = ZERO_ROWS * ROW_TILES

        def fill_tail(r, carry):
            dst = xs_ref.at[pl.ds(pl.multiple_of((tail_start + r * ZERO_ROWS) * ROW_TILES, chunk), chunk)]
            pltpu.make_async_copy(zero_ref, dst, pad_sem).start()
            return carry

        def drain_tail(r, carry):
            pltpu.make_async_copy(zero_ref, xs_ref.at[pl.ds(0, chunk)], pad_sem).wait()
            return carry

        lax.fori_loop(0, pad_count_ref[N_EXPERTS], fill_tail, 0)
        lax.fori_loop(0, pad_count_ref[N_EXPERTS], drain_tail, 0)

    load.wait()

    def copy(t, k, pos):
        pltpu.make_async_copy(_token_tile(hm_ref, t), _token_tile(xs_ref, pos),
                              row_sems.at[k]).start(priority=k % 2)

    _issue_row_copies_looped(pos_smem, tm, copy)
    for k in range(TOP_K):
        pltpu.make_async_copy(hm_ref, _token_tiles(xs_ref, tm), row_sems.at[k]).wait()


def _dispatch(pad_start, pad_count, pos_tiles, hm):
    tm = TM_DISPATCH
    grid_spec = pltpu.PrefetchScalarGridSpec(
        num_scalar_prefetch=2,
        grid=(TOKENS // tm,),
        in_specs=[
            pl.BlockSpec(memory_space=pl.ANY),
            _token_tile_spec(tm, lambda i, ps, pc: i),
        ],
        out_specs=pl.BlockSpec(memory_space=pl.ANY),
        scratch_shapes=[
            pltpu.SMEM((TOP_K * tm // LANES, LANES), jnp.int32),
            pltpu.VMEM((ZERO_ROWS * ROW_TILES, LANES), F32),
            pltpu.SemaphoreType.DMA,
            pltpu.SemaphoreType.DMA((TOP_K,)),
            pltpu.SemaphoreType.DMA,
        ],
    )
    return pl.pallas_call(
        _dispatch_kernel,
        grid_spec=grid_spec,
        out_shape=jax.ShapeDtypeStruct((N_SORTED_ROWS * ROW_TILES, LANES), F32),
        compiler_params=pltpu.CompilerParams(
            dimension_semantics=("arbitrary",), vmem_limit_bytes=VMEM_LIMIT, has_side_effects=True),
        name="dispatch",
    )(pad_start, pad_count, pos_tiles, hm)


def _experts_kernel(te_ref, ts_ref, nr_ref, nx_ref, xs_ref, wup_hbm, bup_ref, wdn_hbm, bdn_ref, ys_ref,
                    wup_f32_ref, wdn_f32_ref, wup_bf_ref, wdn_bf_ref, w_sems, *, layer):
    del ts_ref
    i = pl.program_id(0)
    n_rows = nr_ref[i]
    expert = te_ref[i]

    def weight_copies(e):
        return (pltpu.make_async_copy(wup_hbm.at[layer, e], wup_f32_ref, w_sems.at[0]),
                pltpu.make_async_copy(wdn_hbm.at[layer, e], wdn_f32_ref, w_sems.at[1]))

    @pl.when(i == 0)
    def _():
        for cp in weight_copies(expert):
            cp.start()

    @pl.when(jnp.logical_or(i == 0, expert != te_ref[jnp.maximum(i - 1, 0)]))
    def _():
        for cp in weight_copies(expert):
            cp.wait()
        wup_bf_ref[...] = wup_f32_ref[...].astype(BF16)
        wdn_bf_ref[...] = wdn_f32_ref[...].astype(BF16)

        @pl.when(nx_ref[i] >= 0)
        def _():
            for cp in weight_copies(nx_ref[i]):
                cp.start()

    @pl.when(n_rows > 0)
    def _():
        x = _load_token_tiles(xs_ref).astype(BF16)
        gu = jnp.dot(x, wup_bf_ref[...], preferred_element_type=F32) + bup_ref[0, 0]
        x_glu = jnp.minimum(gu[:, :D_FF], SWIGLU_LIMIT)
        x_lin = jnp.clip(gu[:, D_FF:], -SWIGLU_LIMIT, SWIGLU_LIMIT)
        act = x_glu * jax.nn.sigmoid(SWIGLU_ALPHA * x_glu) * (x_lin + 1.0)
        _store_token_tiles(
            ys_ref, jnp.dot(act.astype(BF16), wdn_bf_ref[...], preferred_element_type=F32) + bdn_ref[0, 0])

    @pl.when(n_rows == 0)
    def _():
        ys_ref[...] = jnp.zeros_like(ys_ref)


def _experts(layer, tile_expert, tile_src, tile_rows, next_expert, xs, wup, bup, wdn, bdn):
    tm = TM_EXPERT
    per_expert = lambda r, c: pl.BlockSpec((1, 1, r, c), lambda i, te, ts, nr, nx: (layer, te[i], 0, 0))
    grid_spec = pltpu.PrefetchScalarGridSpec(
        num_scalar_prefetch=4,
        grid=(N_EXPERT_TILES,),
        in_specs=[
            _token_tile_spec(tm, lambda i, te, ts, nr, nx: ts[i]),
            pl.BlockSpec(memory_space=pl.ANY), per_expert(1, 2 * D_FF),
            pl.BlockSpec(memory_space=pl.ANY), per_expert(1, D_MODEL),
        ],
        out_specs=_token_tile_spec(tm, lambda i, te, ts, nr, nx: i),
        scratch_shapes=[pltpu.VMEM((D_MODEL, 2 * D_FF), F32), pltpu.VMEM((D_FF, D_MODEL), F32),
                        pltpu.VMEM((D_MODEL, 2 * D_FF), BF16), pltpu.VMEM((D_FF, D_MODEL), BF16),
                        pltpu.SemaphoreType.DMA((2,))],
    )
    return pl.pallas_call(
        functools.partial(_experts_kernel, layer=layer),
        grid_spec=grid_spec,
        out_shape=jax.ShapeDtypeStruct((N_SORTED_ROWS * ROW_TILES, LANES), F32),
        compiler_params=pltpu.CompilerParams(
            dimension_semantics=("arbitrary",), vmem_limit_bytes=VMEM_LIMIT),
        name="experts",
    )(tile_expert, tile_src, tile_rows, next_expert, xs, wup, bup, wdn, bdn)


def _combine_kernel(pos_hbm, ys_hbm, x1_ref, gate_ref, p_ref, g_ref, wg_ref, wp_ref, gf_ref,
                    o_ref, pos_smem, rows_ref, pos_sem, row_sems, *, apply_final_norm):
    i = pl.program_id(0)
    tm = x1_ref.shape[0]
    slot = lax.rem(i, 2)

    def start_gather(tile, dst_slot):
        load = pltpu.make_async_copy(pos_hbm.at[tile], pos_smem, pos_sem)
        load.start()
        load.wait()

        def copy(t, k, pos):
            pltpu.make_async_copy(_token_tile(ys_hbm, pos), _token_tile(rows_ref.at[dst_slot, k], t),
                                  row_sems.at[dst_slot, k]).start(priority=k % 2)

        _issue_row_copies(pos_smem, tm, copy)

    @pl.when(i == 0)
    def _():
        start_gather(i, slot)

    @pl.when(i + 1 < pl.num_programs(0))
    def _():
        start_gather(i + 1, 1 - slot)

    for k in range(TOP_K):
        pltpu.make_async_copy(_token_tiles(ys_hbm, tm), rows_ref.at[slot, k], row_sems.at[slot, k]).wait()

    x2 = x1_ref[...]
    for k in range(TOP_K):
        x2 = x2 + _load_token_tiles(rows_ref.at[slot, k]) * gate_ref[:, k:k + 1]
    hp = _rms(x2, g_ref[...]).astype(BF16)
    gate = jax.nn.sigmoid(jnp.dot(hp, wg_ref[...], preferred_element_type=F32))
    pe = jnp.dot(p_ref[...].astype(BF16), wp_ref[...], preferred_element_type=F32)
    x3 = x2 + pe * gate
    if apply_final_norm:
        x3 = _rms(x3, gf_ref[...])
    o_ref[...] = x3


def _combine(pos_tiles, ys, x1, gates_t, p, g, wg, wp, gf, apply_final_norm):
    tm = TM_COMBINE
    const = lambda i: (0, 0)
    row_blk = lambda w: pl.BlockSpec((tm, w), lambda i: (i, 0))
    kernel = functools.partial(_combine_kernel, apply_final_norm=apply_final_norm)
    return pl.pallas_call(
        kernel,
        grid=(TOKENS // tm,),
        in_specs=[
            pl.BlockSpec(memory_space=pl.ANY),
            pl.BlockSpec(memory_space=pl.ANY),
            row_blk(D_MODEL), row_blk(LANES), row_blk(PLE_DIM),
            pl.BlockSpec((1, D_MODEL), const),
            pl.BlockSpec((D_MODEL, D_MODEL), const),
            pl.BlockSpec((PLE_DIM, D_MODEL), const),
            pl.BlockSpec((1, D_MODEL), const),
        ],
        out_specs=row_blk(D_MODEL),
        out_shape=jax.ShapeDtypeStruct((TOKENS, D_MODEL), F32),
        scratch_shapes=[
            pltpu.SMEM((TOP_K * tm // LANES, LANES), jnp.int32),
            pltpu.VMEM((2, TOP_K, tm * ROW_TILES, LANES), F32),
            pltpu.SemaphoreType.DMA,
            pltpu.SemaphoreType.DMA((2, TOP_K)),
        ],
        compiler_params=pltpu.CompilerParams(
            dimension_semantics=("arbitrary",), vmem_limit_bytes=VMEM_LIMIT),
        name="combine",
    )(pos_tiles, ys, x1, gates_t, p, g, wg, wp, gf)


def _in_proj_columns():
    src = np.full((Z_COLS,), -1, np.int64)
    src[Z_CQ:Z_CQ + Q_LORA] = np.arange(Q_LORA)
    src[Z_CKV:Z_CKV + KV_LORA] = Q_LORA + np.arange(KV_LORA)
    src[Z_KROPE + NOPE_A:Z_KROPE + NOPE_A + ROPE_A] = Q_LORA + KV_LORA + np.arange(ROPE_A)
    b0 = A_COLS
    for j in range(HB // 2):
        src[Z_BQ + j * LANES:Z_BQ + j * LANES + HALF] = b0 + j * HEAD_DIM + np.arange(HEAD_DIM)
        src[Z_BQ + j * LANES + HALF:Z_BQ + (j + 1) * LANES] = b0 + (HB // 2 + j) * HEAD_DIM + np.arange(HEAD_DIM)
    src[Z_BK:Z_BK + LANES] = b0 + HB * HEAD_DIM + np.arange(LANES)
    src[Z_BV:Z_BV + LANES] = b0 + (HB + KVB) * HEAD_DIM + np.arange(LANES)
    c0 = A_COLS + B_COLS
    for j in range(HC // 2):
        src[Z_CQS + j * LANES:Z_CQS + j * LANES + HALF] = c0 + j * HEAD_DIM + np.arange(HEAD_DIM)
        src[Z_CQS + j * LANES + HALF:Z_CQS + (j + 1) * LANES] = c0 + (HC // 2 + j) * HEAD_DIM + np.arange(HEAD_DIM)
    src[Z_CK:Z_CK + LANES] = c0 + HC * HEAD_DIM + np.arange(LANES)
    src[Z_CV:Z_CV + LANES] = c0 + (HC + KVC) * HEAD_DIM + np.arange(LANES)
    return src


def _wq_columns():
    src = np.full((HA * LANES,), -1, np.int64)
    dq = NOPE_A + ROPE_A
    for h in range(HA):
        src[h * LANES:h * LANES + dq] = h * dq + np.arange(dq)
    return src


def _wkv_columns():
    src = np.full((HA * LANES + (HA // 2) * LANES,), -1, np.int64)
    dkv = NOPE_A + V_A
    for h in range(HA):
        src[h * LANES:h * LANES + NOPE_A] = h * dkv + np.arange(NOPE_A)
        v0 = HA * LANES + (h // 2) * LANES + (h % 2) * HALF
        src[v0:v0 + V_A] = h * dkv + NOPE_A + np.arange(V_A)
    return src


def _paired_rows(base, n_heads):
    rows = []
    for j in range(n_heads // 2):
        rows.append(base + j * HEAD_DIM + np.arange(HEAD_DIM))
        rows.append(base + (n_heads // 2 + j) * HEAD_DIM + np.arange(HEAD_DIM))
    return np.concatenate(rows)


def _take_cols(w, src):
    cols = jnp.take(w, jnp.asarray(np.maximum(src, 0)), axis=1)
    return jnp.where(jnp.asarray(src >= 0)[None, :], cols, 0.0)


def _rope_tables():
    f32 = np.float32

    def cos_sin(pos, dim):
        inv = (f32(1.0) / (f32(ROPE_THETA) ** (np.arange(0, dim, 2, dtype=f32) / f32(dim)))).astype(f32)
        ang = pos.astype(f32)[:, None] * inv[None, :]
        return np.cos(ang).astype(f32), np.sin(ang).astype(f32)

    pos = np.arange(SEQ, dtype=np.int32)
    rows = pos // GRID_W
    cols = pos % GRID_W
    zeros = lambda w: np.zeros((SEQ, w), f32)
    ones = lambda w: np.ones((SEQ, w), f32)

    cos_a, sin_a = cos_sin(pos, ROPE_A)
    c_a = np.concatenate([ones(NOPE_A), cos_a, cos_a, ones(32)], axis=1)
    up_a = np.concatenate([zeros(NOPE_A), -sin_a, zeros(16), zeros(32)], axis=1)
    dn_a = np.concatenate([zeros(NOPE_A), zeros(16), sin_a, zeros(32)], axis=1)

    cos_r, sin_r = cos_sin(rows, HEAD_DIM // 2)
    cos_w, sin_w = cos_sin(cols, HEAD_DIM // 2)
    z16 = zeros(16)
    c_b = np.concatenate([cos_r, cos_r, cos_w, cos_w] * 2, axis=1)
    up_b = np.concatenate([-sin_r, z16, -sin_w, z16] * 2, axis=1)
    dn_b = np.concatenate([z16, sin_r, z16, sin_w] * 2, axis=1)

    cos_c, sin_c = cos_sin(pos, HEAD_DIM)
    z32 = zeros(32)
    c_c = np.concatenate([cos_c, cos_c] * 2, axis=1)
    up_c = np.concatenate([-sin_c, z32] * 2, axis=1)
    dn_c = np.concatenate([z32, sin_c] * 2, axis=1)
    return jnp.asarray(np.concatenate([c_a, up_a, dn_a, c_b, up_b, dn_b, c_c, up_c, dn_c], axis=1))


def _routing_tables(idx, rank, counts):
    tiles_per_expert = (counts + TM_EXPERT - 1) // TM_EXPERT
    tile_end = jnp.cumsum(tiles_per_expert)
    tile_start = tile_end - tiles_per_expert
    row_start = tile_start * TM_EXPERT
    experts = jnp.arange(N_EXPERTS, dtype=jnp.int32)
    pos = jnp.sum(jnp.where(idx[..., None] == experts, row_start, 0), axis=-1) + rank
    n_valid = tile_end[-1]
    tile_ids = jnp.arange(N_EXPERT_TILES, dtype=jnp.int32)
    tile_src = jnp.minimum(tile_ids, n_valid - 1)
    tile_expert = jnp.minimum(jnp.sum(tile_src[:, None] >= tile_end[None, :], axis=-1), N_EXPERTS - 1)
    hot = tile_expert[:, None] == experts
    rows_left = jnp.sum(jnp.where(hot, counts - (tile_src[:, None] - tile_start) * TM_EXPERT, 0), axis=-1)
    tile_rows = jnp.where(tile_ids < n_valid, jnp.clip(rows_left, 0, TM_EXPERT), 0)
    tail_start = n_valid * TM_EXPERT
    pad_start = jnp.concatenate([row_start + counts, tail_start[None]])
    pad_count = jnp.concatenate([tiles_per_expert * TM_EXPERT - counts,
                                 ((N_SORTED_ROWS - tail_start) // ZERO_ROWS)[None]])
    group_end = jnp.sum(jnp.where(hot, tile_end, 0), axis=-1)
    following = jnp.minimum(jnp.sum(group_end[:, None] >= tile_end[None, :], axis=-1), N_EXPERTS - 1)
    next_expert = jnp.where(group_end < n_valid, following, -1)
    tables = (tile_expert, tile_src, tile_rows, next_expert, pad_start, pad_count)
    return (pos,) + tuple(t.astype(jnp.int32) for t in tables)


def _pos_tiles(pos, tm):
    return pos.reshape(TOP_K, TOKENS // tm, tm).transpose(1, 0, 2).reshape(TOKENS // tm, TOP_K * tm // LANES, LANES)


def kernel(x, p, attn_norm, w_in, mla_q_norm, mla_wq_up, mla_kv_norm, mla_wkv_up, gqa_q_norm, gqa_k_norm,
           swa_sink, w_out, moe_norm, router_w, router_b, w_up, b_up, w_down, b_down, ple_norm, w_ple,
           w_ple_gate, final_norm):
    tabs = _rope_tables()
    in_cols, wq_cols, wkv_cols = _in_proj_columns(), _wq_columns(), _wkv_columns()
    rows_c = _paired_rows(HA * V_A + HB * HEAD_DIM, HC)

    xf = x.reshape(TOKENS, D_MODEL)
    for i in range(DEPTH):
        win = _take_cols(w_in[i], in_cols).astype(BF16)
        wq = _take_cols(mla_wq_up[i], wq_cols).astype(BF16)
        wkv = _take_cols(mla_wkv_up[i], wkv_cols).astype(BF16)
        gbq = jnp.tile(gqa_q_norm[i], 2)[None, :]
        gbk = jnp.tile(gqa_k_norm[i], 2)[None, :]
        qa, ka, va, qb, kb, vb, qc, kc, vc = _in_proj(
            xf.reshape(BATCH, SEQ, D_MODEL), attn_norm[i][None, :], win, mla_q_norm[i][None, :], wq,
            mla_kv_norm[i][None, :], wkv, gbq, gbk, tabs)

        oa = _dense_attention(qa, ka, va, HA, lambda h: h, lambda h: h, "attn_a")
        ob = _dense_attention(qb, kb, vb, HB, lambda h: h * 0, lambda h: h // (HB // KVB), "attn_b")
        oc = _window_attention(swa_sink[i], qc, kc, vc)

        wo = w_out[i]
        x1, hm, idx, gates_t, rank, cnt = _out_proj(
            xf, oa.reshape(TOKENS, -1), ob.reshape(TOKENS, -1), oc.reshape(TOKENS, -1),
            wo[:HA * V_A].astype(BF16), wo[HA * V_A:HA * V_A + HB * HEAD_DIM].astype(BF16),
            jnp.take(wo, jnp.asarray(rows_c), axis=0).astype(BF16),
            moe_norm[i][None, :], router_w[i].T, router_b[i][:, None])

        pos, tile_expert, tile_src, tile_rows, next_expert, pad_start, pad_count = _routing_tables(
            idx, rank, cnt[:, 0])
        xs = _dispatch(pad_start, pad_count, _pos_tiles(pos, TM_DISPATCH), hm)
        ys = _experts(i, tile_expert, tile_src, tile_rows, next_expert, xs, w_up, b_up[:, :, None, :],
                      w_down, b_down[:, :, None, :])
        xf = _combine(_pos_tiles(pos, TM_COMBINE), ys, x1, gates_t, p[i].reshape(TOKENS, PLE_DIM),
                      ple_norm[i][None, :], w_ple_gate[i].astype(BF16), w_ple[i].astype(BF16),
                      final_norm[None, :], apply_final_norm=(i == DEPTH - 1))
    return xf.reshape(BATCH, SEQ, D_MODEL)
```

```python
import functools
import math

import numpy as np
import jax
import jax.numpy as jnp
from jax import lax
from jax.experimental import pallas as pl
from jax.experimental.pallas import tpu as pltpu

F32 = jnp.float32
BF16 = jnp.bfloat16

D_MODEL = 1024
BATCH = 8
SEQ = 4096
DEPTH = 2
TOKENS = BATCH * SEQ
GRID_W = 64
PLE_DIM = 256
HEAD_DIM = 64
ROPE_THETA = 10000.0
EPS = 1e-6
HA, Q_LORA, KV_LORA, NOPE_A, ROPE_A, V_A = 6, 256, 128, 64, 32, 64
HB, KVB = 6, 2
HC, KVC, WINDOW = 4, 2, 128
A_COLS = Q_LORA + KV_LORA + ROPE_A
B_COLS = (HB + 2 * KVB) * HEAD_DIM
C_COLS = (HC + 2 * KVC) * HEAD_DIM
N_EXPERTS = 32
TOP_K = 4
D_FF = D_MODEL
SWIGLU_LIMIT = 7.0
SWIGLU_ALPHA = 1.702

LANES = 128
HALF = LANES // 2
BF16_SUBLANES = 16
ROW_TILES = D_MODEL // LANES

VT_ROWS = HEAD_DIM + BF16_SUBLANES

TM_IN = 512
TQ_SUB = 256
WIN_BLOCKS = 8
TM_OUT = 1024
TM_DISPATCH = 1024
TM_EXPERT = 512
TM_COMBINE = 512
ZERO_ROWS = 64
N_EXPERT_TILES = TOKENS * TOP_K // TM_EXPERT + N_EXPERTS
N_SORTED_ROWS = N_EXPERT_TILES * TM_EXPERT
VMEM_LIMIT = 52 * 1024 * 1024

Z_CQ = 0
Z_CKV = Q_LORA
Z_KROPE = Z_CKV + KV_LORA
Z_BQ = Z_KROPE + LANES
Z_BK = Z_BQ + 3 * LANES
Z_BV = Z_BK + LANES
Z_CQS = Z_BV + LANES
Z_CK = Z_CQS + 2 * LANES
Z_CV = Z_CK + LANES
Z_COLS = Z_CV + LANES

NEG_BIG = -1e30
LOG2E = math.log2(math.e)
QSCALE_A = (NOPE_A + ROPE_A) ** -0.5 * LOG2E
QSCALE_B = HEAD_DIM ** -0.5 * LOG2E


def _rms(x, g):
    return x * lax.rsqrt(jnp.mean(x * x, axis=-1, keepdims=True) + EPS) * g


def _lane_is_low(shape):
    return lax.broadcasted_iota(jnp.int32, shape, len(shape) - 1) < HALF


def _rms_per_half(xs, g):
    low = _lane_is_low(xs.shape)
    x2 = xs * xs
    s_lo = jnp.sum(jnp.where(low, x2, 0.0), axis=-1, keepdims=True)
    s_hi = jnp.sum(jnp.where(low, 0.0, x2), axis=-1, keepdims=True)
    ms = jnp.where(low, s_lo, s_hi) * (1.0 / HEAD_DIM)
    return xs * lax.rsqrt(ms + EPS) * g


def _token_tile_spec(tm, index):
    return pl.BlockSpec((tm * ROW_TILES, LANES), lambda i, *prefetch: (index(i, *prefetch), 0))


def _token_tile(ref, row):
    return ref.at[pl.ds(pl.multiple_of(row * ROW_TILES, ROW_TILES), ROW_TILES)]


def _token_tiles(ref, n_rows):
    return ref.at[pl.ds(0, n_rows * ROW_TILES)]


def _store_token_tiles(ref, x):
    tm = x.shape[0]
    for c in range(ROW_TILES):
        ref[pl.ds(c, tm, stride=ROW_TILES), :] = x[:, c * LANES:(c + 1) * LANES]


def _load_token_tiles(ref):
    tm = ref.shape[0] // ROW_TILES
    return jnp.concatenate([ref[pl.ds(c, tm, stride=ROW_TILES), :] for c in range(ROW_TILES)], axis=1)


def _in_proj_kernel(x_ref, g_ref, win_ref, gq_ref, wq_ref, gkv_ref, wkv_ref, gbq_ref, gbk_ref, tab_ref,
                    qa_ref, ka_ref, va_ref, qb_ref, kb_ref, vb_ref, qc_ref, kc_ref, vc_ref):
    h = _rms(x_ref[0], g_ref[...])
    z = jnp.dot(h.astype(BF16), win_ref[...], preferred_element_type=F32)
    tm = z.shape[0]
    ones = jnp.ones((tm, LANES), BF16)
    ones_t = jnp.ones((VT_ROWS - HEAD_DIM, tm), BF16)
    low = _lane_is_low((tm, LANES))
    top = lax.broadcasted_iota(jnp.int32, (LANES, tm), 0) < HALF

    def rope(xs, table, shift):
        base = table * 3 * LANES
        c = tab_ref[:, base:base + LANES]
        s_up = tab_ref[:, base + LANES:base + 2 * LANES]
        s_dn = tab_ref[:, base + 2 * LANES:base + 3 * LANES]
        return xs * c + pltpu.roll(xs, LANES - shift, 1) * s_up + pltpu.roll(xs, shift, 1) * s_dn

    c_q = _rms(z[:, Z_CQ:Z_CQ + Q_LORA], gq_ref[...])
    q = jnp.dot(c_q.astype(BF16), wq_ref[...], preferred_element_type=F32)
    c_kv = _rms(z[:, Z_CKV:Z_CKV + KV_LORA], gkv_ref[...])
    kv = jnp.dot(c_kv.astype(BF16), wkv_ref[...], preferred_element_type=F32)
    k_rope = rope(z[:, Z_KROPE:Z_KROPE + LANES], 0, ROPE_A // 2)
    for hd in range(HA):
        qa_ref[0, hd] = (rope(q[:, hd * LANES:(hd + 1) * LANES], 0, ROPE_A // 2) * QSCALE_A).T.astype(BF16)
        ka_ref[0, hd] = (kv[:, hd * LANES:(hd + 1) * LANES] + k_rope).astype(BF16)
    for j in range(HA // 2):
        vt = kv[:, (HA + j) * LANES:(HA + j + 1) * LANES].T.astype(BF16)
        for half in range(2):
            va_ref[0, 2 * j + half, 0:HEAD_DIM, :] = vt[half * HALF:(half + 1) * HALF]
            va_ref[0, 2 * j + half, HEAD_DIM:VT_ROWS, :] = ones_t

    for j in range(HB // 2):
        s = rope(_rms_per_half(z[:, Z_BQ + j * LANES:Z_BQ + (j + 1) * LANES], gbq_ref[...]), 1, HEAD_DIM // 4)
        st = (s * QSCALE_B).T
        qb_ref[0, j] = jnp.where(top, st, 0.0).astype(BF16)
        qb_ref[0, HB // 2 + j] = jnp.where(top, 0.0, st).astype(BF16)
    kb_ref[0, 0] = rope(_rms_per_half(z[:, Z_BK:Z_BK + LANES], gbk_ref[...]), 1, HEAD_DIM // 4).astype(BF16)
    vt = z[:, Z_BV:Z_BV + LANES].T.astype(BF16)
    for g in range(KVB):
        vb_ref[0, g, 0:HEAD_DIM, :] = vt[g * HALF:(g + 1) * HALF]
        vb_ref[0, g, HEAD_DIM:VT_ROWS, :] = ones_t

    for j in range(HC // 2):
        s = rope(z[:, Z_CQS + j * LANES:Z_CQS + (j + 1) * LANES], 2, HEAD_DIM // 2)
        qc_ref[0, j] = jnp.where(low, s, 0.0).astype(BF16)
        qc_ref[0, HC // 2 + j] = jnp.where(low, 0.0, s).astype(BF16)
    kc_ref[0, 0] = rope(z[:, Z_CK:Z_CK + LANES], 2, HEAD_DIM // 2).astype(BF16)
    vc_ref[0, 0, :, 0:LANES] = z[:, Z_CV:Z_CV + LANES].astype(BF16)
    vc_ref[0, 0, :, LANES:2 * LANES] = ones


def _in_proj(x3, g, win, gq, wq, gkv, wkv, gbq, gbk, tabs):
    nst = SEQ // TM_IN
    const2 = lambda b, s: (0, 0)
    head_out = lambda n, w: (jax.ShapeDtypeStruct((BATCH, n, SEQ, w), BF16),
                             pl.BlockSpec((1, n, TM_IN, w), lambda b, s: (b, 0, s, 0)))
    head_out_t = lambda n, r: (jax.ShapeDtypeStruct((BATCH, n, r, SEQ), BF16),
                               pl.BlockSpec((1, n, r, TM_IN), lambda b, s: (b, 0, 0, s)))
    outs = [head_out_t(HA, LANES), head_out(HA, LANES), head_out_t(HA, VT_ROWS),
            head_out_t(HB, LANES), head_out(1, LANES), head_out_t(KVB, VT_ROWS),
            head_out(HC, LANES), head_out(1, LANES), head_out(1, 2 * LANES)]
    return pl.pallas_call(
        _in_proj_kernel,
        grid=(BATCH, nst),
        in_specs=[
            pl.BlockSpec((1, TM_IN, D_MODEL), lambda b, s: (b, s, 0)),
            pl.BlockSpec((1, D_MODEL), const2),
            pl.BlockSpec((D_MODEL, Z_COLS), const2),
            pl.BlockSpec((1, Q_LORA), const2),
            pl.BlockSpec((Q_LORA, HA * LANES), const2),
            pl.BlockSpec((1, KV_LORA), const2),
            pl.BlockSpec((KV_LORA, HA * LANES + (HA // 2) * LANES), const2),
            pl.BlockSpec((1, LANES), const2),
            pl.BlockSpec((1, LANES), const2),
            pl.BlockSpec((TM_IN, 9 * LANES), lambda b, s: (s, 0)),
        ],
        out_specs=[o[1] for o in outs],
        out_shape=[o[0] for o in outs],
        compiler_params=pltpu.CompilerParams(
            dimension_semantics=("arbitrary", "arbitrary"), vmem_limit_bytes=VMEM_LIMIT),
        name="in_proj",
    )(x3, g, win, gq, wq, gkv, wkv, gbq, gbk, tabs)


def _attn_kernel(q1_ref, q2_ref, k1_ref, k2_ref, v1_ref, v2_ref, o_ref,
                 st1a_ref, st1b_ref, st2a_ref, st2b_ref, pt1a_ref, pt1b_ref, pt2a_ref, pt2b_ref,
                 m1a_ref, m1b_ref, m2a_ref, m2b_ref):
    st_refs = ((st1a_ref, st1b_ref), (st2a_ref, st2b_ref))
    pt_refs = ((pt1a_ref, pt1b_ref), (pt2a_ref, pt2b_ref))
    m_refs = ((m1a_ref, m1b_ref), (m2a_ref, m2b_ref))
    q_refs, k_refs, v_refs = (q1_ref, q2_ref), (k1_ref, k2_ref), (v1_ref, v2_ref)
    n_sub = SEQ // TQ_SUB

    def scores(u, slot):
        cols = pl.ds(pl.multiple_of(u * TQ_SUB, TQ_SUB), TQ_SUB)
        for h in range(2):
            qt = q_refs[h][0, 0, :, cols]
            st = jnp.dot(k_refs[h][0, 0], qt, preferred_element_type=F32)
            st_refs[h][slot][...] = st
            m_refs[h][slot][...] = jnp.max(st, axis=0, keepdims=True)

    def probs(slot):
        for h in range(2):
            pt_refs[h][slot][...] = jnp.exp2((st_refs[h][slot][...] - m_refs[h][slot][...]).astype(BF16))

    def values(u, slot):
        outs = []
        for h in range(2):
            acc = jnp.dot(v_refs[h][0, 0], pt_refs[h][slot][...], preferred_element_type=F32)
            outs.append(acc[:HEAD_DIM] / acc[HEAD_DIM:HEAD_DIM + 1])
        rows = pl.ds(pl.multiple_of(u * TQ_SUB, TQ_SUB), TQ_SUB)
        o_ref[0, rows, :] = jnp.concatenate(outs, axis=0).T.astype(o_ref.dtype)

    scores(0, 0)
    scores(1, 1)
    probs(0)

    def body(g, carry):
        u = 2 * g
        scores(u, 0)
        probs(1)
        values(u - 2, 0)
        scores(u + 1, 1)
        probs(0)
        values(u - 1, 1)
        return carry

    lax.fori_loop(1, n_sub // 2, body, 0)
    probs(1)
    values(n_sub - 2, 0)
    values(n_sub - 1, 1)


def _dense_attention(qt, k, vt, n_heads, k_head, v_head, name):
    kernel = _attn_kernel
    q_spec = lambda off: pl.BlockSpec((1, 1, LANES, SEQ), lambda b, p: (b, 2 * p + off, 0, 0))
    k_spec = lambda off: pl.BlockSpec((1, 1, SEQ, LANES), lambda b, p: (b, k_head(2 * p + off), 0, 0))
    v_spec = lambda off: pl.BlockSpec((1, 1, VT_ROWS, SEQ), lambda b, p: (b, v_head(2 * p + off), 0, 0))
    return pl.pallas_call(
        kernel,
        grid=(BATCH, n_heads // 2),
        in_specs=[q_spec(0), q_spec(1), k_spec(0), k_spec(1), v_spec(0), v_spec(1)],
        out_specs=pl.BlockSpec((1, SEQ, LANES), lambda b, p: (b, 0, p)),
        out_shape=jax.ShapeDtypeStruct((BATCH, SEQ, n_heads * HEAD_DIM), BF16),
        scratch_shapes=([pltpu.VMEM((SEQ, TQ_SUB), F32)] * 4 + [pltpu.VMEM((SEQ, TQ_SUB), BF16)] * 4
                        + [pltpu.VMEM((1, TQ_SUB), F32)] * 4),
        compiler_params=pltpu.CompilerParams(
            dimension_semantics=("arbitrary", "arbitrary"), vmem_limit_bytes=VMEM_LIMIT),
        name=name,
    )(qt, qt, k, k, vt, vt)


def _win_kernel(sink_ref, q1_ref, q2_ref, k_ref, v_ref, o_ref, *, scale):
    pair = pl.program_id(1)
    nb = SEQ // WINDOW
    sink1 = sink_ref[pair]
    sink2 = sink_ref[HC // 2 + pair]
    low = _lane_is_low((WINDOW, LANES))
    rel = (lax.broadcasted_iota(jnp.int32, (WINDOW, 3 * WINDOW), 0)
           - lax.broadcasted_iota(jnp.int32, (WINDOW, 3 * WINDOW), 1))

    for blk in range(WIN_BLOCKS):
        n = pl.program_id(2) * WIN_BLOCKS + blk
        start = pl.multiple_of(jnp.clip(n - 1, 0, nb - 3) * WINDOW, WINDOW)
        k = k_ref[0, 0, pl.ds(start, 3 * WINDOW), :]
        v = v_ref[0, 0, pl.ds(start, 3 * WINDOW), :]
        band = jnp.abs(rel + (n * WINDOW - start)) <= WINDOW

        def one_head(q, sink):
            s = lax.dot_general(q, k, (((1,), (1,)), ((), ())), preferred_element_type=F32) * scale
            s = jnp.where(band, s, NEG_BIG)
            m = jnp.maximum(jnp.max(s, axis=-1, keepdims=True), sink)
            p = jnp.exp(s - m)
            acc = jnp.dot(p.astype(BF16), v, preferred_element_type=F32)
            return acc[:, :LANES] / (acc[:, LANES:] + jnp.exp(sink - m))

        rows = slice(blk * WINDOW, (blk + 1) * WINDOW)
        o1 = one_head(q1_ref[0, 0, rows, :], sink1)
        o2 = one_head(q2_ref[0, 0, rows, :], sink2)
        o_ref[0, rows, :] = jnp.where(low, o1, o2).astype(o_ref.dtype)


def _window_attention(sink, q, k, vext):
    kernel = functools.partial(_win_kernel, scale=HEAD_DIM ** -0.5)
    n_pairs = HC // 2
    return pl.pallas_call(
        kernel,
        grid=(BATCH, n_pairs, SEQ // (WINDOW * WIN_BLOCKS)),
        in_specs=[
            pl.BlockSpec(memory_space=pltpu.SMEM),
            pl.BlockSpec((1, 1, WINDOW * WIN_BLOCKS, LANES), lambda b, p, i: (b, p, i, 0)),
            pl.BlockSpec((1, 1, WINDOW * WIN_BLOCKS, LANES), lambda b, p, i: (b, n_pairs + p, i, 0)),
            pl.BlockSpec((1, 1, SEQ, LANES), lambda b, p, i: (b, 0, 0, 0)),
            pl.BlockSpec((1, 1, SEQ, 2 * LANES), lambda b, p, i: (b, 0, 0, 0)),
        ],
        out_specs=pl.BlockSpec((1, WINDOW * WIN_BLOCKS, LANES), lambda b, p, i: (b, i, p)),
        out_shape=jax.ShapeDtypeStruct((BATCH, SEQ, n_pairs * LANES), BF16),
        compiler_params=pltpu.CompilerParams(
            dimension_semantics=("arbitrary", "arbitrary", "arbitrary"), vmem_limit_bytes=VMEM_LIMIT),
        name="win_attn",
    )(sink, q, q, k, vext)


def _out_proj_kernel(x_ref, oa_ref, ob_ref, oc_ref, wa_ref, wb_ref, wc_ref, g_ref, rwt_ref, rb_ref,
                     x1_ref, hm_ref, idx_ref, gate_ref, rank_ref, cnt_ref, run_ref):
    @pl.when(pl.program_id(0) == 0)
    def _():
        run_ref[...] = jnp.zeros_like(run_ref)

    x1 = (x_ref[...]
          + jnp.dot(oa_ref[...], wa_ref[...], preferred_element_type=F32)
          + jnp.dot(ob_ref[...], wb_ref[...], preferred_element_type=F32)
          + jnp.dot(oc_ref[...], wc_ref[...], preferred_element_type=F32))
    x1_ref[...] = x1
    hm = _rms(x1, g_ref[...])
    _store_token_tiles(hm_ref, hm)
    tm = hm.shape[0]

    logits = lax.dot_general(rwt_ref[...], hm, (((1,), (1,)), ((), ())),
                             precision=lax.Precision.HIGHEST, preferred_element_type=F32) + rb_ref[...]
    eidx = lax.broadcasted_iota(jnp.int32, (N_EXPERTS, tm), 0)
    vals, sels, hots = [], [], []
    cur = logits
    for _ in range(TOP_K):
        mx = jnp.max(cur, axis=0, keepdims=True)
        sel = jnp.min(jnp.where(cur == mx, eidx, N_EXPERTS), axis=0, keepdims=True)
        hot = eidx == sel
        vals.append(mx)
        sels.append(sel)
        hots.append(hot)
        cur = jnp.where(hot, -jnp.inf, cur)
    exps = [jnp.exp(v - vals[0]) for v in vals]
    denom = exps[0] + exps[1] + exps[2] + exps[3]
    gates = jnp.concatenate([e / denom for e in exps] + [jnp.zeros((LANES - TOP_K, tm), F32)], axis=0)
    gate_ref[...] = gates.T
    idx_ref[...] = jnp.concatenate(sels, axis=0)

    hot_all = jnp.zeros((N_EXPERTS, tm), F32)
    for hot in hots:
        hot_all = hot_all + jnp.where(hot, 1.0, 0.0)
    row = lax.broadcasted_iota(jnp.int32, (tm, tm), 0)
    col = lax.broadcasted_iota(jnp.int32, (tm, tm), 1)
    before = jnp.where(row < col, 1.0, 0.0).astype(BF16)
    rank_full = jnp.dot(hot_all.astype(BF16), before, preferred_element_type=F32) + run_ref[...]
    ranks = [jnp.sum(jnp.where(hot, rank_full, 0.0), axis=0, keepdims=True) for hot in hots]
    rank_ref[...] = jnp.concatenate(ranks, axis=0).astype(jnp.int32)
    run_ref[...] = run_ref[...] + jnp.sum(hot_all, axis=1, keepdims=True)
    cnt_ref[...] = jnp.broadcast_to(run_ref[...], (N_EXPERTS, LANES)).astype(jnp.int32)


def _out_proj(x, oa, ob, oc, wa, wb, wc, g, rwt, rb):
    tm = TM_OUT
    const = lambda i: (0, 0)
    row_blk = lambda w: pl.BlockSpec((tm, w), lambda i: (i, 0))
    col_blk = pl.BlockSpec((TOP_K, tm), lambda i: (0, i))
    return pl.pallas_call(
        _out_proj_kernel,
        grid=(TOKENS // tm,),
        in_specs=[
            row_blk(D_MODEL), row_blk(oa.shape[1]), row_blk(ob.shape[1]), row_blk(oc.shape[1]),
            pl.BlockSpec(wa.shape, const), pl.BlockSpec(wb.shape, const), pl.BlockSpec(wc.shape, const),
            pl.BlockSpec((1, D_MODEL), const),
            pl.BlockSpec((N_EXPERTS, D_MODEL), const),
            pl.BlockSpec((N_EXPERTS, 1), const),
        ],
        out_specs=[row_blk(D_MODEL), _token_tile_spec(tm, lambda i: i), col_blk, row_blk(LANES), col_blk,
                   pl.BlockSpec((N_EXPERTS, LANES), const)],
        out_shape=[
            jax.ShapeDtypeStruct((TOKENS, D_MODEL), F32),
            jax.ShapeDtypeStruct((TOKENS * ROW_TILES, LANES), F32),
            jax.ShapeDtypeStruct((TOP_K, TOKENS), jnp.int32),
            jax.ShapeDtypeStruct((TOKENS, LANES), F32),
            jax.ShapeDtypeStruct((TOP_K, TOKENS), jnp.int32),
            jax.ShapeDtypeStruct((N_EXPERTS, LANES), jnp.int32),
        ],
        scratch_shapes=[pltpu.VMEM((N_EXPERTS, 1), F32)],
        compiler_params=pltpu.CompilerParams(
            dimension_semantics=("arbitrary",), vmem_limit_bytes=VMEM_LIMIT),
        name="out_proj",
    )(x, oa, ob, oc, wa, wb, wc, g, rwt, rb)


def _issue_row_copies(pos_smem, tm, copy):
    for t in range(tm):
        for k in range(TOP_K):
            copy(t, k, pos_smem[k * (tm // LANES) + t // LANES, t % LANES])


def _issue_row_copies_looped(pos_smem, tm, copy):
    for chunk in range(tm // LANES):
        def issue(j, carry, chunk=chunk):
            for k in range(TOP_K):
                copy(chunk * LANES + j, k, pos_smem[k * (tm // LANES) + chunk, j])
            return carry
        lax.fori_loop(0, LANES, issue, 0, unroll=16)


def _dispatch_kernel(pad_start_ref, pad_count_ref, pos_hbm, hm_ref, xs_ref,
                     pos_smem, zero_ref, pos_sem, row_sems, pad_sem):
    i = pl.program_id(0)
    tm = hm_ref.shape[0] // ROW_TILES
    load = pltpu.make_async_copy(pos_hbm.at[i], pos_smem, pos_sem)
    load.start()

    @pl.when(i == 0)
    def _():
        zero_ref[...] = jnp.zeros_like(zero_ref)
        zero_row = _token_tile(zero_ref, 0)
        for e in range(N_EXPERTS):
            start = pad_start_ref[e]

            def fill(r, carry, start=start):
                pltpu.make_async_copy(zero_row, _token_tile(xs_ref, start + r), pad_sem).start()
                return carry

            def drain(r, carry):
                pltpu.make_async_copy(zero_row, _token_tile(xs_ref, 0), pad_sem).wait()
                return carry

            lax.fori_loop(0, pad_count_ref[e], fill, 0)
            lax.fori_loop(0, pad_count_ref[e], drain, 0)

        tail_start = pad_start_ref[N_EXPERTS]
        chunk = ZERO_ROWS * ROW_TILES

        def fill_tail(r, carry):
            dst = xs_ref.at[pl.ds(pl.multiple_of((tail_start + r * ZERO_ROWS) * ROW_TILES, chunk), chunk)]
            pltpu.make_async_copy(zero_ref, dst, pad_sem).start()
            return carry

        def drain_tail(r, carry):
            pltpu.make_async_copy(zero_ref, xs_ref.at[pl.ds(0, chunk)], pad_sem).wait()
            return carry

        lax.fori_loop(0, pad_count_ref[N_EXPERTS], fill_tail, 0)
        lax.fori_loop(0, pad_count_ref[N_EXPERTS], drain_tail, 0)

    load.wait()

    def copy(t, k, pos):
        pltpu.make_async_copy(_token_tile(hm_ref, t), _token_tile(xs_ref, pos),
                              row_sems.at[k]).start(priority=k % 2)

    _issue_row_copies_looped(pos_smem, tm, copy)
    for k in range(TOP_K):
        pltpu.make_async_copy(hm_ref, _token_tiles(xs_ref, tm), row_sems.at[k]).wait()


def _dispatch(pad_start, pad_count, pos_tiles, hm):
    tm = TM_DISPATCH
    grid_spec = pltpu.PrefetchScalarGridSpec(
        num_scalar_prefetch=2,
        grid=(TOKENS // tm,),
        in_specs=[
            pl.BlockSpec(memory_space=pl.ANY),
            _token_tile_spec(tm, lambda i, ps, pc: i),
        ],
        out_specs=pl.BlockSpec(memory_space=pl.ANY),
        scratch_shapes=[
            pltpu.SMEM((TOP_K * tm // LANES, LANES), jnp.int32),
            pltpu.VMEM((ZERO_ROWS * ROW_TILES, LANES), F32),
            pltpu.SemaphoreType.DMA,
            pltpu.SemaphoreType.DMA((TOP_K,)),
            pltpu.SemaphoreType.DMA,
        ],
    )
    return pl.pallas_call(
        _dispatch_kernel,
        grid_spec=grid_spec,
        out_shape=jax.ShapeDtypeStruct((N_SORTED_ROWS * ROW_TILES, LANES), F32),
        compiler_params=pltpu.CompilerParams(
            dimension_semantics=("arbitrary",), vmem_limit_bytes=VMEM_LIMIT, has_side_effects=True),
        name="dispatch",
    )(pad_start, pad_count, pos_tiles, hm)


def _experts_kernel(te_ref, ts_ref, nr_ref, nx_ref, xs_ref, wup_hbm, bup_ref, wdn_hbm, bdn_ref, ys_ref,
                    wup_f32_ref, wdn_f32_ref, wup_bf_ref, wdn_bf_ref, w_sems, *, layer):
    del ts_ref
    i = pl.program_id(0)
    n_rows = nr_ref[i]
    expert = te_ref[i]

    def weight_copies(e):
        return (pltpu.make_async_copy(wup_hbm.at[layer, e], wup_f32_ref, w_sems.at[0]),
                pltpu.make_async_copy(wdn_hbm.at[layer, e], wdn_f32_ref, w_sems.at[1]))

    @pl.when(i == 0)
    def _():
        for cp in weight_copies(expert):
            cp.start()

    @pl.when(jnp.logical_or(i == 0, expert != te_ref[jnp.maximum(i - 1, 0)]))
    def _():
        for cp in weight_copies(expert):
            cp.wait()
        wup_bf_ref[...] = wup_f32_ref[...].astype(BF16)
        wdn_bf_ref[...] = wdn_f32_ref[...].astype(BF16)

        @pl.when(nx_ref[i] >= 0)
        def _():
            for cp in weight_copies(nx_ref[i]):
                cp.start()

    @pl.when(n_rows > 0)
    def _():
        x = _load_token_tiles(xs_ref).astype(BF16)
        gu = jnp.dot(x, wup_bf_ref[...], preferred_element_type=F32) + bup_ref[0, 0]
        x_glu = jnp.minimum(gu[:, :D_FF], SWIGLU_LIMIT)
        x_lin = jnp.clip(gu[:, D_FF:], -SWIGLU_LIMIT, SWIGLU_LIMIT)
        act = x_glu * jax.nn.sigmoid(SWIGLU_ALPHA * x_glu) * (x_lin + 1.0)
        _store_token_tiles(
            ys_ref, jnp.dot(act.astype(BF16), wdn_bf_ref[...], preferred_element_type=F32) + bdn_ref[0, 0])

    @pl.when(n_rows == 0)
    def _():
        ys_ref[...] = jnp.zeros_like(ys_ref)


def _experts(layer, tile_expert, tile_src, tile_rows, next_expert, xs, wup, bup, wdn, bdn):
    tm = TM_EXPERT
    per_expert = lambda r, c: pl.BlockSpec((1, 1, r, c), lambda i, te, ts, nr, nx: (layer, te[i], 0, 0))
    grid_spec = pltpu.PrefetchScalarGridSpec(
        num_scalar_prefetch=4,
        grid=(N_EXPERT_TILES,),
        in_specs=[
            _token_tile_spec(tm, lambda i, te, ts, nr, nx: ts[i]),
            pl.BlockSpec(memory_space=pl.ANY), per_expert(1, 2 * D_FF),
            pl.BlockSpec(memory_space=pl.ANY), per_expert(1, D_MODEL),
        ],
        out_specs=_token_tile_spec(tm, lambda i, te, ts, nr, nx: i),
        scratch_shapes=[pltpu.VMEM((D_MODEL, 2 * D_FF), F32), pltpu.VMEM((D_FF, D_MODEL), F32),
                        pltpu.VMEM((D_MODEL, 2 * D_FF), BF16), pltpu.VMEM((D_FF, D_MODEL), BF16),
                        pltpu.SemaphoreType.DMA((2,))],
    )
    return pl.pallas_call(
        functools.partial(_experts_kernel, layer=layer),
        grid_spec=grid_spec,
        out_shape=jax.ShapeDtypeStruct((N_SORTED_ROWS * ROW_TILES, LANES), F32),
        compiler_params=pltpu.CompilerParams(
            dimension_semantics=("arbitrary",), vmem_limit_bytes=VMEM_LIMIT),
        name="experts",
    )(tile_expert, tile_src, tile_rows, next_expert, xs, wup, bup, wdn, bdn)


def _combine_kernel(pos_hbm, ys_hbm, x1_ref, gate_ref, p_ref, g_ref, wg_ref, wp_ref, gf_ref,
                    o_ref, pos_smem, rows_ref, pos_sem, row_sems, *, apply_final_norm):
    i = pl.program_id(0)
    tm = x1_ref.shape[0]
    slot = lax.rem(i, 2)

    def start_gather(tile, dst_slot):
        load = pltpu.make_async_copy(pos_hbm.at[tile], pos_smem, pos_sem)
        load.start()
        load.wait()

        def copy(t, k, pos):
            pltpu.make_async_copy(_token_tile(ys_hbm, pos), _token_tile(rows_ref.at[dst_slot, k], t),
                                  row_sems.at[dst_slot, k]).start(priority=k % 2)

        _issue_row_copies(pos_smem, tm, copy)

    @pl.when(i == 0)
    def _():
        start_gather(i, slot)

    @pl.when(i + 1 < pl.num_programs(0))
    def _():
        start_gather(i + 1, 1 - slot)

    for k in range(TOP_K):
        pltpu.make_async_copy(_token_tiles(ys_hbm, tm), rows_ref.at[slot, k], row_sems.at[slot, k]).wait()

    x2 = x1_ref[...]
    for k in range(TOP_K):
        x2 = x2 + _load_token_tiles(rows_ref.at[slot, k]) * gate_ref[:, k:k + 1]
    hp = _rms(x2, g_ref[...]).astype(BF16)
    gate = jax.nn.sigmoid(jnp.dot(hp, wg_ref[...], preferred_element_type=F32))
    pe = jnp.dot(p_ref[...].astype(BF16), wp_ref[...], preferred_element_type=F32)
    x3 = x2 + pe * gate
    if apply_final_norm:
        x3 = _rms(x3, gf_ref[...])
    o_ref[...] = x3


def _combine(pos_tiles, ys, x1, gates_t, p, g, wg, wp, gf, apply_final_norm):
    tm = TM_COMBINE
    const = lambda i: (0, 0)
    row_blk = lambda w: pl.BlockSpec((tm, w), lambda i: (i, 0))
    kernel = functools.partial(_combine_kernel, apply_final_norm=apply_final_norm)
    return pl.pallas_call(
        kernel,
        grid=(TOKENS // tm,),
        in_specs=[
            pl.BlockSpec(memory_space=pl.ANY),
            pl.BlockSpec(memory_space=pl.ANY),
            row_blk(D_MODEL), row_blk(LANES), row_blk(PLE_DIM),
            pl.BlockSpec((1, D_MODEL), const),
            pl.BlockSpec((D_MODEL, D_MODEL), const),
            pl.BlockSpec((PLE_DIM, D_MODEL), const),
            pl.BlockSpec((1, D_MODEL), const),
        ],
        out_specs=row_blk(D_MODEL),
        out_shape=jax.ShapeDtypeStruct((TOKENS, D_MODEL), F32),
        scratch_shapes=[
            pltpu.SMEM((TOP_K * tm // LANES, LANES), jnp.int32),
            pltpu.VMEM((2, TOP_K, tm * ROW_TILES, LANES), F32),
            pltpu.SemaphoreType.DMA,
            pltpu.SemaphoreType.DMA((2, TOP_K)),
        ],
        compiler_params=pltpu.CompilerParams(
            dimension_semantics=("arbitrary",), vmem_limit_bytes=VMEM_LIMIT),
        name="combine",
    )(pos_tiles, ys, x1, gates_t, p, g, wg, wp, gf)


def _in_proj_columns():
    src = np.full((Z_COLS,), -1, np.int64)
    src[Z_CQ:Z_CQ + Q_LORA] = np.arange(Q_LORA)
    src[Z_CKV:Z_CKV + KV_LORA] = Q_LORA + np.arange(KV_LORA)
    src[Z_KROPE + NOPE_A:Z_KROPE + NOPE_A + ROPE_A] = Q_LORA + KV_LORA + np.arange(ROPE_A)
    b0 = A_COLS
    for j in range(HB // 2):
        src[Z_BQ + j * LANES:Z_BQ + j * LANES + HALF] = b0 + j * HEAD_DIM + np.arange(HEAD_DIM)
        src[Z_BQ + j * LANES + HALF:Z_BQ + (j + 1) * LANES] = b0 + (HB // 2 + j) * HEAD_DIM + np.arange(HEAD_DIM)
    src[Z_BK:Z_BK + LANES] = b0 + HB * HEAD_DIM + np.arange(LANES)
    src[Z_BV:Z_BV + LANES] = b0 + (HB + KVB) * HEAD_DIM + np.arange(LANES)
    c0 = A_COLS + B_COLS
    for j in range(HC // 2):
        src[Z_CQS + j * LANES:Z_CQS + j * LANES + HALF] = c0 + j * HEAD_DIM + np.arange(HEAD_DIM)
        src[Z_CQS + j * LANES + HALF:Z_CQS + (j + 1) * LANES] = c0 + (HC // 2 + j) * HEAD_DIM + np.arange(HEAD_DIM)
    src[Z_CK:Z_CK + LANES] = c0 + HC * HEAD_DIM + np.arange(LANES)
    src[Z_CV:Z_CV + LANES] = c0 + (HC + KVC) * HEAD_DIM + np.arange(LANES)
    return src


def _wq_columns():
    src = np.full((HA * LANES,), -1, np.int64)
    dq = NOPE_A + ROPE_A
    for h in range(HA):
        src[h * LANES:h * LANES + dq] = h * dq + np.arange(dq)
    return src


def _wkv_columns():
    src = np.full((HA * LANES + (HA // 2) * LANES,), -1, np.int64)
    dkv = NOPE_A + V_A
    for h in range(HA):
        src[h * LANES:h * LANES + NOPE_A] = h * dkv + np.arange(NOPE_A)
        v0 = HA * LANES + (h // 2) * LANES + (h % 2) * HALF
        src[v0:v0 + V_A] = h * dkv + NOPE_A + np.arange(V_A)
    return src


def _paired_rows(base, n_heads):
    rows = []
    for j in range(n_heads // 2):
        rows.append(base + j * HEAD_DIM + np.arange(HEAD_DIM))
        rows.append(base + (n_heads // 2 + j) * HEAD_DIM + np.arange(HEAD_DIM))
    return np.concatenate(rows)


def _take_cols(w, src):
    cols = jnp.take(w, jnp.asarray(np.maximum(src, 0)), axis=1)
    return jnp.where(jnp.asarray(src >= 0)[None, :], cols, 0.0)


def _rope_tables():
    f32 = np.float32

    def cos_sin(pos, dim):
        inv = (f32(1.0) / (f32(ROPE_THETA) ** (np.arange(0, dim, 2, dtype=f32) / f32(dim)))).astype(f32)
        ang = pos.astype(f32)[:, None] * inv[None, :]
        return np.cos(ang).astype(f32), np.sin(ang).astype(f32)

    pos = np.arange(SEQ, dtype=np.int32)
    rows = pos // GRID_W
    cols = pos % GRID_W
    zeros = lambda w: np.zeros((SEQ, w), f32)
    ones = lambda w: np.ones((SEQ, w), f32)

    cos_a, sin_a = cos_sin(pos, ROPE_A)
    c_a = np.concatenate([ones(NOPE_A), cos_a, cos_a, ones(32)], axis=1)
    up_a = np.concatenate([zeros(NOPE_A), -sin_a, zeros(16), zeros(32)], axis=1)
    dn_a = np.concatenate([zeros(NOPE_A), zeros(16), sin_a, zeros(32)], axis=1)

    cos_r, sin_r = cos_sin(rows, HEAD_DIM // 2)
    cos_w, sin_w = cos_sin(cols, HEAD_DIM // 2)
    z16 = zeros(16)
    c_b = np.concatenate([cos_r, cos_r, cos_w, cos_w] * 2, axis=1)
    up_b = np.concatenate([-sin_r, z16, -sin_w, z16] * 2, axis=1)
    dn_b = np.concatenate([z16, sin_r, z16, sin_w] * 2, axis=1)

    cos_c, sin_c = cos_sin(pos, HEAD_DIM)
    z32 = zeros(32)
    c_c = np.concatenate([cos_c, cos_c] * 2, axis=1)
    up_c = np.concatenate([-sin_c, z32] * 2, axis=1)
    dn_c = np.concatenate([z32, sin_c] * 2, axis=1)
    return jnp.asarray(np.concatenate([c_a, up_a, dn_a, c_b, up_b, dn_b, c_c, up_c, dn_c], axis=1))


def _routing_tables(idx, rank, counts):
    tiles_per_expert = (counts + TM_EXPERT - 1) // TM_EXPERT
    tile_end = jnp.cumsum(tiles_per_expert)
    tile_start = tile_end - tiles_per_expert
    row_start = tile_start * TM_EXPERT
    experts = jnp.arange(N_EXPERTS, dtype=jnp.int32)
    pos = jnp.sum(jnp.where(idx[..., None] == experts, row_start, 0), axis=-1) + rank
    n_valid = tile_end[-1]
    tile_ids = jnp.arange(N_EXPERT_TILES, dtype=jnp.int32)
    tile_src = jnp.minimum(tile_ids, n_valid - 1)
    tile_expert = jnp.minimum(jnp.sum(tile_src[:, None] >= tile_end[None, :], axis=-1), N_EXPERTS - 1)
    hot = tile_expert[:, None] == experts
    rows_left = jnp.sum(jnp.where(hot, counts - (tile_src[:, None] - tile_start) * TM_EXPERT, 0), axis=-1)
    tile_rows = jnp.where(tile_ids < n_valid, jnp.clip(rows_left, 0, TM_EXPERT), 0)
    tail_start = n_valid * TM_EXPERT
    pad_start = jnp.concatenate([row_start + counts, tail_start[None]])
    pad_count = jnp.concatenate([tiles_per_expert * TM_EXPERT - counts,
                                 ((N_SORTED_ROWS - tail_start) // ZERO_ROWS)[None]])
    group_end = jnp.sum(jnp.where(hot, tile_end, 0), axis=-1)
    following = jnp.minimum(jnp.sum(group_end[:, None] >= tile_end[None, :], axis=-1), N_EXPERTS - 1)
    next_expert = jnp.where(group_end < n_valid, following, -1)
    tables = (tile_expert, tile_src, tile_rows, next_expert, pad_start, pad_count)
    return (pos,) + tuple(t.astype(jnp.int32) for t in tables)


def _pos_tiles(pos, tm):
    return pos.reshape(TOP_K, TOKENS // tm, tm).transpose(1, 0, 2).reshape(TOKENS // tm, TOP_K * tm // LANES, LANES)


def kernel(x, p, attn_norm, w_in, mla_q_norm, mla_wq_up, mla_kv_norm, mla_wkv_up, gqa_q_norm, gqa_k_norm,
           swa_sink, w_out, moe_norm, router_w, router_b, w_up, b_up, w_down, b_down, ple_norm, w_ple,
           w_ple_gate, final_norm):
    tabs = _rope_tables()
    in_cols, wq_cols, wkv_cols = _in_proj_columns(), _wq_columns(), _wkv_columns()
    rows_c = _paired_rows(HA * V_A + HB * HEAD_DIM, HC)

    xf = x.reshape(TOKENS, D_MODEL)
    for i in range(DEPTH):
        win = _take_cols(w_in[i], in_cols).astype(BF16)
        wq = _take_cols(mla_wq_up[i], wq_cols).astype(BF16)
        wkv = _take_cols(mla_wkv_up[i], wkv_cols).astype(BF16)
        gbq = jnp.tile(gqa_q_norm[i], 2)[None, :]
        gbk = jnp.tile(gqa_k_norm[i], 2)[None, :]
        qa, ka, va, qb, kb, vb, qc, kc, vc = _in_proj(
            xf.reshape(BATCH, SEQ, D_MODEL), attn_norm[i][None, :], win, mla_q_norm[i][None, :], wq,
            mla_kv_norm[i][None, :], wkv, gbq, gbk, tabs)

        oa = _dense_attention(qa, ka, va, HA, lambda h: h, lambda h: h, "attn_a")
        ob = _dense_attention(qb, kb, vb, HB, lambda h: h * 0, lambda h: h // (HB // KVB), "attn_b")
        oc = _window_attention(swa_sink[i], qc, kc, vc)

        wo = w_out[i]
        x1, hm, idx, gates_t, rank, cnt = _out_proj(
            xf, oa.reshape(TOKENS, -1), ob.reshape(TOKENS, -1), oc.reshape(TOKENS, -1),
            wo[:HA * V_A].astype(BF16), wo[HA * V_A:HA * V_A + HB * HEAD_DIM].astype(BF16),
            jnp.take(wo, jnp.asarray(rows_c), axis=0).astype(BF16),
            moe_norm[i][None, :], router_w[i].T, router_b[i][:, None])

        pos, tile_expert, tile_src, tile_rows, next_expert, pad_start, pad_count = _routing_tables(
            idx, rank, cnt[:, 0])
        xs = _dispatch(pad_start, pad_count, _pos_tiles(pos, TM_DISPATCH), hm)
        ys = _experts(i, tile_expert, tile_src, tile_rows, next_expert, xs, w_up, b_up[:, :, None, :],
                      w_down, b_down[:, :, None, :])
        xf = _combine(_pos_tiles(pos, TM_COMBINE), ys, x1, gates_t, p[i].reshape(TOKENS, PLE_DIM),
                      ple_norm[i][None, :], w_ple_gate[i].astype(BF16), w_ple[i].astype(BF16),
                      final_norm[None, :], apply_final_norm=(i == DEPTH - 1))
    return xf.reshape(BATCH, SEQ, D_MODEL)
```
